```python
import math
import jax
import jax.numpy as jnp
from jax import lax
import numpy as np

D_MODEL = 1024
BATCH = 8
SEQ = 4096
DEPTH = 4

HEAD_DIM = 64
N_MIXERS = 3
Q_BLOCK = 128
RMS_EPS = 1e-6
A_HEADS = 16
A_PATTERNS = ((128, 1), (512, 4), (2048, 16))
A_GROUPS = len(A_PATTERNS)
A_WIDTH = A_HEADS * HEAD_DIM
B_HEADS = D_MODEL // (2 * HEAD_DIM)
B_WIDTH = B_HEADS * 2 * HEAD_DIM
C_HEADS = D_MODEL // HEAD_DIM
C_WIDTH = C_HEADS * HEAD_DIM
MOE_GROUPS = 8
MOE_EXPERTS_PER_GROUP = 8
N_EXPERTS = MOE_GROUPS * MOE_EXPERTS_PER_GROUP
MOE_TOP_K = 2
D_EXPERT = D_MODEL // 2
MOE_CHUNK = 128
PLE_DIM = 256
N_A_LAYERS = (DEPTH + N_MIXERS - 1) // N_MIXERS
N_B_LAYERS = (DEPTH + N_MIXERS - 2) // N_MIXERS
N_C_LAYERS = (DEPTH + N_MIXERS - 3) // N_MIXERS

kernel_name = 'hybrid_dilated_diff_stickbreak_hmoe'


def rmsnorm(x, g):
    xf = x.astype(jnp.float32)
    y = xf * lax.rsqrt(jnp.mean(xf * xf, axis=-1, keepdims=True) + RMS_EPS)
    return y.astype(x.dtype) * g


def alibi_slopes(n):
    return jnp.asarray(2.0 ** (-8.0 * np.arange(1, n + 1) / n), dtype=jnp.float32)


def dilated_window_group(q, k, v, window, dilation, slopes):
    B, S, H, dh = q.shape
    r = dilation
    L = S // r
    n_back = window // r
    blk = Q_BLOCK
    assert n_back <= blk
    nb = -(-L // blk)
    Lp = nb * blk

    def to_streams(t):
        t = t.reshape(B, L, r, H, dh).transpose(0, 2, 3, 1, 4)
        t = jnp.pad(t, ((0, 0), (0, 0), (0, 0), (0, Lp - L), (0, 0)))
        return t.reshape(B, r, H, nb, blk, dh)

    def with_prev(t):
        prev = jnp.pad(t, ((0, 0),) * 3 + ((1, 0), (0, 0), (0, 0)))[:, :, :, :-1]
        return jnp.concatenate([prev, t], axis=4)

    qs = to_streams(q * (1.0 / math.sqrt(dh)))
    kk = with_prev(to_streams(k))
    vv = with_prev(to_streams(v))
    scores = jnp.einsum('brhnqd,brhnkd->brhnqk', qs, kk, preferred_element_type=jnp.float32)
    a = jnp.arange(blk)[:, None]
    b = jnp.arange(2 * blk)[None, :]
    delta = blk + a - b
    key_pos = (jnp.arange(nb)[:, None, None] - 1) * blk + b[None]
    valid = (delta >= 0) & (delta <= n_back) & (key_pos >= 0)
    bias = -slopes[:, None, None, None] * (delta * r).astype(jnp.float32)[None, None]
    scores = jnp.where(valid, scores + bias, -jnp.inf)
    m = jnp.max(scores, axis=-1, keepdims=True)
    e = jnp.exp(scores - m)
    s = jnp.sum(e, axis=-1, keepdims=True)
    o = jnp.einsum('brhnqk,brhnkd->brhnqd', e, vv) / s
    lse = (m + jnp.log(s))[..., 0]
    o = o.reshape(B, r, H, Lp, dh)[:, :, :, :L].transpose(0, 3, 1, 2, 4).reshape(B, S, H, dh)
    lse = lse.reshape(B, r, H, Lp)[:, :, :, :L].transpose(0, 3, 1, 2).reshape(B, S, H)
    return o, lse


def dilated_attention(x, w_in, w_out):
    B, S, _ = x.shape
    qkv = (x @ w_in).reshape(B, S, A_GROUPS, 3, A_HEADS, HEAD_DIM)
    slopes = alibi_slopes(A_HEADS)
    outs, lses = [], []
    for g, (window, dilation) in enumerate(A_PATTERNS):
        o, lse = dilated_window_group(qkv[:, :, g, 0], qkv[:, :, g, 1], qkv[:, :, g, 2],
                                      window, dilation, slopes)
        outs.append(o)
        lses.append(lse)
    w = jax.nn.softmax(jnp.stack(lses), axis=0)[..., None]
    o = jnp.sum(w * jnp.stack(outs), axis=0)
    return o.reshape(B, S, A_WIDTH).astype(x.dtype) @ w_out


def differential_attention(x, w_in, w_out, lq1, lk1, lq2, lk2, subln, lambda_init):
    B, S, _ = x.shape
    H, dh = B_HEADS, HEAD_DIM
    qkv = x @ w_in
    q = (qkv[..., :B_WIDTH] * (1.0 / math.sqrt(dh))).reshape(B, S, H, 2, dh)
    k = qkv[..., B_WIDTH:2 * B_WIDTH].reshape(B, S, H, 2, dh)
    v = qkv[..., 2 * B_WIDTH:].reshape(B, S, H, 2 * dh)
    lam = (jnp.exp(jnp.sum(lq1.astype(jnp.float32) * lk1.astype(jnp.float32)))
           - jnp.exp(jnp.sum(lq2.astype(jnp.float32) * lk2.astype(jnp.float32))) + lambda_init)
    slopes = alibi_slopes(H)
    nb = S // Q_BLOCK
    qb = q.reshape(B, nb, Q_BLOCK, H, 2, dh).transpose(1, 0, 2, 3, 4, 5)
    kpos = jnp.arange(S)

    def block(args):
        qblk, n = args
        sc = jnp.einsum('bqhcd,bkhcd->bhcqk', qblk, k, preferred_element_type=jnp.float32)
        dist = (n * Q_BLOCK + jnp.arange(Q_BLOCK))[:, None] - kpos[None, :]
        sc = sc - slopes[:, None, None, None] * dist.astype(jnp.float32)
        sc = jnp.where(dist >= 0, sc, -jnp.inf)
        pr = jax.nn.softmax(sc, axis=-1)
        attn = pr[:, :, 0] - lam * pr[:, :, 1]
        return jnp.einsum('bhqk,bkhe->bqhe', attn, v)

    o = lax.map(block, (qb, jnp.arange(nb)))
    o = o.transpose(1, 0, 2, 3, 4).reshape(B, S, H, 2 * dh)
    o = rmsnorm(o, subln) * (1.0 - lambda_init)
    return o.reshape(B, S, B_WIDTH).astype(x.dtype) @ w_out


def stick_breaking_attention(x, w_in, w_out):
    B, S, _ = x.shape
    H, dh = C_HEADS, HEAD_DIM
    qkv = x @ w_in
    q = (qkv[..., :C_WIDTH] * (1.0 / math.sqrt(dh))).reshape(B, S, H, dh)
    k = qkv[..., C_WIDTH:2 * C_WIDTH].reshape(B, S, H, dh)
    v = qkv[..., 2 * C_WIDTH:].reshape(B, S, H, dh)
    nb = S // Q_BLOCK
    qb = q.reshape(B, nb, Q_BLOCK, H, dh).transpose(1, 0, 2, 3, 4)
    kpos = jnp.arange(S)

    def block(args):
        qblk, n = args
        z = jnp.einsum('bqhd,bkhd->bhqk', qblk, k, preferred_element_type=jnp.float32)
        causal = kpos[None, :] < (n * Q_BLOCK + jnp.arange(Q_BLOCK))[:, None]
        log_beta = jax.nn.log_sigmoid(z)
        log_rest = jnp.where(causal, jax.nn.log_sigmoid(-z), 0.0)
        tail = lax.cumsum(log_rest, axis=3, reverse=True) - log_rest
        a = jnp.where(causal, jnp.exp(log_beta + tail), 0.0)
        return jnp.einsum('bhqk,bkhd->bqhd', a, v)

    o = lax.map(block, (qb, jnp.arange(nb)))
    o = o.transpose(1, 0, 2, 3, 4).reshape(B, S, C_WIDTH)
    return o.astype(x.dtype) @ w_out


def dropless_expert_ffn(xf, experts, gates, w_gate_up, w_down):
    T, D = xf.shape
    ne = w_gate_up.shape[0]
    C = MOE_CHUNK
    n_assign = experts.shape[0] * experts.shape[1]
    flat_e = experts.reshape(-1)
    order = jnp.argsort(flat_e)
    sorted_e = flat_e[order]
    counts = jnp.bincount(flat_e, length=ne)
    padded = (counts + C - 1) // C * C
    pad_end = jnp.cumsum(padded)
    pad_start = pad_end - padded
    start = jnp.cumsum(counts) - counts
    dest = pad_start[sorted_e] + jnp.arange(n_assign) - start[sorted_e]
    n_chunks = (n_assign + ne * (C - 1) + C - 1) // C
    P = n_chunks * C
    row_token = jnp.full((P,), T, jnp.int32).at[dest].set((order // MOE_TOP_K).astype(jnp.int32))
    row_gate = jnp.zeros((P,), jnp.float32).at[dest].set(gates.reshape(-1)[order])
    chunk_expert = jnp.minimum(jnp.searchsorted(pad_end, jnp.arange(n_chunks) * C, side='right'), ne - 1)
    x_rows = jnp.concatenate([xf, jnp.zeros((1, D), xf.dtype)], axis=0)[row_token].reshape(n_chunks, C, D)

    def chunk(args):
        xc, e = args
        gu = xc @ w_gate_up[e]
        return (jax.nn.silu(gu[:, :D_EXPERT]) * gu[:, D_EXPERT:]) @ w_down[e]

    y_rows = lax.map(chunk, (x_rows, chunk_expert)).reshape(P, D)
    y_rows = y_rows * row_gate[:, None].astype(y_rows.dtype)
    return jax.ops.segment_sum(y_rows, row_token, num_segments=T + 1)[:T]


def hierarchical_moe(x, w_coarse, b_coarse, w_fine, b_fine, w_gate_up, w_down):
    B, S, D = x.shape
    xf = x.reshape(-1, D)
    T = xf.shape[0]
    coarse = jax.nn.softmax((xf @ w_coarse + b_coarse).astype(jnp.float32), axis=-1)
    g_prob, g_idx = lax.top_k(coarse, 1)
    fine_logits = (xf @ w_fine + b_fine).astype(jnp.float32).reshape(T, MOE_GROUPS, MOE_EXPERTS_PER_GROUP)
    fine_sel = fine_logits[jnp.arange(T), g_idx[:, 0]]
    f_prob, f_idx = lax.top_k(jax.nn.softmax(fine_sel, axis=-1), MOE_TOP_K)
    f_prob = f_prob / jnp.sum(f_prob, axis=-1, keepdims=True)
    gates = g_prob * f_prob
    experts = g_idx * MOE_EXPERTS_PER_GROUP + f_idx
    y = dropless_expert_ffn(xf, experts, gates, w_gate_up, w_down)
    return y.reshape(B, S, D)


def setup_inputs(seed: int = 0) -> dict:
    key = jax.random.key(seed)
    ks = jax.random.split(key, 25)
    f32 = jnp.float32

    def nrm(k, shape, fan_in):
        return jax.random.normal(k, shape, f32) * fan_in ** -0.5

    def gain(k, shape):
        return 1.0 + 0.01 * jax.random.normal(k, shape, f32)

    D = D_MODEL
    return {
        'x': jax.random.normal(ks[0], (BATCH, SEQ, D), f32),
        'p': jax.random.normal(ks[1], (DEPTH, BATCH, SEQ, PLE_DIM), f32),
        'norm_mix': gain(ks[2], (DEPTH, D)),
        'norm_ffn': gain(ks[3], (DEPTH, D)),
        'norm_ple': gain(ks[4], (DEPTH, D)),
        'norm_final': gain(ks[5], (D,)),
        'a_w_in': nrm(ks[6], (N_A_LAYERS, D, A_GROUPS * 3 * A_WIDTH), D),
        'a_w_out': nrm(ks[7], (N_A_LAYERS, A_WIDTH, D), A_WIDTH),
        'b_w_in': nrm(ks[8], (N_B_LAYERS, D, 3 * B_WIDTH), D),
        'b_w_out': nrm(ks[9], (N_B_LAYERS, B_WIDTH, D), B_WIDTH),
        'b_lambda_q1': 0.1 * jax.random.normal(ks[10], (N_B_LAYERS, HEAD_DIM), f32),
        'b_lambda_k1': 0.1 * jax.random.normal(ks[11], (N_B_LAYERS, HEAD_DIM), f32),
        'b_lambda_q2': 0.1 * jax.random.normal(ks[12], (N_B_LAYERS, HEAD_DIM), f32),
        'b_lambda_k2': 0.1 * jax.random.normal(ks[13], (N_B_LAYERS, HEAD_DIM), f32),
        'b_subln': gain(ks[14], (N_B_LAYERS, 2 * HEAD_DIM)),
        'c_w_in': nrm(ks[15], (N_C_LAYERS, D, 3 * C_WIDTH), D),
        'c_w_out': nrm(ks[16], (N_C_LAYERS, C_WIDTH, D), C_WIDTH),
        'moe_w_coarse': nrm(ks[17], (DEPTH, D, MOE_GROUPS), D),
        'moe_b_coarse': 0.01 * jax.random.normal(ks[18], (DEPTH, MOE_GROUPS), f32),
        'moe_w_fine': nrm(ks[19], (DEPTH, D, N_EXPERTS), D),
        'moe_b_fine': 0.01 * jax.random.normal(ks[20], (DEPTH, N_EXPERTS), f32),
        'moe_w_gate_up': nrm(ks[21], (DEPTH, N_EXPERTS, D, 2 * D_EXPERT), D),
        'moe_w_down': nrm(ks[22], (DEPTH, N_EXPERTS, D_EXPERT, D), D_EXPERT),
        'ple_w_proj': nrm(ks[23], (DEPTH, PLE_DIM, D), PLE_DIM),
        'ple_w_gate': nrm(ks[24], (DEPTH, D, D), D),
    }


def reference(x, p, norm_mix, norm_ffn, norm_ple, norm_final, a_w_in, a_w_out,
              b_w_in, b_w_out, b_lambda_q1, b_lambda_k1, b_lambda_q2, b_lambda_k2, b_subln,
              c_w_in, c_w_out, moe_w_coarse, moe_b_coarse, moe_w_fine, moe_b_fine,
              moe_w_gate_up, moe_w_down, ple_w_proj, ple_w_gate):
    h = x
    for i in range(DEPTH):
        kind = i % N_MIXERS
        j = i // N_MIXERS
        hn = rmsnorm(h, norm_mix[i])
        if kind == 0:
            mix = dilated_attention(hn, a_w_in[j], a_w_out[j])
        elif kind == 1:
            lambda_init = 0.8 - 0.6 * math.exp(-0.3 * i)
            mix = differential_attention(hn, b_w_in[j], b_w_out[j], b_lambda_q1[j], b_lambda_k1[j],
                                         b_lambda_q2[j], b_lambda_k2[j], b_subln[j], lambda_init)
        else:
            mix = stick_breaking_attention(hn, c_w_in[j], c_w_out[j])
        h = h + mix
        h = h + hierarchical_moe(rmsnorm(h, norm_ffn[i]), moe_w_coarse[i], moe_b_coarse[i],
                                 moe_w_fine[i], moe_b_fine[i], moe_w_gate_up[i], moe_w_down[i])
        gate = jax.nn.sigmoid(rmsnorm(h, norm_ple[i]) @ ple_w_gate[i])
        h = h + (p[i] @ ple_w_proj[i]) * gate
    return rmsnorm(h, norm_final)
```

```python
import functools
import math

import jax
import jax.numpy as jnp
import numpy as np
from jax import lax
from jax.experimental import pallas as pl
from jax.experimental.pallas import tpu as pltpu

F32 = jnp.float32
BF16 = jnp.bfloat16

HEAD_DIM = 64
N_MIXERS = 3
ATTN_BLOCK = 128
RMS_EPS = 1e-6
A_HEADS = 16
A_PATTERNS = ((128, 1), (512, 4), (2048, 16))
MOE_GROUPS = 8
MOE_EXPERTS_PER_GROUP = 8
MOE_TOP_K = 2
MOE_CHUNK = 128
LANES = 128
ROUTER_LANES = 128
F32_EXP_ZERO = -104.0
VMEM_LIMIT = 48 * 1024 * 1024


def _params(sem, vmem=VMEM_LIMIT):
    return pltpu.CompilerParams(dimension_semantics=sem, vmem_limit_bytes=vmem)


def _rms(x):
    return x * lax.rsqrt(jnp.mean(x * x, axis=-1, keepdims=True) + RMS_EPS)


def _dot_t(a, b):
    return lax.dot_general(a, b, (((1,), (1,)), ((), ())), preferred_element_type=F32)


def _dot(a, b):
    return jnp.dot(a, b, preferred_element_type=F32)


def _split_bf16(x):
    hi = x.astype(BF16)
    lo = (x - hi.astype(F32)).astype(BF16)
    return hi, lo


def _norm_mm_kernel(x_ref, g_ref, w_ref, o_ref, xn_ref):
    @pl.when(pl.program_id(1) == 0)
    def _():
        xn_ref[...] = (_rms(x_ref[...]) * g_ref[...]).astype(BF16)

    o_ref[...] = _dot(xn_ref[...], w_ref[...]).astype(o_ref.dtype)


def _norm_mm(x, g, w, *, tm, tn, out_dtype=BF16):
    T, D = x.shape
    N = w.shape[1]
    tm = min(tm, T)
    tn = min(tn, N)
    return pl.pallas_call(
        _norm_mm_kernel,
        grid=(T // tm, N // tn),
        in_specs=[
            pl.BlockSpec((tm, D), lambda i, j: (i, 0)),
            pl.BlockSpec((1, D), lambda i, j: (0, 0)),
            pl.BlockSpec((D, tn), lambda i, j: (0, j)),
        ],
        out_specs=pl.BlockSpec((tm, tn), lambda i, j: (i, j)),
        out_shape=jax.ShapeDtypeStruct((T, N), out_dtype),
        scratch_shapes=[pltpu.VMEM((tm, D), BF16)],
        compiler_params=_params(("parallel", "arbitrary")),
        name="norm_proj",
    )(x, g.reshape(1, D), w)


def _mm_res_kernel(a_ref, w_ref, r_ref, o_ref):
    o_ref[...] = r_ref[...] + _dot(a_ref[...], w_ref[...])


def _mm_res(a, w, res, *, tm):
    T, K = a.shape
    N = w.shape[1]
    tm = min(tm, T)
    return pl.pallas_call(
        _mm_res_kernel,
        grid=(T // tm,),
        in_specs=[
            pl.BlockSpec((tm, K), lambda i: (i, 0)),
            pl.BlockSpec((K, N), lambda i: (0, 0)),
            pl.BlockSpec((tm, N), lambda i: (i, 0)),
        ],
        out_specs=pl.BlockSpec((tm, N), lambda i: (i, 0)),
        out_shape=jax.ShapeDtypeStruct((T, N), F32),
        compiler_params=_params(("parallel",)),
        name="out_proj",
    )(a, w, res)


def _attn_a_kernel(q_ref, kp_ref, kc_ref, vp_ref, vc_ref, o_ref, lse_ref, *, dilation, heads):
    n = pl.program_id(2)
    blk = ATTN_BLOCK
    qi = lax.broadcasted_iota(jnp.int32, (blk, blk), 0)
    ki = lax.broadcasted_iota(jnp.int32, (blk, blk), 1)
    d_cur = qi - ki
    d_prev = d_cur + blk
    ok_cur = d_cur >= 0
    ok_prev = jnp.logical_and(d_prev <= blk, n > 0)
    tok_cur = d_cur.astype(F32) * float(dilation)
    tok_prev = d_prev.astype(F32) * float(dilation)
    lane = lax.broadcasted_iota(jnp.int32, (1, LANES), 1)
    first_head = lane < HEAD_DIM
    lse_lane = lax.broadcasted_iota(jnp.int32, (blk, LANES), 1)
    lse_all = jnp.zeros((blk, LANES), F32)
    for pair in range(heads // 2):
        sl = slice(LANES * pair, LANES * (pair + 1))
        q = q_ref[:, sl]
        kp, kc = kp_ref[:, sl], kc_ref[:, sl]
        vp, vc = vp_ref[:, sl], vc_ref[:, sl]
        outs = []
        for sub in range(2):
            h = 2 * pair + sub
            slope = 2.0 ** (-8.0 * (h + 1) / heads)
            mine = first_head if sub == 0 else jnp.logical_not(first_head)
            qm = jnp.where(mine, q, jnp.zeros_like(q))
            s_cur = jnp.where(ok_cur, _dot_t(qm, kc) - slope * tok_cur, -jnp.inf)
            s_prev = jnp.where(ok_prev, _dot_t(qm, kp) - slope * tok_prev, -jnp.inf)
            m = jnp.maximum(jnp.max(s_cur, axis=-1, keepdims=True),
                            jnp.max(s_prev, axis=-1, keepdims=True))
            e_cur = jnp.exp(s_cur - m)
            e_prev = jnp.exp(s_prev - m)
            den = jnp.sum(e_cur, axis=-1, keepdims=True) + jnp.sum(e_prev, axis=-1, keepdims=True)
            o = _dot(e_cur.astype(BF16), vc) + _dot(e_prev.astype(BF16), vp)
            outs.append(o / den)
            lse_all = jnp.where(lse_lane == h, m + jnp.log(den), lse_all)
        o_ref[:, sl] = jnp.where(first_head, outs[0], outs[1]).astype(o_ref.dtype)
    lse_ref[...] = lse_all


def _attn_a_group(qkv, *, batch, seq, group, dilation, heads):
    width = heads * HEAD_DIM
    total = qkv.shape[1]
    per_row = total // width
    r = dilation
    L = seq // r
    assert L % ATTN_BLOCK == 0 and seq % r == 0
    nb = L // ATTN_BLOCK
    x = qkv.reshape(batch, L, r * total)

    def col(which):
        return lambda b, c, n: (b, n, c * per_row + group * 3 + which)

    def col_prev(which):
        return lambda b, c, n: (b, jnp.maximum(n - 1, 0), c * per_row + group * 3 + which)

    blk = (None, ATTN_BLOCK, width)
    o, lse = pl.pallas_call(
        functools.partial(_attn_a_kernel, dilation=r, heads=heads),
        grid=(batch, r, nb),
        in_specs=[
            pl.BlockSpec(blk, col(0)),
            pl.BlockSpec(blk, col_prev(1)),
            pl.BlockSpec(blk, col(1)),
            pl.BlockSpec(blk, col_prev(2)),
            pl.BlockSpec(blk, col(2)),
        ],
        out_specs=[
            pl.BlockSpec(blk, lambda b, c, n: (b, n, c)),
            pl.BlockSpec((None, ATTN_BLOCK, LANES), lambda b, c, n: (b, n, c)),
        ],
        out_shape=[
            jax.ShapeDtypeStruct((batch, L, r * width), BF16),
            jax.ShapeDtypeStruct((batch, L, r * LANES), F32),
        ],
        compiler_params=_params(("parallel", "parallel", "arbitrary")),
        name=f"dilated_attn_r{r}",
    )(x, x, x, x, x)
    return o.reshape(batch * seq, width), lse.reshape(batch * seq, LANES)


def _merge_proj_kernel(o0_ref, o1_ref, o2_ref, l0_ref, l1_ref, l2_ref, e_ref, w_ref, r_ref, out_ref):
    l0, l1, l2 = l0_ref[...], l1_ref[...], l2_ref[...]
    mx = jnp.maximum(jnp.maximum(l0, l1), l2)
    e0, e1, e2 = jnp.exp(l0 - mx), jnp.exp(l1 - mx), jnp.exp(l2 - mx)
    inv = 1.0 / (e0 + e1 + e2)
    expand = e_ref[...]
    acc = None
    for e, o_ref in ((e0, o0_ref), (e1, o1_ref), (e2, o2_ref)):
        hi, lo = _split_bf16(e * inv)
        w_full = _dot(hi, expand) + _dot(lo, expand)
        term = w_full * o_ref[...].astype(F32)
        acc = term if acc is None else acc + term
    out_ref[...] = r_ref[...] + _dot(acc.astype(BF16), w_ref[...])


def _merge_proj(outs, lses, w_out, res, *, heads, tm):
    T, width = outs[0].shape
    N = w_out.shape[1]
    tm = min(tm, T)
    expand = np.zeros((LANES, width), np.float32)
    for h in range(heads):
        expand[h, h * HEAD_DIM:(h + 1) * HEAD_DIM] = 1.0
    expand = jnp.asarray(expand, BF16)
    row = lambda i: (i, 0)
    fixed = lambda i: (0, 0)
    return pl.pallas_call(
        _merge_proj_kernel,
        grid=(T // tm,),
        in_specs=[pl.BlockSpec((tm, width), row)] * 3 + [pl.BlockSpec((tm, LANES), row)] * 3 + [
            pl.BlockSpec((LANES, width), fixed),
            pl.BlockSpec((width, N), fixed),
            pl.BlockSpec((tm, N), row),
        ],
        out_specs=pl.BlockSpec((tm, N), row),
        out_shape=jax.ShapeDtypeStruct((T, N), F32),
        compiler_params=_params(("parallel",)),
        name="merge_out_proj",
    )(*outs, *lses, expand, w_out, res)


def _attn_b_kernel(slope_ref, q_ref, k_ref, v_ref, lq1_ref, lk1_ref, lq2_ref, lk2_ref, sub_ref,
                   o_ref, *, tq, tk, lambda_init):
    h = pl.program_id(1)
    i = pl.program_id(2)
    slope = slope_ref[h]
    q = q_ref[...]
    lane = lax.broadcasted_iota(jnp.int32, (1, LANES), 1)
    first = lane < HEAD_DIM
    zero = jnp.zeros_like(q)
    qs = (jnp.where(first, q, zero), jnp.where(first, zero, q))
    rel = lax.broadcasted_iota(jnp.int32, (tq, tk), 0) - lax.broadcasted_iota(jnp.int32, (tq, tk), 1)

    def body(j, carry):
        start = pl.multiple_of(j * tk, tk)
        k = k_ref[pl.ds(start, tk), :]
        v = v_ref[pl.ds(start, tk), :]
        dist = rel + (i * tq - j * tk)
        ok = dist >= 0
        bias = slope * dist.astype(F32)
        new = []
        for c in range(2):
            m, l, acc = carry[3 * c:3 * c + 3]
            s = jnp.where(ok, _dot_t(qs[c], k) - bias, -jnp.inf)
            m_new = jnp.maximum(m, jnp.max(s, axis=-1, keepdims=True))
            alpha = jnp.exp(m - m_new)
            p = jnp.exp(s - m_new)
            l = alpha * l + jnp.sum(p, axis=-1, keepdims=True)
            acc = alpha * acc + _dot(p.astype(BF16), v)
            new += [m_new, l, acc]
        return tuple(new)

    init = (jnp.full((tq, 1), -jnp.inf, F32), jnp.zeros((tq, 1), F32), jnp.zeros((tq, LANES), F32)) * 2
    n_blocks = (i * tq + tq + tk - 1) // tk
    m1, l1, a1, m2, l2, a2 = lax.fori_loop(0, n_blocks, body, init)
    lam = (jnp.exp(jnp.sum(lq1_ref[...] * lk1_ref[...], axis=-1, keepdims=True))
           - jnp.exp(jnp.sum(lq2_ref[...] * lk2_ref[...], axis=-1, keepdims=True)) + lambda_init)
    o = a1 / l1 - lam * (a2 / l2)
    o = _rms(o) * sub_ref[...] * (1.0 - lambda_init)
    o_ref[...] = o.astype(o_ref.dtype)


def _attn_b(qkv, lq1, lk1, lq2, lk2, subln, *, batch, seq, heads, lambda_init, tq, tk):
    tq = min(tq, seq)
    tk = min(tk, seq)
    width = 2 * HEAD_DIM
    slopes = jnp.asarray(2.0 ** (-8.0 * np.arange(1, heads + 1) / heads), F32)
    vec = lambda a: a.reshape(1, -1).astype(F32)
    small = lambda n: pl.BlockSpec((1, n), lambda b, h, i, s: (0, 0))
    return pl.pallas_call(
        functools.partial(_attn_b_kernel, tq=tq, tk=tk, lambda_init=lambda_init),
        grid_spec=pltpu.PrefetchScalarGridSpec(
            num_scalar_prefetch=1,
            grid=(batch, heads, seq // tq),
            in_specs=[
                pl.BlockSpec((None, tq, width), lambda b, h, i, s: (b, i, h)),
                pl.BlockSpec((None, seq, width), lambda b, h, i, s: (b, 0, heads + h)),
                pl.BlockSpec((None, seq, width), lambda b, h, i, s: (b, 0, 2 * heads + h)),
                small(HEAD_DIM), small(HEAD_DIM), small(HEAD_DIM), small(HEAD_DIM), small(width),
            ],
            out_specs=pl.BlockSpec((None, tq, width), lambda b, h, i, s: (b, i, h)),
        ),
        out_shape=jax.ShapeDtypeStruct((batch, seq, heads * width), BF16),
        compiler_params=_params(("parallel", "parallel", "arbitrary")),
        name="diff_attn",
    )(slopes, qkv, qkv, qkv, vec(lq1), vec(lk1), vec(lq2), vec(lk2), vec(subln))


def _attn_c_kernel(q_ref, k_ref, v_ref, u_ref, o_ref, *, tq):
    i = pl.program_id(2)
    kb_w = LANES
    q = q_ref[...]
    lane = lax.broadcasted_iota(jnp.int32, (1, LANES), 1)
    first = lane < HEAD_DIM
    zero = jnp.zeros_like(q)
    qs = (jnp.where(first, q, zero), jnp.where(first, zero, q))
    upper = u_ref[...]
    rel = lax.broadcasted_iota(jnp.int32, (tq, kb_w), 0) - lax.broadcasted_iota(jnp.int32, (tq, kb_w), 1)

    def cond(carry):
        return jnp.logical_and(carry[0] >= 0, carry[1] > 0)

    def body(carry):
        kb, _, c0, c1, acc0, acc1 = carry
        start = pl.multiple_of(kb * kb_w, kb_w)
        k = k_ref[pl.ds(start, kb_w), :]
        v = v_ref[pl.ds(start, kb_w), :]
        causal = (rel + (i * tq - kb * kb_w)) > 0
        new_c, new_acc = [], []
        for qh, c, acc in ((qs[0], c0, acc0), (qs[1], c1, acc1)):
            z = _dot_t(qh, k)
            softplus = jnp.maximum(z, 0.0) + jnp.log(1.0 + jnp.exp(-jnp.abs(z)))
            log_rest = jnp.where(causal, -softplus, 0.0)
            hi, lo = _split_bf16(log_rest)
            tail = c + _dot(hi, upper) + _dot(lo, upper)
            a = jnp.where(causal, jnp.exp(z - softplus + tail), 0.0)
            new_acc.append(acc + _dot(a.astype(BF16), v))
            new_c.append(c + jnp.sum(log_rest, axis=-1, keepdims=True))
        alive = jnp.max(jnp.maximum(new_c[0], new_c[1])) > F32_EXP_ZERO
        return kb - 1, alive.astype(jnp.int32), new_c[0], new_c[1], new_acc[0], new_acc[1]

    last_kb = (i + 1) * (tq // kb_w) - 1
    zc = jnp.zeros((tq, 1), F32)
    za = jnp.zeros((tq, LANES), F32)
    out = lax.while_loop(cond, body, (last_kb, jnp.int32(1), zc, zc, za, za))
    o_ref[...] = jnp.where(first, out[4], out[5]).astype(o_ref.dtype)


def _attn_c(qkv, *, batch, seq, heads, tq):
    tq = min(tq, seq)
    pairs = heads // 2
    upper = jnp.asarray(np.tril(np.ones((LANES, LANES), np.float32), -1), BF16)
    return pl.pallas_call(
        functools.partial(_attn_c_kernel, tq=tq),
        grid=(batch, pairs, seq // tq),
        in_specs=[
            pl.BlockSpec((None, tq, LANES), lambda b, h, i: (b, i, h)),
            pl.BlockSpec((None, seq, LANES), lambda b, h, i: (b, 0, pairs + h)),
            pl.BlockSpec((None, seq, LANES), lambda b, h, i: (b, 0, 2 * pairs + h)),
            pl.BlockSpec((LANES, LANES), lambda b, h, i: (0, 0)),
        ],
        out_specs=pl.BlockSpec((None, tq, LANES), lambda b, h, i: (b, i, h)),
        out_shape=jax.ShapeDtypeStruct((batch, seq, heads * HEAD_DIM), BF16),
        compiler_params=_params(("parallel", "parallel", "arbitrary")),
        name="stick_breaking_attn",
    )(qkv, qkv, qkv, upper)


def _router_kernel(x_ref, g_ref, whi_ref, wlo_ref, b_ref, o_ref):
    xn = _rms(x_ref[...]) * g_ref[...]
    hi, lo = _split_bf16(xn)
    whi = whi_ref[...]
    o_ref[...] = _dot(hi, whi) + _dot(lo, whi) + _dot(hi, wlo_ref[...]) + b_ref[...]


def _router(h, g, w_coarse, b_coarse, w_fine, b_fine, *, tm):
    T, D = h.shape
    tm = min(tm, T)
    n_log = w_coarse.shape[1] + w_fine.shape[1]
    w = jnp.pad(jnp.concatenate([w_coarse, w_fine], axis=1), ((0, 0), (0, ROUTER_LANES - n_log)))
    b = jnp.pad(jnp.concatenate([b_coarse, b_fine]), (0, ROUTER_LANES - n_log)).reshape(1, ROUTER_LANES)
    whi, wlo = _split_bf16(w)
    fixed = lambda i: (0, 0)
    return pl.pallas_call(
        _router_kernel,
        grid=(T // tm,),
        in_specs=[
            pl.BlockSpec((tm, D), lambda i: (i, 0)),
            pl.BlockSpec((1, D), fixed),
            pl.BlockSpec((D, ROUTER_LANES), fixed),
            pl.BlockSpec((D, ROUTER_LANES), fixed),
            pl.BlockSpec((1, ROUTER_LANES), fixed),
        ],
        out_specs=pl.BlockSpec((tm, ROUTER_LANES), lambda i: (i, 0)),
        out_shape=jax.ShapeDtypeStruct((T, ROUTER_LANES), F32),
        compiler_params=_params(("parallel",)),
        name="moe_router",
    )(h, g.reshape(1, D), whi, wlo, b)


def _route(logits, n_tokens):
    G, E, K, C = MOE_GROUPS, MOE_EXPERTS_PER_GROUP, MOE_TOP_K, MOE_CHUNK
    ne = G * E
    T = n_tokens
    coarse = jax.nn.softmax(logits[:, :G], axis=-1)
    g_prob, g_idx = lax.top_k(coarse, 1)
    fine = logits[:, G:G + ne].reshape(T, G, E)
    fine_sel = jnp.take_along_axis(fine, g_idx[:, :, None], axis=1)[:, 0]
    f_prob, f_idx = lax.top_k(jax.nn.softmax(fine_sel, axis=-1), K)
    f_prob = f_prob / jnp.sum(f_prob, axis=-1, keepdims=True)
    gates = (g_prob * f_prob).reshape(-1)
    flat_e = (g_idx * E + f_idx).reshape(-1).astype(jnp.int32)
    n_assign = T * K
    order = jnp.argsort(flat_e).astype(jnp.int32)
    sorted_e = flat_e[order]
    counts = jnp.bincount(flat_e, length=ne).astype(jnp.int32)
    padded = (counts + C - 1) // C * C
    pad_end = jnp.cumsum(padded)
    pad_start = pad_end - padded
    start = jnp.cumsum(counts) - counts
    dest = pad_start[sorted_e] + jnp.arange(n_assign, dtype=jnp.int32) - start[sorted_e]
    n_chunks = (n_assign + ne * (C - 1) + C - 1) // C
    P = n_chunks * C
    slot = jnp.arange(P, dtype=jnp.int32)
    dump = n_assign + (slot // C % 2) * C + slot % C
    row_token = jnp.zeros((P,), jnp.int32).at[dest].set(order // K)
    row_dest = dump.at[dest].set(order)
    row_gate = jnp.zeros((P,), F32).at[dest].set(gates[order])
    n_used = (pad_end[-1] // C).astype(jnp.int32)
    chunk_start = jnp.minimum(jnp.arange(n_chunks, dtype=jnp.int32), n_used - 1) * C
    chunk_expert = jnp.minimum(jnp.searchsorted(pad_end, chunk_start, side='right'), ne - 1).astype(jnp.int32)
    return (chunk_expert, n_used.reshape(1), row_token.reshape(n_chunks, 1, C),
            row_dest.reshape(n_chunks, 1, C), row_gate.reshape(n_chunks, C, 1))


def _moe_kernel(ce_ref, nu_ref, tok_ref, tokn_ref, dst_ref, gate_ref, g_ref, wgu_ref, wdn_ref,
                h_hbm, y_hbm, hbuf, ybuf, wgu_bf, wdn_bf, sem_in, sem_out, *, d_expert):
    c = pl.program_id(0)
    n_used = nu_ref[0]
    C = MOE_CHUNK
    slot = c % 2

    def gather(tokens, s):
        def issue(r, _):
            pltpu.make_async_copy(h_hbm.at[pl.ds(tokens[0, 0, r], 1)], hbuf.at[s, pl.ds(r, 1)],
                                  sem_in.at[s]).start()
            return 0
        lax.fori_loop(0, C, issue, 0, unroll=8)

    def wait_gather(s):
        pltpu.make_async_copy(h_hbm.at[pl.ds(0, C)], hbuf.at[s], sem_in.at[s]).wait()

    def scatter(s):
        def issue(r, _):
            pltpu.make_async_copy(ybuf.at[s, pl.ds(r, 1)], y_hbm.at[pl.ds(dst_ref[0, 0, r], 1)],
                                  sem_out.at[s]).start()
            return 0
        lax.fori_loop(0, C, issue, 0, unroll=8)

    def wait_scatter(s):
        pltpu.make_async_copy(ybuf.at[s], y_hbm.at[pl.ds(0, C)], sem_out.at[s]).wait()

    @pl.when(c < n_used)
    def _():
        @pl.when(c == 0)
        def _():
            gather(tok_ref, 0)
            ybuf[1] = jnp.zeros(ybuf.shape[1:], ybuf.dtype)
            n_real = y_hbm.shape[0] - 2 * C
            for part in range(2):
                cp = pltpu.make_async_copy(ybuf.at[1], y_hbm.at[pl.ds(n_real + part * C, C)], sem_out.at[1])
                cp.start()
                cp.wait()

        @pl.when(c + 1 < n_used)
        def _():
            gather(tokn_ref, 1 - slot)

        new_expert = jnp.logical_or(c == 0, ce_ref[c] != ce_ref[jnp.maximum(c - 1, 0)])

        @pl.when(new_expert)
        def _():
            wgu_bf[...] = wgu_ref[...].astype(BF16)
            wdn_bf[...] = wdn_ref[...].astype(BF16)

        wait_gather(slot)
        xn = (_rms(hbuf[slot]) * g_ref[...]).astype(BF16)
        gu = _dot(xn, wgu_bf[...])
        gate_half, up_half = gu[:, :d_expert], gu[:, d_expert:]
        act = (gate_half * jax.nn.sigmoid(gate_half)) * up_half
        y = _dot(act.astype(BF16), wdn_bf[...]) * gate_ref[...]

        @pl.when(c >= 2)
        def _():
            wait_scatter(slot)

        ybuf[slot] = y
        scatter(slot)

        @pl.when(c == n_used - 1)
        def _():
            wait_scatter(slot)

            @pl.when(c >= 1)
            def _():
                wait_scatter(1 - slot)


def _moe_ffn(h, g, route, w_gate_up, w_down):
    chunk_expert, n_used, row_token, row_dest, row_gate = route
    T, D = h.shape
    n_chunks = row_token.shape[0]
    C = MOE_CHUNK
    F2 = w_gate_up.shape[2]
    d_expert = w_down.shape[1]
    n_rows = T * MOE_TOP_K + 2 * C
    chunk = lambda c, ce, nu: (c, 0, 0)
    nxt = lambda c, ce, nu: (jnp.minimum(c + 1, n_chunks - 1), 0, 0)
    smem = functools.partial(pl.BlockSpec, memory_space=pltpu.SMEM)
    y = pl.pallas_call(
        functools.partial(_moe_kernel, d_expert=d_expert),
        grid_spec=pltpu.PrefetchScalarGridSpec(
            num_scalar_prefetch=2,
            grid=(n_chunks,),
            in_specs=[
                smem((1, 1, C), chunk),
                smem((1, 1, C), nxt),
                smem((1, 1, C), chunk),
                pl.BlockSpec((None, C, 1), chunk),
                pl.BlockSpec((1, D), lambda c, ce, nu: (0, 0)),
                pl.BlockSpec((None, D, F2), lambda c, ce, nu: (ce[c], 0, 0)),
                pl.BlockSpec((None, d_expert, D), lambda c, ce, nu: (ce[c], 0, 0)),
                pl.BlockSpec(memory_space=pl.ANY),
            ],
            out_specs=pl.BlockSpec(memory_space=pl.ANY),
            scratch_shapes=[
                pltpu.VMEM((2, C, D), F32),
                pltpu.VMEM((2, C, D), F32),
                pltpu.VMEM((D, F2), BF16),
                pltpu.VMEM((d_expert, D), BF16),
                pltpu.SemaphoreType.DMA((2,)),
                pltpu.SemaphoreType.DMA((2,)),
            ],
        ),
        out_shape=jax.ShapeDtypeStruct((n_rows, D), F32),
        compiler_params=_params(("arbitrary",)),
        name="moe_expert_ffn",
    )(chunk_expert, n_used, row_token, row_token, row_dest, row_gate, g.reshape(1, D),
      w_gate_up, w_down, h)
    return y.reshape(n_rows // MOE_TOP_K, MOE_TOP_K * D)


def _combine_ple_kernel(h_ref, y_ref, p_ref, g_ref, wp_ref, wg_ref, gf_ref, o_ref, *, final):
    D = h_ref.shape[1]
    y = y_ref[...]
    h = h_ref[...] + (y[:, :D] + y[:, D:])
    xn = (_rms(h) * g_ref[...]).astype(BF16)
    gate = jax.nn.sigmoid(_dot(xn, wg_ref[...]))
    out = h + _dot(p_ref[...].astype(BF16), wp_ref[...]) * gate
    if final:
        out = _rms(out) * gf_ref[...]
    o_ref[...] = out


def _combine_ple(h, y, p, g, w_proj, w_gate, g_final, *, final, tm):
    T, D = h.shape
    Pd = p.shape[1]
    tm = min(tm, T)
    row = lambda i: (i, 0)
    fixed = lambda i: (0, 0)
    return pl.pallas_call(
        functools.partial(_combine_ple_kernel, final=final),
        grid=(T // tm,),
        in_specs=[
            pl.BlockSpec((tm, D), row),
            pl.BlockSpec((tm, MOE_TOP_K * D), row),
            pl.BlockSpec((tm, Pd), row),
            pl.BlockSpec((1, D), fixed),
            pl.BlockSpec((Pd, D), fixed),
            pl.BlockSpec((D, D), fixed),
            pl.BlockSpec((1, D), fixed),
        ],
        out_specs=pl.BlockSpec((tm, D), row),
        out_shape=jax.ShapeDtypeStruct((T, D), F32),
        compiler_params=_params(("parallel",)),
        name="moe_combine_ple",
    )(h, y, p, g.reshape(1, D), w_proj, w_gate, g_final.reshape(1, D))


def _scale_q(w, q_cols):
    scale = jnp.where(q_cols, 1.0 / math.sqrt(HEAD_DIM), 1.0).astype(w.dtype)
    return (w * scale[None, :]).astype(BF16)


def _mixer_a(h, g, w_in, w_out, *, batch, seq):
    width = A_HEADS * HEAD_DIM
    col = np.arange(w_in.shape[1])
    w = _scale_q(w_in, jnp.asarray((col // width) % 3 == 0))
    qkv = _norm_mm(h, g, w, tm=1024, tn=1024)
    outs, lses = [], []
    for grp, (window, dilation) in enumerate(A_PATTERNS):
        assert window // dilation == ATTN_BLOCK
        o, lse = _attn_a_group(qkv, batch=batch, seq=seq, group=grp, dilation=dilation, heads=A_HEADS)
        outs.append(o)
        lses.append(lse)
    return _merge_proj(outs, lses, w_out.astype(BF16), h, heads=A_HEADS, tm=512)


def _mixer_b(h, g, w_in, w_out, lq1, lk1, lq2, lk2, subln, lambda_init, *, batch, seq):
    D = h.shape[1]
    heads = D // (2 * HEAD_DIM)
    width = heads * 2 * HEAD_DIM
    w = _scale_q(w_in, jnp.asarray(np.arange(w_in.shape[1]) < width))
    qkv = _norm_mm(h, g, w, tm=1024, tn=1024).reshape(batch, seq, 3 * width)
    o = _attn_b(qkv, lq1, lk1, lq2, lk2, subln, batch=batch, seq=seq, heads=heads,
                lambda_init=lambda_init, tq=256, tk=256)
    return _mm_res(o.reshape(batch * seq, width), w_out.astype(BF16), h, tm=512)


def _mixer_c(h, g, w_in, w_out, *, batch, seq):
    D = h.shape[1]
    heads = D // HEAD_DIM
    width = heads * HEAD_DIM
    w = _scale_q(w_in, jnp.asarray(np.arange(w_in.shape[1]) < width))
    qkv = _norm_mm(h, g, w, tm=1024, tn=1024).reshape(batch, seq, 3 * width)
    o = _attn_c(qkv, batch=batch, seq=seq, heads=heads, tq=256)
    return _mm_res(o.reshape(batch * seq, width), w_out.astype(BF16), h, tm=512)


def kernel(x, p, norm_mix, norm_ffn, norm_ple, norm_final, a_w_in, a_w_out, b_w_in, b_w_out,
           b_lambda_q1, b_lambda_k1, b_lambda_q2, b_lambda_k2, b_subln, c_w_in, c_w_out,
           moe_w_coarse, moe_b_coarse, moe_w_fine, moe_b_fine, moe_w_gate_up, moe_w_down,
           ple_w_proj, ple_w_gate):
    batch, seq, D = x.shape
    depth = p.shape[0]
    T = batch * seq
    h = x.reshape(T, D)
    for i in range(depth):
        kind = i % N_MIXERS
        j = i // N_MIXERS
        if kind == 0:
            h = _mixer_a(h, norm_mix[i], a_w_in[j], a_w_out[j], batch=batch, seq=seq)
        elif kind == 1:
            lambda_init = 0.8 - 0.6 * math.exp(-0.3 * i)
            h = _mixer_b(h, norm_mix[i], b_w_in[j], b_w_out[j], b_lambda_q1[j], b_lambda_k1[j],
                         b_lambda_q2[j], b_lambda_k2[j], b_subln[j], lambda_init, batch=batch, seq=seq)
        else:
            h = _mixer_c(h, norm_mix[i], c_w_in[j], c_w_out[j], batch=batch, seq=seq)
        logits = _router(h, norm_ffn[i], moe_w_coarse[i], moe_b_coarse[i], moe_w_fine[i],
                         moe_b_fine[i], tm=1024)
        route = _route(logits, T)
        y = _moe_ffn(h, norm_ffn[i], route, moe_w_gate_up[i], moe_w_down[i])
        h = _combine_ple(h, y, p[i].reshape(T, -1), norm_ple[i], ple_w_proj[i].astype(BF16),
                         ple_w_gate[i].astype(BF16), norm_final, final=(i == depth - 1), tm=256)
    return h.reshape(batch, seq, D)
```

```python
import functools
import math

import jax
import jax.numpy as jnp
import numpy as np
from jax import lax
from jax.experimental import pallas as pl
from jax.experimental.pallas import tpu as pltpu

F32 = jnp.float32
BF16 = jnp.bfloat16
U32 = jnp.uint32
I32 = jnp.int32

HEAD_DIM = 64
N_MIXERS = 3
ATTN_BLOCK = 128
RMS_EPS = 1e-6
A_HEADS = 16
A_PATTERNS = ((128, 1), (512, 4), (2048, 16))
MOE_GROUPS = 8
MOE_EXPERTS_PER_GROUP = 8
MOE_TOP_K = 2
FFN_CHUNK = 256
LANES = 128
SUBLANES = 8
F32_EXP_ZERO = -104.0
VMEM_LIMIT = 48 * 1024 * 1024


def _params(sem, vmem=VMEM_LIMIT):
    return pltpu.CompilerParams(dimension_semantics=sem, vmem_limit_bytes=vmem)


def _rms(x):
    return x * lax.rsqrt(jnp.mean(x * x, axis=-1, keepdims=True) + RMS_EPS)


def _dot_t(a, b):
    return lax.dot_general(a, b, (((1,), (1,)), ((), ())), preferred_element_type=F32)


def _dot(a, b):
    return jnp.dot(a, b, preferred_element_type=F32)


def _split_bf16(x):
    hi = x.astype(BF16)
    lo = (x - hi.astype(F32)).astype(BF16)
    return hi, lo


def _bits(x):
    return lax.bitcast_convert_type(x, U32)


def _pack_bf16_pair(a, b):
    a = a.astype(BF16).astype(F32)
    b = b.astype(BF16).astype(F32)
    return (_bits(a) >> 16) | _bits(b)


def _unpack_bf16_pair(w):
    a = lax.bitcast_convert_type(w << 16, F32).astype(BF16)
    b = lax.bitcast_convert_type(w & jnp.uint32(0xFFFF0000), F32).astype(BF16)
    return a, b


def _norm_kernel(x_ref, g_ref, o_ref):
    o_ref[...] = (_rms(x_ref[...]) * g_ref[...]).astype(o_ref.dtype)


def _norm(x, g, *, tm):
    T, D = x.shape
    tm = min(tm, T)
    return pl.pallas_call(
        _norm_kernel,
        grid=(T // tm,),
        in_specs=[pl.BlockSpec((tm, D), lambda i: (i, 0)), pl.BlockSpec((1, D), lambda i: (0, 0))],
        out_specs=pl.BlockSpec((tm, D), lambda i: (i, 0)),
        out_shape=jax.ShapeDtypeStruct((T, D), BF16),
        compiler_params=_params(("parallel",)),
        name="rmsnorm",
    )(x, g.reshape(1, D))


def _mm_kernel(a_ref, w_ref, o_ref):
    o_ref[...] = _dot(a_ref[...], w_ref[...]).astype(o_ref.dtype)


def _mm(a, w, *, tm, tn):
    T, K = a.shape
    N = w.shape[1]
    tm = min(tm, T)
    tn = min(tn, N)
    return pl.pallas_call(
        _mm_kernel,
        grid=(T // tm, N // tn),
        in_specs=[pl.BlockSpec((tm, K), lambda i, j: (i, 0)), pl.BlockSpec((K, tn), lambda i, j: (0, j))],
        out_specs=pl.BlockSpec((tm, tn), lambda i, j: (i, j)),
        out_shape=jax.ShapeDtypeStruct((T, N), BF16),
        compiler_params=_params(("parallel", "arbitrary")),
        name="proj",
    )(a, w)


def _norm_mm_kernel(x_ref, g_ref, w_ref, o_ref, xn_ref):
    @pl.when(pl.program_id(1) == 0)
    def _():
        xn_ref[...] = (_rms(x_ref[...]) * g_ref[...]).astype(BF16)

    o_ref[...] = _dot(xn_ref[...], w_ref[...]).astype(o_ref.dtype)


def _norm_mm(x, g, w, *, tm, tn, out_dtype=BF16):
    T, D = x.shape
    N = w.shape[1]
    tm = min(tm, T)
    tn = min(tn, N)
    return pl.pallas_call(
        _norm_mm_kernel,
        grid=(T // tm, N // tn),
        in_specs=[
            pl.BlockSpec((tm, D), lambda i, j: (i, 0)),
            pl.BlockSpec((1, D), lambda i, j: (0, 0)),
            pl.BlockSpec((D, tn), lambda i, j: (0, j)),
        ],
        out_specs=pl.BlockSpec((tm, tn), lambda i, j: (i, j)),
        out_shape=jax.ShapeDtypeStruct((T, N), out_dtype),
        scratch_shapes=[pltpu.VMEM((tm, D), BF16)],
        compiler_params=_params(("parallel", "arbitrary")),
        name="norm_proj",
    )(x, g.reshape(1, D), w)


def _mm_res_kernel(a_ref, w_ref, r_ref, o_ref):
    o_ref[...] = r_ref[...] + _dot(a_ref[...], w_ref[...])


def _mm_res(a, w, res, *, tm):
    T, K = a.shape
    N = w.shape[1]
    tm = min(tm, T)
    return pl.pallas_call(
        _mm_res_kernel,
        grid=(T // tm,),
        in_specs=[
            pl.BlockSpec((tm, K), lambda i: (i, 0)),
            pl.BlockSpec((K, N), lambda i: (0, 0)),
            pl.BlockSpec((tm, N), lambda i: (i, 0)),
        ],
        out_specs=pl.BlockSpec((tm, N), lambda i: (i, 0)),
        out_shape=jax.ShapeDtypeStruct((T, N), F32),
        compiler_params=_params(("parallel",)),
        name="out_proj",
    )(a, w, res)


def _attn_a_kernel(bias_ref, q_ref, kp_ref, kc_ref, vp_ref, vc_ref, o_ref, lse_ref, *, heads):
    n = pl.program_id(2)
    variant = jnp.minimum(n, 1)
    blk = ATTN_BLOCK
    lane = lax.broadcasted_iota(I32, (1, LANES), 1)
    first_head = lane < HEAD_DIM
    lse_lane = lax.broadcasted_iota(I32, (blk, LANES), 1)
    lse_all = jnp.zeros((blk, LANES), F32)
    for pair in range(heads // 2):
        sl = slice(LANES * pair, LANES * (pair + 1))
        q = q_ref[:, sl]
        keys = jnp.concatenate([kp_ref[:, sl], kc_ref[:, sl]], axis=0)
        vals = jnp.concatenate([vp_ref[:, sl], vc_ref[:, sl]], axis=0)
        outs = []
        for sub in range(2):
            h = 2 * pair + sub
            mine = first_head if sub == 0 else jnp.logical_not(first_head)
            qm = jnp.where(mine, q, jnp.zeros_like(q))
            s = _dot_t(qm, keys) + bias_ref[variant, h]
            m = jnp.max(s, axis=-1, keepdims=True)
            e = jnp.exp(s - m)
            den = jnp.sum(e, axis=-1, keepdims=True)
            outs.append(_dot(e.astype(BF16), vals) * (1.0 / den))
            lse_all = jnp.where(lse_lane == h, m + jnp.log(den), lse_all)
        o_ref[:, sl] = jnp.where(first_head, outs[0], outs[1]).astype(o_ref.dtype)
    lse_ref[...] = lse_all


def _alibi_window_bias(dilation, heads):
    blk = ATTN_BLOCK
    a = np.arange(blk)[:, None]
    b = np.arange(2 * blk)[None, :]
    delta = blk + a - b
    valid = (delta >= 0) & (delta <= blk)
    slopes = 2.0 ** (-8.0 * np.arange(1, heads + 1) / heads)
    bias = -slopes[:, None, None] * (delta * dilation).astype(np.float64)[None]
    bias = np.where(valid[None], bias, -np.inf)
    first = bias.copy()
    first[:, :, :blk] = -np.inf
    return jnp.asarray(np.stack([first, bias]), F32)


def _attn_a_group(qkv, *, batch, streams, length, dilation, heads):
    width = heads * HEAD_DIM
    nb = length // ATTN_BLOCK
    x = qkv.reshape(batch, streams, length, 3 * width)
    blk = (None, None, ATTN_BLOCK, width)
    cur = lambda which: (lambda b, c, n: (b, c, n, which))
    prev = lambda which: (lambda b, c, n: (b, c, jnp.maximum(n - 1, 0), which))
    bias = _alibi_window_bias(dilation, heads)
    o, lse = pl.pallas_call(
        functools.partial(_attn_a_kernel, heads=heads),
        grid=(batch, streams, nb),
        in_specs=[
            pl.BlockSpec(bias.shape, lambda b, c, n: (0, 0, 0, 0)),
            pl.BlockSpec(blk, cur(0)),
            pl.BlockSpec(blk, prev(1)),
            pl.BlockSpec(blk, cur(1)),
            pl.BlockSpec(blk, prev(2)),
            pl.BlockSpec(blk, cur(2)),
        ],
        out_specs=[
            pl.BlockSpec(blk, lambda b, c, n: (b, c, n, 0)),
            pl.BlockSpec((None, None, ATTN_BLOCK, LANES), lambda b, c, n: (b, c, n, 0)),
        ],
        out_shape=[
            jax.ShapeDtypeStruct((batch, streams, length, width), BF16),
            jax.ShapeDtypeStruct((batch, streams, length, LANES), F32),
        ],
        compiler_params=_params(("parallel", "parallel", "arbitrary")),
        name=f"dilated_attn_r{dilation}",
    )(bias, x, x, x, x, x)
    return o, lse


def _merge_proj_kernel(o0_ref, o1_ref, o2_ref, l0_ref, l1_ref, l2_ref, e_ref, w_ref, r_ref, out_ref):
    l0, l1, l2 = l0_ref[...], l1_ref[...], l2_ref[...]
    mx = jnp.maximum(jnp.maximum(l0, l1), l2)
    e0, e1, e2 = jnp.exp(l0 - mx), jnp.exp(l1 - mx), jnp.exp(l2 - mx)
    inv = 1.0 / (e0 + e1 + e2)
    expand = e_ref[...]
    acc = None
    for e, o_ref in ((e0, o0_ref), (e1, o1_ref), (e2, o2_ref)):
        hi, lo = _split_bf16(e * inv)
        w_full = _dot(hi, expand) + _dot(lo, expand)
        term = w_full * o_ref[...].astype(F32)
        acc = term if acc is None else acc + term
    out_ref[...] = r_ref[...] + _dot(acc.astype(BF16), w_ref[...])


def _merge_proj(outs, lses, w_out, res, *, heads, tm):
    T, width = outs[0].shape
    N = w_out.shape[1]
    tm = min(tm, T)
    expand = np.zeros((LANES, width), np.float32)
    for h in range(heads):
        expand[h, h * HEAD_DIM:(h + 1) * HEAD_DIM] = 1.0
    expand = jnp.asarray(expand, BF16)
    row = lambda i: (i, 0)
    fixed = lambda i: (0, 0)
    return pl.pallas_call(
        _merge_proj_kernel,
        grid=(T // tm,),
        in_specs=[pl.BlockSpec((tm, width), row)] * 3 + [pl.BlockSpec((tm, LANES), row)] * 3 + [
            pl.BlockSpec((LANES, width), fixed),
            pl.BlockSpec((width, N), fixed),
            pl.BlockSpec((tm, N), row),
        ],
        out_specs=pl.BlockSpec((tm, N), row),
        out_shape=jax.ShapeDtypeStruct((T, N), F32),
        compiler_params=_params(("parallel",)),
        name="merge_out_proj",
    )(*outs, *lses, expand, w_out, res)


def _attn_b_kernel(slope_ref, q_ref, k_ref, v_ref, lq1_ref, lk1_ref, lq2_ref, lk2_ref, sub_ref,
                   o_ref, *, t, lambda_init):
    h = pl.program_id(1)
    i = pl.program_id(2)
    slope = slope_ref[h]
    q = q_ref[...]
    lane = lax.broadcasted_iota(I32, (1, LANES), 1)
    first = lane < HEAD_DIM
    zero = jnp.zeros_like(q)
    qs = (jnp.where(first, q, zero), jnp.where(first, zero, q))
    rel = lax.broadcasted_iota(I32, (t, t), 0) - lax.broadcasted_iota(I32, (t, t), 1)
    rel_bias = slope * rel.astype(F32)

    def step(j, carry, diagonal):
        start = pl.multiple_of(j * t, t)
        k = k_ref[pl.ds(start, t), :]
        v = v_ref[pl.ds(start, t), :]
        shift = slope * ((i - j) * t).astype(F32)
        new = []
        for c in range(2):
            m, l, acc = carry[3 * c:3 * c + 3]
            s = _dot_t(qs[c], k) - rel_bias
            if diagonal:
                s = jnp.where(rel >= 0, s, -jnp.inf)
            m_new = jnp.maximum(m, jnp.max(s, axis=-1, keepdims=True) - shift)
            alpha = jnp.exp(m - m_new)
            p = jnp.exp(s - (m_new + shift))
            l = alpha * l + jnp.sum(p, axis=-1, keepdims=True)
            acc = alpha * acc + _dot(p.astype(BF16), v)
            new += [m_new, l, acc]
        return tuple(new)

    init = (jnp.full((t, 1), -jnp.inf, F32), jnp.zeros((t, 1), F32), jnp.zeros((t, LANES), F32)) * 2
    carry = lax.fori_loop(0, i, lambda j, c: step(j, c, False), init)
    m1, l1, a1, m2, l2, a2 = step(i, carry, True)
    lam = (jnp.exp(jnp.sum(lq1_ref[...] * lk1_ref[...], axis=-1, keepdims=True))
           - jnp.exp(jnp.sum(lq2_ref[...] * lk2_ref[...], axis=-1, keepdims=True)) + lambda_init)
    o = a1 * (1.0 / l1) - lam * (a2 * (1.0 / l2))
    o = _rms(o) * sub_ref[...] * (1.0 - lambda_init)
    o_ref[...] = o.astype(o_ref.dtype)


def _attn_b(qkv, lq1, lk1, lq2, lk2, subln, *, batch, seq, heads, lambda_init, t):
    t = min(t, seq)
    width = 2 * HEAD_DIM
    slopes = jnp.asarray(2.0 ** (-8.0 * np.arange(1, heads + 1) / heads), F32)
    vec = lambda a: a.reshape(1, -1).astype(F32)
    small = lambda n: pl.BlockSpec((1, n), lambda b, h, i, s: (0, 0))
    return pl.pallas_call(
        functools.partial(_attn_b_kernel, t=t, lambda_init=lambda_init),
        grid_spec=pltpu.PrefetchScalarGridSpec(
            num_scalar_prefetch=1,
            grid=(batch, heads, seq // t),
            in_specs=[
                pl.BlockSpec((None, t, width), lambda b, h, i, s: (b, i, h)),
                pl.BlockSpec((None, seq, width), lambda b, h, i, s: (b, 0, heads + h)),
                pl.BlockSpec((None, seq, width), lambda b, h, i, s: (b, 0, 2 * heads + h)),
                small(HEAD_DIM), small(HEAD_DIM), small(HEAD_DIM), small(HEAD_DIM), small(width),
            ],
            out_specs=pl.BlockSpec((None, t, width), lambda b, h, i, s: (b, i, h)),
        ),
        out_shape=jax.ShapeDtypeStruct((batch, seq, heads * width), BF16),
        compiler_params=_params(("parallel", "parallel", "arbitrary")),
        name="diff_attn",
    )(slopes, qkv, qkv, qkv, vec(lq1), vec(lk1), vec(lq2), vec(lk2), vec(subln))


def _attn_c_kernel(q_ref, k_ref, v_ref, u_ref, o_ref, *, tq):
    i = pl.program_id(2)
    kb_w = LANES
    q = q_ref[...]
    lane = lax.broadcasted_iota(I32, (1, LANES), 1)
    first = lane < HEAD_DIM
    zero = jnp.zeros_like(q)
    qs = (jnp.where(first, q, zero), jnp.where(first, zero, q))
    upper = u_ref[...]
    rel = lax.broadcasted_iota(I32, (tq, kb_w), 0) - lax.broadcasted_iota(I32, (tq, kb_w), 1)

    def cond(carry):
        return jnp.logical_and(carry[0] >= 0, carry[1] > 0)

    def body(carry):
        kb, _, c0, c1, acc0, acc1 = carry
        start = pl.multiple_of(kb * kb_w, kb_w)
        k = k_ref[pl.ds(start, kb_w), :]
        v = v_ref[pl.ds(start, kb_w), :]
        causal = (rel + (i * tq - kb * kb_w)) > 0
        new_c, new_acc = [], []
        for qh, c, acc in ((qs[0], c0, acc0), (qs[1], c1, acc1)):
            z = _dot_t(qh, k)
            softplus = jnp.maximum(z, 0.0) + jnp.log(1.0 + jnp.exp(-jnp.abs(z)))
            log_rest = jnp.where(causal, -softplus, 0.0)
            hi, lo = _split_bf16(log_rest)
            tail = c + _dot(hi, upper) + _dot(lo, upper)
            a = jnp.where(causal, jnp.exp(z - softplus + tail), 0.0)
            new_acc.append(acc + _dot(a.astype(BF16), v))
            new_c.append(c + jnp.sum(log_rest, axis=-1, keepdims=True))
        alive = jnp.max(jnp.maximum(new_c[0], new_c[1])) > F32_EXP_ZERO
        return kb - 1, alive.astype(I32), new_c[0], new_c[1], new_acc[0], new_acc[1]

    last_kb = (i + 1) * (tq // kb_w) - 1
    zc = jnp.zeros((tq, 1), F32)
    za = jnp.zeros((tq, LANES), F32)
    out = lax.while_loop(cond, body, (last_kb, jnp.int32(1), zc, zc, za, za))
    o_ref[...] = jnp.where(first, out[4], out[5]).astype(o_ref.dtype)


def _attn_c(qkv, *, batch, seq, heads, tq):
    tq = min(tq, seq)
    pairs = heads // 2
    upper = jnp.asarray(np.tril(np.ones((LANES, LANES), np.float32), -1), BF16)
    return pl.pallas_call(
        functools.partial(_attn_c_kernel, tq=tq),
        grid=(batch, pairs, seq // tq),
        in_specs=[
            pl.BlockSpec((None, tq, LANES), lambda b, h, i: (b, i, h)),
            pl.BlockSpec((None, seq, LANES), lambda b, h, i: (b, 0, pairs + h)),
            pl.BlockSpec((None, seq, LANES), lambda b, h, i: (b, 0, 2 * pairs + h)),
            pl.BlockSpec((LANES, LANES), lambda b, h, i: (0, 0)),
        ],
        out_specs=pl.BlockSpec((None, tq, LANES), lambda b, h, i: (b, i, h)),
        out_shape=jax.ShapeDtypeStruct((batch, seq, heads * HEAD_DIM), BF16),
        compiler_params=_params(("parallel", "parallel", "arbitrary")),
        name="stick_breaking_attn",
    )(qkv, qkv, qkv, upper)


def _router_kernel(x_ref, g_ref, whi_ref, wlo_ref, b_ref, u_ref, info_ref, cnt_ref, run_ref,
                   *, groups, per_group):
    G, E = groups, per_group
    tm = x_ref.shape[0]
    ne = G * E

    @pl.when(pl.program_id(0) == 0)
    def _():
        run_ref[...] = jnp.zeros_like(run_ref)

    xn = _rms(x_ref[...]) * g_ref[...]
    hi, lo = _split_bf16(xn)
    whi = whi_ref[...]
    lt = _dot_t(whi, hi) + _dot_t(whi, lo) + _dot_t(wlo_ref[...], hi) + b_ref[...]
    sub = lax.broadcasted_iota(I32, (E, tm), 0)
    coarse = lt[0:G]
    cmax = jnp.max(coarse, axis=0, keepdims=True)
    g_prob = 1.0 / jnp.sum(jnp.exp(coarse - cmax), axis=0, keepdims=True)
    g_idx = jnp.min(jnp.where(coarse == cmax, sub, G), axis=0, keepdims=True)
    fine = jnp.zeros((E, tm), F32)
    for grp in range(G):
        fine = jnp.where(g_idx == grp, lt[G + grp * E:G + (grp + 1) * E], fine)
    fmax = jnp.max(fine, axis=0, keepdims=True)
    fsum = jnp.sum(jnp.exp(fine - fmax), axis=0, keepdims=True)
    i1 = jnp.min(jnp.where(fine == fmax, sub, E), axis=0, keepdims=True)
    rest = jnp.where(sub == i1, -jnp.inf, fine)
    m2 = jnp.max(rest, axis=0, keepdims=True)
    i2 = jnp.min(jnp.where(rest == m2, sub, E), axis=0, keepdims=True)
    p1 = 1.0 / fsum
    p2 = jnp.exp(m2 - fmax) / fsum
    norm = p1 + p2
    gate1 = g_prob * (p1 / norm)
    gate2 = g_prob * (p2 / norm)
    e1 = g_idx * E + i1
    e2 = g_idx * E + i2
    ex = lax.broadcasted_iota(I32, (ne, tm), 0)
    oh1 = (ex == e1).astype(F32)
    oh2 = (ex == e2).astype(F32)
    both = oh1 + oh2
    earlier = _dot(both.astype(BF16), u_ref[...]) + run_ref[:, 0:1]
    r1 = jnp.sum(oh1 * earlier, axis=0, keepdims=True)
    r2 = jnp.sum(oh2 * earlier, axis=0, keepdims=True)
    total = run_ref[...] + jnp.sum(both, axis=1, keepdims=True)
    run_ref[...] = total
    cnt_ref[...] = total
    row = lax.broadcasted_iota(I32, (SUBLANES, tm), 0)
    fields = (e1.astype(F32), e2.astype(F32), r1, r2, gate1, gate2)
    info = jnp.zeros((SUBLANES, tm), F32)
    for idx, f in enumerate(fields):
        info = jnp.where(row == idx, f, info)
    info_ref[...] = info


def _router(h, g, w_coarse, b_coarse, w_fine, b_fine, *, tm):
    T, D = h.shape
    tm = min(tm, T)
    G, ne = w_coarse.shape[1], w_fine.shape[1]
    pad = LANES - G - ne
    wt = jnp.pad(jnp.concatenate([w_coarse, w_fine], axis=1), ((0, 0), (0, pad))).T
    b = jnp.pad(jnp.concatenate([b_coarse, b_fine]), (0, pad)).reshape(LANES, 1)
    whi, wlo = _split_bf16(wt)
    before = jnp.asarray(np.triu(np.ones((tm, tm), np.float32), 1), BF16)
    fixed = lambda i: (0, 0)
    return pl.pallas_call(
        functools.partial(_router_kernel, groups=G, per_group=ne // G),
        grid=(T // tm,),
        in_specs=[
            pl.BlockSpec((tm, D), lambda i: (i, 0)),
            pl.BlockSpec((1, D), fixed),
            pl.BlockSpec((LANES, D), fixed),
            pl.BlockSpec((LANES, D), fixed),
            pl.BlockSpec((LANES, 1), fixed),
            pl.BlockSpec((tm, tm), fixed),
        ],
        out_specs=[pl.BlockSpec((SUBLANES, tm), lambda i: (0, i)), pl.BlockSpec((ne, LANES), fixed)],
        out_shape=[jax.ShapeDtypeStruct((SUBLANES, T), F32), jax.ShapeDtypeStruct((ne, LANES), F32)],
        scratch_shapes=[pltpu.VMEM((ne, LANES), F32)],
        compiler_params=_params(("arbitrary",)),
        name="moe_router",
    )(h, g.reshape(1, D), whi, wlo, b, before)


def _plan(info, counts, *, tile):
    T = info.shape[1]
    ne = counts.shape[0]
    C = FFN_CHUNK
    experts = info[0:2].astype(I32)
    ranks = info[2:4].astype(I32)
    gates = info[4:6]
    cnt = counts[:, 0].astype(I32)
    padded = (cnt + C - 1) // C * C
    pad_end = jnp.cumsum(padded)
    pad_start = pad_end - padded
    ids = jnp.arange(ne, dtype=I32)
    start_of = jnp.sum(jnp.where(experts[:, :, None] == ids, pad_start, 0), axis=-1)
    dest = ranks + start_of
    n_chunks = (MOE_TOP_K * T + ne * (C - 1) + C - 1) // C
    n_used = (pad_end[-1] // C).astype(I32)
    first_row = jnp.minimum(jnp.arange(n_chunks, dtype=I32), n_used - 1) * C
    chunk_expert = jnp.minimum(jnp.sum(pad_end[None, :] <= first_row[:, None], axis=1), ne - 1).astype(I32)
    dest_tiles = dest.reshape(MOE_TOP_K, T // tile, tile).transpose(1, 0, 2).reshape(T // tile, 1, MOE_TOP_K * tile)
    return dest_tiles, gates.T, chunk_expert, n_used.reshape(1), n_chunks


def _dispatch_kernel(dst_ref, h_ref, g_ref, xs_in, xs_out, xbuf, sem):
    del xs_in
    i = pl.program_id(0)
    n = pl.num_programs(0)
    tm, D = h_ref.shape
    slot = i % 2

    def wait(s):
        for _ in range(MOE_TOP_K):
            pltpu.make_async_copy(xbuf.at[s], xs_out.at[pl.ds(0, tm)], sem.at[s]).wait()

    @pl.when(i >= 2)
    def _():
        wait(slot)

    xn = _rms(h_ref[...]) * g_ref[...]
    xbuf[slot] = _pack_bf16_pair(xn[:, :D // 2], xn[:, D // 2:])

    def issue(r, _):
        for k in range(MOE_TOP_K):
            pltpu.make_async_copy(xbuf.at[slot, pl.ds(r, 1)], xs_out.at[pl.ds(dst_ref[0, 0, k * tm + r], 1)],
                                  sem.at[slot]).start()
        return 0

    lax.fori_loop(0, tm, issue, 0, unroll=8)

    @pl.when(i == n - 1)
    def _():
        wait(slot)

        @pl.when(i >= 1)
        def _():
            wait(1 - slot)


def _dispatch(h, g, dest_tiles, n_rows, *, tm):
    T, D = h.shape
    xs0 = jnp.zeros((n_rows, D // 2), U32)
    return pl.pallas_call(
        _dispatch_kernel,
        grid=(T // tm,),
        in_specs=[
            pl.BlockSpec((1, 1, MOE_TOP_K * tm), lambda i: (i, 0, 0), memory_space=pltpu.SMEM),
            pl.BlockSpec((tm, D), lambda i: (i, 0)),
            pl.BlockSpec((1, D), lambda i: (0, 0)),
            pl.BlockSpec(memory_space=pl.ANY),
        ],
        out_specs=pl.BlockSpec(memory_space=pl.ANY),
        out_shape=jax.ShapeDtypeStruct((n_rows, D // 2), U32),
        scratch_shapes=[pltpu.VMEM((2, tm, D // 2), U32), pltpu.SemaphoreType.DMA((2,))],
        input_output_aliases={3: 0},
        compiler_params=_params(("arbitrary",)),
        name="moe_dispatch",
    )(dest_tiles, h, g.reshape(1, D), xs0)


def _ffn_kernel(ce_ref, nu_ref, x_ref, wgu_ref, wdn_ref, y_ref, wgu_bf, wdn_bf, *, d_expert):
    c = pl.program_id(0)
    n_used = nu_ref[0]

    @pl.when(c < n_used)
    def _():
        @pl.when(jnp.logical_or(c == 0, ce_ref[c] != ce_ref[jnp.maximum(c - 1, 0)]))
        def _():
            wgu_bf[...] = wgu_ref[...].astype(BF16)
            wdn_bf[...] = wdn_ref[...].astype(BF16)

        half = x_ref.shape[1]
        xa, xb = _unpack_bf16_pair(x_ref[...])
        gu = _dot(xa, wgu_bf[:half]) + _dot(xb, wgu_bf[half:])
        gate, up = gu[:, :d_expert], gu[:, d_expert:]
        act = (gate * jax.nn.sigmoid(gate)) * up
        y_ref[...] = _dot(act.astype(BF16), wdn_bf[...])

    @pl.when(c >= n_used)
    def _():
        y_ref[...] = jnp.zeros_like(y_ref)


def _ffn(xs, chunk_expert, n_used, w_gate_up, w_down):
    n_rows, half = xs.shape
    C = FFN_CHUNK
    n_chunks = n_rows // C
    D, F2 = w_gate_up.shape[1:]
    d_expert = w_down.shape[1]
    return pl.pallas_call(
        functools.partial(_ffn_kernel, d_expert=d_expert),
        grid_spec=pltpu.PrefetchScalarGridSpec(
            num_scalar_prefetch=2,
            grid=(n_chunks,),
            in_specs=[
                pl.BlockSpec((C, half), lambda c, ce, nu: (jnp.minimum(c, nu[0] - 1), 0)),
                pl.BlockSpec((None, D, F2), lambda c, ce, nu: (ce[c], 0, 0)),
                pl.BlockSpec((None, d_expert, D), lambda c, ce, nu: (ce[c], 0, 0)),
            ],
            out_specs=pl.BlockSpec((C, D), lambda c, ce, nu: (c, 0)),
            scratch_shapes=[pltpu.VMEM((D, F2), BF16), pltpu.VMEM((d_expert, D), BF16)],
        ),
        out_shape=jax.ShapeDtypeStruct((n_rows, D), F32),
        compiler_params=_params(("arbitrary",)),
        name="moe_expert_ffn",
    )(chunk_expert, n_used, xs, w_gate_up, w_down)


def _combine_ple_kernel(dst_ref, dstn_ref, h_ref, gt_ref, p_ref, g_ref, wp_ref, wg_ref, gf_ref, y_hbm,
                        o_ref, ybuf, sem, *, final):
    i = pl.program_id(0)
    n = pl.num_programs(0)
    tm, D = h_ref.shape
    rows = MOE_TOP_K * tm
    slot = i % 2

    def gather(dref, s):
        def issue(r, _):
            for k in range(MOE_TOP_K):
                pltpu.make_async_copy(y_hbm.at[pl.ds(dref[0, 0, k * tm + r], 1)],
                                      ybuf.at[s, pl.ds(k * tm + r, 1)], sem.at[s]).start()
            return 0
        lax.fori_loop(0, tm, issue, 0, unroll=8)

    @pl.when(i == 0)
    def _():
        gather(dst_ref, 0)

    @pl.when(i + 1 < n)
    def _():
        gather(dstn_ref, 1 - slot)

    pltpu.make_async_copy(y_hbm.at[pl.ds(0, rows)], ybuf.at[slot], sem.at[slot]).wait()
    gt = gt_ref[...]
    h = h_ref[...] + (gt[:, 0:1] * ybuf[slot, :tm] + gt[:, 1:2] * ybuf[slot, tm:])
    xn = (_rms(h) * g_ref[...]).astype(BF16)
    gate = jax.nn.sigmoid(_dot(xn, wg_ref[...]))
    out = h + _dot(p_ref[...].astype(BF16), wp_ref[...]) * gate
    if final:
        out = _rms(out) * gf_ref[...]
    o_ref[...] = out


def _combine_ple(h, y, dest_tiles, gates, p, g, w_proj, w_gate, g_final, *, final, tm):
    T, D = h.shape
    Pd = p.shape[1]
    n_tiles = T // tm
    row = lambda i: (i, 0)
    fixed = lambda i: (0, 0)
    smem = functools.partial(pl.BlockSpec, (1, 1, MOE_TOP_K * tm), memory_space=pltpu.SMEM)
    return pl.pallas_call(
        functools.partial(_combine_ple_kernel, final=final),
        grid=(n_tiles,),
        in_specs=[
            smem(lambda i: (i, 0, 0)),
            smem(lambda i: (jnp.minimum(i + 1, n_tiles - 1), 0, 0)),
            pl.BlockSpec((tm, D), row),
            pl.BlockSpec((tm, MOE_TOP_K), row),
            pl.BlockSpec((tm, Pd), row),
            pl.BlockSpec((1, D), fixed),
            pl.BlockSpec((Pd, D), fixed),
            pl.BlockSpec((D, D), fixed),
            pl.BlockSpec((1, D), fixed),
            pl.BlockSpec(memory_space=pl.ANY),
        ],
        out_specs=pl.BlockSpec((tm, D), row),
        out_shape=jax.ShapeDtypeStruct((T, D), F32),
        scratch_shapes=[pltpu.VMEM((2, MOE_TOP_K * tm, D), F32), pltpu.SemaphoreType.DMA((2,))],
        compiler_params=_params(("arbitrary",)),
        name="moe_combine_ple",
    )(dest_tiles, dest_tiles, h, gates, p, g.reshape(1, D), w_proj, w_gate, g_final.reshape(1, D), y)


def _moe_ple(h, p, g_ffn, w_coarse, b_coarse, w_fine, b_fine, w_gate_up, w_down, g_ple, w_proj, w_gate,
             g_final, *, final):
    tile = min(256, h.shape[0])
    info, counts = _router(h, g_ffn, w_coarse, b_coarse, w_fine, b_fine, tm=min(512, h.shape[0]))
    dest_tiles, gates, chunk_expert, n_used, n_chunks = _plan(info, counts, tile=tile)
    xs = _dispatch(h, g_ffn, dest_tiles, n_chunks * FFN_CHUNK, tm=tile)
    y = _ffn(xs, chunk_expert, n_used, w_gate_up, w_down)
    return _combine_ple(h, y, dest_tiles, gates, p, g_ple, w_proj.astype(BF16), w_gate.astype(BF16),
                        g_final, final=final, tm=tile)


def _scale_q(w, q_cols):
    scale = jnp.where(q_cols, 1.0 / math.sqrt(HEAD_DIM), 1.0).astype(w.dtype)
    return (w * scale[None, :]).astype(BF16)


def _mixer_a(h, g, w_in, w_out, *, batch, seq):
    T, D = h.shape
    width = A_HEADS * HEAD_DIM
    col = np.arange(w_in.shape[1])
    w = _scale_q(w_in, jnp.asarray((col // width) % 3 == 0))
    xn = _norm(h, g, tm=1024)
    outs, lses = [], []
    for grp, (window, r) in enumerate(A_PATTERNS):
        assert window // r == ATTN_BLOCK and seq % (r * ATTN_BLOCK) == 0
        L = seq // r
        to_streams = lambda a: a.reshape(batch, L, r, -1).transpose(0, 2, 1, 3).reshape(T, -1)
        from_streams = lambda a: a.transpose(0, 2, 1, 3).reshape(T, -1)
        xs = xn if r == 1 else to_streams(xn)
        qkv = _mm(xs, w[:, grp * 3 * width:(grp + 1) * 3 * width], tm=1024, tn=1024)
        o, lse = _attn_a_group(qkv, batch=batch, streams=r, length=L, dilation=r, heads=A_HEADS)
        outs.append(from_streams(o))
        lses.append(from_streams(lse))
    return _merge_proj(outs, lses, w_out.astype(BF16), h, heads=A_HEADS, tm=512)


def _mixer_b(h, g, w_in, w_out, lq1, lk1, lq2, lk2, subln, lambda_init, *, batch, seq):
    D = h.shape[1]
    heads = D // (2 * HEAD_DIM)
    width = heads * 2 * HEAD_DIM
    w = _scale_q(w_in, jnp.asarray(np.arange(w_in.shape[1]) < width))
    qkv = _norm_mm(h, g, w, tm=1024, tn=1024).reshape(batch, seq, 3 * width)
    o = _attn_b(qkv, lq1, lk1, lq2, lk2, subln, batch=batch, seq=seq, heads=heads,
                lambda_init=lambda_init, t=256)
    return _mm_res(o.reshape(batch * seq, width), w_out.astype(BF16), h, tm=512)


def _mixer_c(h, g, w_in, w_out, *, batch, seq):
    D = h.shape[1]
    heads = D // HEAD_DIM
    width = heads * HEAD_DIM
    w = _scale_q(w_in, jnp.asarray(np.arange(w_in.shape[1]) < width))
    qkv = _norm_mm(h, g, w, tm=1024, tn=1024).reshape(batch, seq, 3 * width)
    o = _attn_c(qkv, batch=batch, seq=seq, heads=heads, tq=256)
    return _mm_res(o.reshape(batch * seq, width), w_out.astype(BF16), h, tm=512)


def kernel(x, p, norm_mix, norm_ffn, norm_ple, norm_final, a_w_in, a_w_out, b_w_in, b_w_out,
           b_lambda_q1, b_lambda_k1, b_lambda_q2, b_lambda_k2, b_subln, c_w_in, c_w_out,
           moe_w_coarse, moe_b_coarse, moe_w_fine, moe_b_fine, moe_w_gate_up, moe_w_down,
           ple_w_proj, ple_w_gate):
    batch, seq, D = x.shape
    depth = p.shape[0]
    T = batch * seq
    h = x.reshape(T, D)
    for i in range(depth):
        kind = i % N_MIXERS
        j = i // N_MIXERS
        if kind == 0:
            h = _mixer_a(h, norm_mix[i], a_w_in[j], a_w_out[j], batch=batch, seq=seq)
        elif kind == 1:
            lambda_init = 0.8 - 0.6 * math.exp(-0.3 * i)
            h = _mixer_b(h, norm_mix[i], b_w_in[j], b_w_out[j], b_lambda_q1[j], b_lambda_k1[j],
                         b_lambda_q2[j], b_lambda_k2[j], b_subln[j], lambda_init, batch=batch, seq=seq)
        else:
            h = _mixer_c(h, norm_mix[i], c_w_in[j], c_w_out[j], batch=batch, seq=seq)
        h = _moe_ple(h, p[i].reshape(T, -1), norm_ffn[i], moe_w_coarse[i], moe_b_coarse[i], moe_w_fine[i],
                     moe_b_fine[i], moe_w_gate_up[i], moe_w_down[i], norm_ple[i], ple_w_proj[i],
                     ple_w_gate[i], norm_final, final=(i == depth - 1))
    return h.reshape(batch, seq, D)
```

```python
import functools
import math

import jax
import jax.numpy as jnp
import numpy as np
from jax import lax
from jax.experimental import pallas as pl
from jax.experimental.pallas import tpu as pltpu

F32 = jnp.float32
BF16 = jnp.bfloat16
U32 = jnp.uint32
I32 = jnp.int32

HEAD_DIM = 64
N_MIXERS = 3
ATTN_BLOCK = 128
RMS_EPS = 1e-6
A_HEADS = 16
A_PATTERNS = ((128, 1), (512, 4), (2048, 16))
MOE_GROUPS = 8
MOE_EXPERTS_PER_GROUP = 8
MOE_TOP_K = 2
FFN_CHUNK = 256
LANES = 128
SUBLANES = 8
F32_EXP_ZERO = -104.0
VMEM_LIMIT = 48 * 1024 * 1024


def _params(sem, vmem=VMEM_LIMIT):
    return pltpu.CompilerParams(dimension_semantics=sem, vmem_limit_bytes=vmem)


def _rms(x):
    return x * lax.rsqrt(jnp.mean(x * x, axis=-1, keepdims=True) + RMS_EPS)


def _dot_t(a, b):
    return lax.dot_general(a, b, (((1,), (1,)), ((), ())), preferred_element_type=F32)


def _dot(a, b):
    return jnp.dot(a, b, preferred_element_type=F32)


def _split_bf16(x):
    hi = x.astype(BF16)
    lo = (x - hi.astype(F32)).astype(BF16)
    return hi, lo


def _bits(x):
    return lax.bitcast_convert_type(x, U32)


def _pack_bf16_pair(a, b):
    a = a.astype(BF16).astype(F32)
    b = b.astype(BF16).astype(F32)
    return (_bits(a) >> 16) | _bits(b)


def _unpack_bf16_pair(w):
    a = lax.bitcast_convert_type(w << 16, F32).astype(BF16)
    b = lax.bitcast_convert_type(w & jnp.uint32(0xFFFF0000), F32).astype(BF16)
    return a, b


def _norm_kernel(x_ref, g_ref, o_ref):
    o_ref[...] = (_rms(x_ref[...]) * g_ref[...]).astype(o_ref.dtype)


def _norm(x, g, *, tm):
    T, D = x.shape
    tm = min(tm, T)
    return pl.pallas_call(
        _norm_kernel,
        grid=(T // tm,),
        in_specs=[pl.BlockSpec((tm, D), lambda i: (i, 0)), pl.BlockSpec((1, D), lambda i: (0, 0))],
        out_specs=pl.BlockSpec((tm, D), lambda i: (i, 0)),
        out_shape=jax.ShapeDtypeStruct((T, D), BF16),
        compiler_params=_params(("parallel",)),
        name="rmsnorm",
    )(x, g.reshape(1, D))


def _mm_kernel(a_ref, w_ref, o_ref):
    o_ref[...] = _dot(a_ref[...], w_ref[...]).astype(o_ref.dtype)


def _mm(a, w, *, col0, n_out, tm, tn):
    T, K = a.shape
    N = n_out
    tm = min(tm, T)
    tn = min(tn, N)
    first = col0 // tn
    return pl.pallas_call(
        _mm_kernel,
        grid=(T // tm, N // tn),
        in_specs=[pl.BlockSpec((tm, K), lambda i, j: (i, 0)), pl.BlockSpec((K, tn), lambda i, j: (0, first + j))],
        out_specs=pl.BlockSpec((tm, tn), lambda i, j: (i, j)),
        out_shape=jax.ShapeDtypeStruct((T, N), BF16),
        compiler_params=_params(("parallel", "arbitrary")),
        name="proj",
    )(a, w)


def _norm_mm_kernel(x_ref, g_ref, w_ref, o_ref, xn_ref):
    @pl.when(pl.program_id(1) == 0)
    def _():
        xn_ref[...] = (_rms(x_ref[...]) * g_ref[...]).astype(BF16)

    o_ref[...] = _dot(xn_ref[...], w_ref[...]).astype(o_ref.dtype)


def _norm_mm(x, g, w, *, tm, tn, out_dtype=BF16):
    T, D = x.shape
    N = w.shape[1]
    tm = min(tm, T)
    tn = min(tn, N)
    return pl.pallas_call(
        _norm_mm_kernel,
        grid=(T // tm, N // tn),
        in_specs=[
            pl.BlockSpec((tm, D), lambda i, j: (i, 0)),
            pl.BlockSpec((1, D), lambda i, j: (0, 0)),
            pl.BlockSpec((D, tn), lambda i, j: (0, j)),
        ],
        out_specs=pl.BlockSpec((tm, tn), lambda i, j: (i, j)),
        out_shape=jax.ShapeDtypeStruct((T, N), out_dtype),
        scratch_shapes=[pltpu.VMEM((tm, D), BF16)],
        compiler_params=_params(("parallel", "arbitrary")),
        name="norm_proj",
    )(x, g.reshape(1, D), w)


def _mm_res_kernel(a_ref, w_ref, r_ref, o_ref):
    o_ref[...] = r_ref[...] + _dot(a_ref[...], w_ref[...])


def _mm_res(a, w, res, *, tm):
    T, K = a.shape
    N = w.shape[1]
    tm = min(tm, T)
    return pl.pallas_call(
        _mm_res_kernel,
        grid=(T // tm,),
        in_specs=[
            pl.BlockSpec((tm, K), lambda i: (i, 0)),
            pl.BlockSpec((K, N), lambda i: (0, 0)),
            pl.BlockSpec((tm, N), lambda i: (i, 0)),
        ],
        out_specs=pl.BlockSpec((tm, N), lambda i: (i, 0)),
        out_shape=jax.ShapeDtypeStruct((T, N), F32),
        compiler_params=_params(("parallel",)),
        name="out_proj",
    )(a, w, res)


def _attn_a_kernel(bias_ref, q_ref, kp_ref, kc_ref, vp_ref, vc_ref, o_ref, lse_ref, *, heads):
    n = pl.program_id(2)
    variant = jnp.minimum(n, 1)
    blk = ATTN_BLOCK
    lane = lax.broadcasted_iota(I32, (1, LANES), 1)
    first_head = lane < HEAD_DIM
    lse_lane = lax.broadcasted_iota(I32, (blk, LANES), 1)
    lse_all = jnp.zeros((blk, LANES), F32)
    group = 4
    for h0 in range(0, heads, group):
        vals, scores = [], []
        for h in range(h0, h0 + group):
            sl = slice(LANES * (h // 2), LANES * (h // 2 + 1))
            q = q_ref[:, sl]
            mine = first_head if h % 2 == 0 else jnp.logical_not(first_head)
            keys = jnp.concatenate([kp_ref[:, sl], kc_ref[:, sl]], axis=0)
            vals.append(jnp.concatenate([vp_ref[:, sl], vc_ref[:, sl]], axis=0))
            qm = jnp.where(mine, q * (1.0 / math.sqrt(HEAD_DIM)), jnp.zeros_like(q))
            scores.append(_dot_t(qm, keys))
        probs, scales = [], []
        for h, s in zip(range(h0, h0 + group), scores):
            s = s + bias_ref[variant, h]
            m = jnp.max(s, axis=-1, keepdims=True)
            e = jnp.exp(s - m)
            den = jnp.sum(e, axis=-1, keepdims=True)
            probs.append(e.astype(BF16))
            scales.append(1.0 / den)
            lse_all = jnp.where(lse_lane == h, m + jnp.log(den), lse_all)
        outs = [_dot(p, v) * sc for p, v, sc in zip(probs, vals, scales)]
        for idx in range(0, group, 2):
            sl = slice(LANES * ((h0 + idx) // 2), LANES * ((h0 + idx) // 2 + 1))
            o_ref[:, sl] = jnp.where(first_head, outs[idx], outs[idx + 1]).astype(o_ref.dtype)
    lse_ref[...] = lse_all


def _alibi_window_bias(dilation, heads):
    blk = ATTN_BLOCK
    a = np.arange(blk)[:, None]
    b = np.arange(2 * blk)[None, :]
    delta = blk + a - b
    valid = (delta >= 0) & (delta <= blk)
    slopes = 2.0 ** (-8.0 * np.arange(1, heads + 1) / heads)
    bias = -slopes[:, None, None] * (delta * dilation).astype(np.float64)[None]
    bias = np.where(valid[None], bias, -np.inf)
    first = bias.copy()
    first[:, :, :blk] = -np.inf
    return jnp.asarray(np.stack([first, bias]), F32)


def _attn_a_group(qkv, *, batch, streams, length, dilation, heads):
    width = heads * HEAD_DIM
    nb = length // ATTN_BLOCK
    x = qkv.reshape(batch, streams, length, 3 * width)
    blk = (None, None, ATTN_BLOCK, width)
    cur = lambda which: (lambda b, c, n: (b, c, n, which))
    prev = lambda which: (lambda b, c, n: (b, c, jnp.maximum(n - 1, 0), which))
    bias = _alibi_window_bias(dilation, heads)
    o, lse = pl.pallas_call(
        functools.partial(_attn_a_kernel, heads=heads),
        grid=(batch, streams, nb),
        in_specs=[
            pl.BlockSpec(bias.shape, lambda b, c, n: (0, 0, 0, 0)),
            pl.BlockSpec(blk, cur(0)),
            pl.BlockSpec(blk, prev(1)),
            pl.BlockSpec(blk, cur(1)),
            pl.BlockSpec(blk, prev(2)),
            pl.BlockSpec(blk, cur(2)),
        ],
        out_specs=[
            pl.BlockSpec(blk, lambda b, c, n: (b, c, n, 0)),
            pl.BlockSpec((None, None, ATTN_BLOCK, LANES), lambda b, c, n: (b, c, n, 0)),
        ],
        out_shape=[
            jax.ShapeDtypeStruct((batch, streams, length, width), BF16),
            jax.ShapeDtypeStruct((batch, streams, length, LANES), F32),
        ],
        compiler_params=_params(("parallel", "parallel", "arbitrary")),
        name=f"dilated_attn_r{dilation}",
    )(bias, x, x, x, x, x)
    return o, lse


def _merge_proj_kernel(o0_ref, o1_ref, o2_ref, l0_ref, l1_ref, l2_ref, e_ref, w_ref, r_ref, out_ref):
    l0, l1, l2 = l0_ref[...], l1_ref[...], l2_ref[...]
    mx = jnp.maximum(jnp.maximum(l0, l1), l2)
    e0, e1, e2 = jnp.exp(l0 - mx), jnp.exp(l1 - mx), jnp.exp(l2 - mx)
    inv = 1.0 / (e0 + e1 + e2)
    expand = e_ref[...]
    acc = None
    for e, o_ref in ((e0, o0_ref), (e1, o1_ref), (e2, o2_ref)):
        hi, lo = _split_bf16(e * inv)
        w_full = _dot(hi, expand) + _dot(lo, expand)
        term = w_full * o_ref[...].astype(F32)
        acc = term if acc is None else acc + term
    out_ref[...] = r_ref[...] + _dot(acc.astype(BF16), w_ref[...])


def _merge_proj(outs, lses, w_out, res, *, heads, tm):
    T, width = outs[0].shape
    N = w_out.shape[1]
    tm = min(tm, T)
    expand = np.zeros((LANES, width), np.float32)
    for h in range(heads):
        expand[h, h * HEAD_DIM:(h + 1) * HEAD_DIM] = 1.0
    expand = jnp.asarray(expand, BF16)
    row = lambda i: (i, 0)
    fixed = lambda i: (0, 0)
    return pl.pallas_call(
        _merge_proj_kernel,
        grid=(T // tm,),
        in_specs=[pl.BlockSpec((tm, width), row)] * 3 + [pl.BlockSpec((tm, LANES), row)] * 3 + [
            pl.BlockSpec((LANES, width), fixed),
            pl.BlockSpec((width, N), fixed),
            pl.BlockSpec((tm, N), row),
        ],
        out_specs=pl.BlockSpec((tm, N), row),
        out_shape=jax.ShapeDtypeStruct((T, N), F32),
        compiler_params=_params(("parallel",)),
        name="merge_out_proj",
    )(*outs, *lses, expand, w_out, res)


def _proj_qk_vt_kernel(x_ref, g_ref, wqk_ref, wvt_ref, qk_ref, vt_ref, *, q_width):
    xn = (_rms(x_ref[...]) * g_ref[...]).astype(BF16)
    qk = _dot(xn, wqk_ref[...])
    qk_ref[:, :q_width] = (qk[:, :q_width] * (1.0 / math.sqrt(HEAD_DIM))).astype(BF16)
    qk_ref[:, q_width:] = qk[:, q_width:].astype(BF16)
    vt_ref[...] = _dot_t(wvt_ref[...], xn).astype(BF16)


def _proj_qk_vt(h, g, w_in, *, batch, seq, tm):
    T, D = h.shape
    width = w_in.shape[1] // 3
    tm = min(tm, seq)
    per_seq = seq // tm
    w = w_in.astype(BF16)
    wqk = w[:, :2 * width]
    wvt = w[:, 2 * width:].T
    fixed = lambda i: (0, 0)
    qk, vt = pl.pallas_call(
        functools.partial(_proj_qk_vt_kernel, q_width=width),
        grid=(T // tm,),
        in_specs=[
            pl.BlockSpec((tm, D), lambda i: (i, 0)),
            pl.BlockSpec((1, D), fixed),
            pl.BlockSpec((D, 2 * width), fixed),
            pl.BlockSpec((width, D), fixed),
        ],
        out_specs=[
            pl.BlockSpec((tm, 2 * width), lambda i: (i, 0)),
            pl.BlockSpec((None, width, tm), lambda i: (i // per_seq, 0, i % per_seq)),
        ],
        out_shape=[
            jax.ShapeDtypeStruct((T, 2 * width), BF16),
            jax.ShapeDtypeStruct((batch, width, seq), BF16),
        ],
        compiler_params=_params(("parallel",)),
        name="norm_proj_qk_vt",
    )(h, g.reshape(1, D), wqk, wvt)
    return qk.reshape(batch, seq, 2 * width), vt


def _head_halves(q):
    lane = lax.broadcasted_iota(I32, (1, LANES), 1)
    first = lane < HEAD_DIM
    zero = jnp.zeros_like(q)
    return jnp.where(first, q, zero), jnp.where(first, zero, q)


def _attn_b_kernel(slope_ref, q_ref, k_ref, vt_ref, lq1_ref, lk1_ref, lq2_ref, lk2_ref, sub_ref,
                   o_ref, *, t, lambda_init):
    h = pl.program_id(1)
    i = pl.program_id(2)
    slope = slope_ref[h]
    n_col = q_ref.shape[0] // t
    qs = [_head_halves(q_ref[col * t:(col + 1) * t, :]) for col in range(n_col)]
    rel = lax.broadcasted_iota(I32, (t, t), 1) - lax.broadcasted_iota(I32, (t, t), 0)
    rel_bias = slope * rel.astype(F32)

    def step(j, carry, modes):
        start = pl.multiple_of(j * t, t)
        k = k_ref[pl.ds(start, t), :]
        vt = vt_ref[:, pl.ds(start, t)]
        new = list(carry)
        chains = [(col, c) for col, mode in enumerate(modes) if mode is not None for c in range(2)]
        scores = [_dot_t(k, qs[col][c]) for col, c in chains]
        probs, alphas = [], []
        for (col, c), s in zip(chains, scores):
            base = 3 * (2 * col + c)
            m, l = carry[base], carry[base + 1]
            shift = slope * ((i * n_col + col - j) * t).astype(F32)
            s = s - rel_bias
            if modes[col]:
                s = jnp.where(rel >= 0, s, -jnp.inf)
            m_new = jnp.maximum(m, jnp.max(s, axis=0, keepdims=True) - shift)
            alpha = jnp.exp(m - m_new)
            p = jnp.exp(s - (m_new + shift))
            new[base] = m_new
            new[base + 1] = alpha * l + jnp.sum(p, axis=0, keepdims=True)
            probs.append(p.astype(BF16))
            alphas.append(alpha)
        for (col, c), p, alpha in zip(chains, probs, alphas):
            base = 3 * (2 * col + c)
            new[base + 2] = alpha * carry[base + 2] + _dot(vt, p)
        return tuple(new)

    init = (jnp.full((1, t), -jnp.inf, F32), jnp.zeros((1, t), F32), jnp.zeros((LANES, t), F32)) * (2 * n_col)
    carry = lax.fori_loop(0, n_col * i, lambda j, c: step(j, c, (False,) * n_col), init)
    for d in range(n_col):
        modes = tuple(None if col < d else (col == d) for col in range(n_col))
        carry = step(n_col * i + d, carry, modes)
    lam = (jnp.exp(jnp.sum(lq1_ref[...] * lk1_ref[...], axis=-1, keepdims=True))
           - jnp.exp(jnp.sum(lq2_ref[...] * lk2_ref[...], axis=-1, keepdims=True)) + lambda_init)
    for col in range(n_col):
        _, l1, a1, _, l2, a2 = carry[6 * col:6 * col + 6]
        o = a1 * (1.0 / l1) - lam * (a2 * (1.0 / l2))
        o = o * lax.rsqrt(jnp.mean(o * o, axis=0, keepdims=True) + RMS_EPS)
        o = o * (sub_ref[...] * (1.0 - lambda_init))
        o_ref[col * t:(col + 1) * t, :] = o.T.astype(o_ref.dtype)


def _attn_b(qk, vt, lq1, lk1, lq2, lk2, subln, *, batch, seq, heads, lambda_init, t, tq):
    t = min(t, seq)
    tq = min(tq, seq)
    width = 2 * HEAD_DIM
    slopes = jnp.asarray(2.0 ** (-8.0 * np.arange(1, heads + 1) / heads), F32)
    vec = lambda a: a.reshape(1, -1).astype(F32)
    small = lambda n: pl.BlockSpec((1, n), lambda b, h, i, s: (0, 0))
    return pl.pallas_call(
        functools.partial(_attn_b_kernel, t=t, lambda_init=lambda_init),
        grid_spec=pltpu.PrefetchScalarGridSpec(
            num_scalar_prefetch=1,
            grid=(batch, heads, seq // tq),
            in_specs=[
                pl.BlockSpec((None, tq, width), lambda b, h, i, s: (b, i, h)),
                pl.BlockSpec((None, seq, width), lambda b, h, i, s: (b, 0, heads + h)),
                pl.BlockSpec((None, width, seq), lambda b, h, i, s: (b, h, 0)),
                small(HEAD_DIM), small(HEAD_DIM), small(HEAD_DIM), small(HEAD_DIM),
                pl.BlockSpec((width, 1), lambda b, h, i, s: (0, 0)),
            ],
            out_specs=pl.BlockSpec((None, tq, width), lambda b, h, i, s: (b, i, h)),
        ),
        out_shape=jax.ShapeDtypeStruct((batch, seq, heads * width), BF16),
        compiler_params=_params(("parallel", "parallel", "arbitrary")),
        name="diff_attn",
    )(slopes, qk, qk, vt, vec(lq1), vec(lk1), vec(lq2), vec(lk2), subln.reshape(width, 1).astype(F32))


def _attn_c_kernel(q_ref, k_ref, vt_ref, later_ref, o_ref):
    i = pl.program_id(2)
    tq = q_ref.shape[0]
    kw = LANES
    n_sub = tq // kw
    qs = _head_halves(q_ref[...])
    later = later_ref[...]
    rel = lax.broadcasted_iota(I32, (kw, tq), 1) - lax.broadcasted_iota(I32, (kw, tq), 0)

    def blocks(kbs, carry, diagonal):
        units = []
        for kb in kbs:
            start = pl.multiple_of(kb * kw, kw)
            k = k_ref[pl.ds(start, kw), :]
            vt = vt_ref[:, pl.ds(start, kw)]
            causal = (rel + (i * tq - kb * kw)) > 0 if diagonal else None
            for head in range(2):
                units.append(dict(head=head, vt=vt, causal=causal, z=_dot_t(k, qs[head])))
        for u in units:
            z = u["z"]
            softplus = jnp.maximum(z, 0.0) + jnp.log(1.0 + jnp.exp(-jnp.abs(z)))
            rest = jnp.where(u["causal"], softplus, 0.0) if diagonal else softplus
            u["logit"] = z - softplus
            u["rest"] = rest
            u["split"] = _split_bf16(rest)
        for u in units:
            hi, lo = u["split"]
            u["after"] = _dot(later, hi) + _dot(later, lo)
        c = [carry[0], carry[2]]
        weights = []
        for u in units:
            a = jnp.exp(u["logit"] - u["after"] - c[u["head"]])
            if diagonal:
                a = jnp.where(u["causal"], a, 0.0)
            weights.append(a.astype(BF16))
            c[u["head"]] = c[u["head"]] + jnp.sum(u["rest"], axis=0, keepdims=True)
        acc = [carry[1], carry[3]]
        for u, a in zip(units, weights):
            acc[u["head"]] = acc[u["head"]] + _dot(u["vt"], a)
        return c[0], acc[0], c[1], acc[1]

    def alive(carry):
        return (jnp.min(jnp.minimum(carry[0], carry[2])) < -F32_EXP_ZERO).astype(I32)

    zc = jnp.zeros((1, tq), F32)
    za = jnp.zeros((LANES, tq), F32)
    carry = blocks([i * n_sub + d for d in reversed(range(n_sub))], (zc, za, zc, za), True)

    def cond(state):
        return jnp.logical_and(state[0] >= 0, state[1] > 0)

    def body(state):
        new = blocks([state[0]], state[2:], False)
        return (state[0] - 1, alive(new)) + new

    out = lax.while_loop(cond, body, (i * n_sub - 1, alive(carry)) + carry)
    row = lax.broadcasted_iota(I32, (LANES, 1), 0)
    o = jnp.where(row < HEAD_DIM, out[3], out[5])
    o_ref[...] = o.T.astype(o_ref.dtype)


def _attn_c(qk, vt, *, batch, seq, heads, tq):
    tq = min(tq, seq)
    pairs = heads // 2
    later = jnp.asarray(np.triu(np.ones((LANES, LANES), np.float32), 1), BF16)
    return pl.pallas_call(
        _attn_c_kernel,
        grid=(batch, pairs, seq // tq),
        in_specs=[
            pl.BlockSpec((None, tq, LANES), lambda b, h, i: (b, i, h)),
            pl.BlockSpec((None, seq, LANES), lambda b, h, i: (b, 0, pairs + h)),
            pl.BlockSpec((None, LANES, seq), lambda b, h, i: (b, h, 0)),
            pl.BlockSpec((LANES, LANES), lambda b, h, i: (0, 0)),
        ],
        out_specs=pl.BlockSpec((None, tq, LANES), lambda b, h, i: (b, i, h)),
        out_shape=jax.ShapeDtypeStruct((batch, seq, heads * HEAD_DIM), BF16),
        compiler_params=_params(("parallel", "parallel", "arbitrary")),
        name="stick_breaking_attn",
    )(qk, qk, vt, later)


def _router_kernel(x_ref, g_ref, whi_ref, wlo_ref, b_ref, u_ref, info_ref, cnt_ref, run_ref,
                   *, groups, per_group):
    G, E = groups, per_group
    tm = x_ref.shape[0]
    ne = G * E

    @pl.when(pl.program_id(0) == 0)
    def _():
        run_ref[...] = jnp.zeros_like(run_ref)

    xn = _rms(x_ref[...]) * g_ref[...]
    hi, lo = _split_bf16(xn)
    whi = whi_ref[...]
    lt = _dot_t(whi, hi) + _dot_t(whi, lo) + _dot_t(wlo_ref[...], hi) + b_ref[...]
    sub = lax.broadcasted_iota(I32, (E, tm), 0)
    coarse = lt[0:G]
    cmax = jnp.max(coarse, axis=0, keepdims=True)
    g_prob = 1.0 / jnp.sum(jnp.exp(coarse - cmax), axis=0, keepdims=True)
    g_idx = jnp.min(jnp.where(coarse == cmax, sub, G), axis=0, keepdims=True)
    fine = jnp.zeros((E, tm), F32)
    for grp in range(G):
        fine = jnp.where(g_idx == grp, lt[G + grp * E:G + (grp + 1) * E], fine)
    fmax = jnp.max(fine, axis=0, keepdims=True)
    fsum = jnp.sum(jnp.exp(fine - fmax), axis=0, keepdims=True)
    i1 = jnp.min(jnp.where(fine == fmax, sub, E), axis=0, keepdims=True)
    rest = jnp.where(sub == i1, -jnp.inf, fine)
    m2 = jnp.max(rest, axis=0, keepdims=True)
    i2 = jnp.min(jnp.where(rest == m2, sub, E), axis=0, keepdims=True)
    p1 = 1.0 / fsum
    p2 = jnp.exp(m2 - fmax) / fsum
    norm = p1 + p2
    gate1 = g_prob * (p1 / norm)
    gate2 = g_prob * (p2 / norm)
    e1 = g_idx * E + i1
    e2 = g_idx * E + i2
    ex = lax.broadcasted_iota(I32, (ne, tm), 0)
    oh1 = (ex == e1).astype(F32)
    oh2 = (ex == e2).astype(F32)
    both = oh1 + oh2
    earlier = _dot(both.astype(BF16), u_ref[...]) + run_ref[:, 0:1]
    r1 = jnp.sum(oh1 * earlier, axis=0, keepdims=True)
    r2 = jnp.sum(oh2 * earlier, axis=0, keepdims=True)
    total = run_ref[...] + jnp.sum(both, axis=1, keepdims=True)
    run_ref[...] = total
    cnt_ref[...] = total
    row = lax.broadcasted_iota(I32, (SUBLANES, tm), 0)
    fields = (e1.astype(F32), e2.astype(F32), r1, r2, gate1, gate2)
    info = jnp.zeros((SUBLANES, tm), F32)
    for idx, f in enumerate(fields):
        info = jnp.where(row == idx, f, info)
    info_ref[...] = info


def _router(h, g, w_coarse, b_coarse, w_fine, b_fine, *, tm):
    T, D = h.shape
    tm = min(tm, T)
    G, ne = w_coarse.shape[1], w_fine.shape[1]
    pad = LANES - G - ne
    wt = jnp.pad(jnp.concatenate([w_coarse, w_fine], axis=1), ((0, 0), (0, pad))).T
    b = jnp.pad(jnp.concatenate([b_coarse, b_fine]), (0, pad)).reshape(LANES, 1)
    whi, wlo = _split_bf16(wt)
    before = jnp.asarray(np.triu(np.ones((tm, tm), np.float32), 1), BF16)
    fixed = lambda i: (0, 0)
    return pl.pallas_call(
        functools.partial(_router_kernel, groups=G, per_group=ne // G),
        grid=(T // tm,),
        in_specs=[
            pl.BlockSpec((tm, D), lambda i: (i, 0)),
            pl.BlockSpec((1, D), fixed),
            pl.BlockSpec((LANES, D), fixed),
            pl.BlockSpec((LANES, D), fixed),
            pl.BlockSpec((LANES, 1), fixed),
            pl.BlockSpec((tm, tm), fixed),
        ],
        out_specs=[pl.BlockSpec((SUBLANES, tm), lambda i: (0, i)), pl.BlockSpec((ne, LANES), fixed)],
        out_shape=[jax.ShapeDtypeStruct((SUBLANES, T), F32), jax.ShapeDtypeStruct((ne, LANES), F32)],
        scratch_shapes=[pltpu.VMEM((ne, LANES), F32)],
        compiler_params=_params(("arbitrary",)),
        name="moe_router",
    )(h, g.reshape(1, D), whi, wlo, b, before)


def _plan(info, counts, *, tile):
    T = info.shape[1]
    ne = counts.shape[0]
    C = FFN_CHUNK
    experts = info[0:2].astype(I32)
    ranks = info[2:4].astype(I32)
    gates = info[4:6]
    cnt = counts[:, 0].astype(I32)
    padded = (cnt + C - 1) // C * C
    pad_end = jnp.cumsum(padded)
    pad_start = pad_end - padded
    ids = jnp.arange(ne, dtype=I32)
    start_of = jnp.sum(jnp.where(experts[:, :, None] == ids, pad_start, 0), axis=-1)
    dest = ranks + start_of
    n_chunks = (MOE_TOP_K * T + ne * (C - 1) + C - 1) // C
    n_used = (pad_end[-1] // C).astype(I32)
    first_row = jnp.minimum(jnp.arange(n_chunks, dtype=I32), n_used - 1) * C
    chunk_expert = jnp.minimum(jnp.sum(pad_end[None, :] <= first_row[:, None], axis=1), ne - 1).astype(I32)
    dest_tiles = dest.reshape(MOE_TOP_K, T // tile, tile).transpose(1, 0, 2).reshape(T // tile, 1, MOE_TOP_K * tile)
    return dest_tiles, gates.T, chunk_expert, n_used.reshape(1), n_chunks


def _dispatch_kernel(dst_ref, h_ref, g_ref, xs_in, xs_out, xbuf, sem):
    del xs_in
    i = pl.program_id(0)
    n = pl.num_programs(0)
    tm, D = h_ref.shape
    slot = i % 2

    def wait(s):
        for _ in range(MOE_TOP_K):
            pltpu.make_async_copy(xbuf.at[s], xs_out.at[pl.ds(0, tm)], sem.at[s]).wait()

    @pl.when(i >= 2)
    def _():
        wait(slot)

    xn = _rms(h_ref[...]) * g_ref[...]
    xbuf[slot] = _pack_bf16_pair(xn[:, :D // 2], xn[:, D // 2:])

    def issue(r, _):
        for k in range(MOE_TOP_K):
            pltpu.make_async_copy(xbuf.at[slot, pl.ds(r, 1)], xs_out.at[pl.ds(dst_ref[0, 0, k * tm + r], 1)],
                                  sem.at[slot]).start()
        return 0

    lax.fori_loop(0, tm, issue, 0, unroll=8)

    @pl.when(i == n - 1)
    def _():
        wait(slot)

        @pl.when(i >= 1)
        def _():
            wait(1 - slot)


def _dispatch(h, g, dest_tiles, n_rows, *, tm):
    T, D = h.shape
    xs0 = jnp.zeros((n_rows, D // 2), U32)
    return pl.pallas_call(
        _dispatch_kernel,
        grid=(T // tm,),
        in_specs=[
            pl.BlockSpec((1, 1, MOE_TOP_K * tm), lambda i: (i, 0, 0), memory_space=pltpu.SMEM),
            pl.BlockSpec((tm, D), lambda i: (i, 0)),
            pl.BlockSpec((1, D), lambda i: (0, 0)),
            pl.BlockSpec(memory_space=pl.ANY),
        ],
        out_specs=pl.BlockSpec(memory_space=pl.ANY),
        out_shape=jax.ShapeDtypeStruct((n_rows, D // 2), U32),
        scratch_shapes=[pltpu.VMEM((2, tm, D // 2), U32), pltpu.SemaphoreType.DMA((2,))],
        input_output_aliases={3: 0},
        compiler_params=_params(("arbitrary",)),
        name="moe_dispatch",
    )(dest_tiles, h, g.reshape(1, D), xs0)


def _ffn_kernel(ce_ref, nu_ref, x_ref, wgu_ref, wdn_ref, y_ref, wgu_bf, wdn_bf, *, d_expert):
    c = pl.program_id(0)
    n_used = nu_ref[0]

    @pl.when(c < n_used)
    def _():
        @pl.when(jnp.logical_or(c == 0, ce_ref[c] != ce_ref[jnp.maximum(c - 1, 0)]))
        def _():
            wgu_bf[...] = wgu_ref[...].astype(BF16)
            wdn_bf[...] = wdn_ref[...].astype(BF16)

        half = x_ref.shape[1]
        xa, xb = _unpack_bf16_pair(x_ref[...])
        gu = _dot(xa, wgu_bf[:half]) + _dot(xb, wgu_bf[half:])
        gate, up = gu[:, :d_expert], gu[:, d_expert:]
        act = (gate * jax.nn.sigmoid(gate)) * up
        y_ref[...] = _dot(act.astype(BF16), wdn_bf[...])

    @pl.when(c >= n_used)
    def _():
        y_ref[...] = jnp.zeros_like(y_ref)


def _ffn(xs, chunk_expert, n_used, w_gate_up, w_down):
    n_rows, half = xs.shape
    C = FFN_CHUNK
    n_chunks = n_rows // C
    D, F2 = w_gate_up.shape[1:]
    d_expert = w_down.shape[1]
    return pl.pallas_call(
        functools.partial(_ffn_kernel, d_expert=d_expert),
        grid_spec=pltpu.PrefetchScalarGridSpec(
            num_scalar_prefetch=2,
            grid=(n_chunks,),
            in_specs=[
                pl.BlockSpec((C, half), lambda c, ce, nu: (jnp.minimum(c, nu[0] - 1), 0)),
                pl.BlockSpec((None, D, F2), lambda c, ce, nu: (ce[c], 0, 0)),
                pl.BlockSpec((None, d_expert, D), lambda c, ce, nu: (ce[c], 0, 0)),
            ],
            out_specs=pl.BlockSpec((C, D), lambda c, ce, nu: (c, 0)),
            scratch_shapes=[pltpu.VMEM((D, F2), BF16), pltpu.VMEM((d_expert, D), BF16)],
        ),
        out_shape=jax.ShapeDtypeStruct((n_rows, D), F32),
        compiler_params=_params(("arbitrary",)),
        name="moe_expert_ffn",
    )(chunk_expert, n_used, xs, w_gate_up, w_down)


def _combine_ple_kernel(dst_ref, dstn_ref, h_ref, gt_ref, p_ref, g_ref, wp_ref, wg_ref, gf_ref, y_hbm,
                        o_ref, ybuf, sem, *, final):
    i = pl.program_id(0)
    n = pl.num_programs(0)
    tm, D = h_ref.shape
    rows = MOE_TOP_K * tm
    slot = i % 2

    def gather(dref, s):
        def issue(r, _):
            for k in range(MOE_TOP_K):
                pltpu.make_async_copy(y_hbm.at[pl.ds(dref[0, 0, k * tm + r], 1)],
                                      ybuf.at[s, pl.ds(k * tm + r, 1)], sem.at[s]).start()
            return 0
        lax.fori_loop(0, tm, issue, 0, unroll=8)

    @pl.when(i == 0)
    def _():
        gather(dst_ref, 0)

    @pl.when(i + 1 < n)
    def _():
        gather(dstn_ref, 1 - slot)

    pltpu.make_async_copy(y_hbm.at[pl.ds(0, rows)], ybuf.at[slot], sem.at[slot]).wait()
    gt = gt_ref[...]
    h = h_ref[...] + (gt[:, 0:1] * ybuf[slot, :tm] + gt[:, 1:2] * ybuf[slot, tm:])
    xn = (_rms(h) * g_ref[...]).astype(BF16)
    gate = jax.nn.sigmoid(_dot(xn, wg_ref[...]))
    out = h + _dot(p_ref[...].astype(BF16), wp_ref[...]) * gate
    if final:
        out = _rms(out) * gf_ref[...]
    o_ref[...] = out


def _combine_ple(h, y, dest_tiles, gates, p, g, w_proj, w_gate, g_final, *, final, tm):
    T, D = h.shape
    Pd = p.shape[1]
    n_tiles = T // tm
    row = lambda i: (i, 0)
    fixed = lambda i: (0, 0)
    smem = functools.partial(pl.BlockSpec, (1, 1, MOE_TOP_K * tm), memory_space=pltpu.SMEM)
    return pl.pallas_call(
        functools.partial(_combine_ple_kernel, final=final),
        grid=(n_tiles,),
        in_specs=[
            smem(lambda i: (i, 0, 0)),
            smem(lambda i: (jnp.minimum(i + 1, n_tiles - 1), 0, 0)),
            pl.BlockSpec((tm, D), row),
            pl.BlockSpec((tm, MOE_TOP_K), row),
            pl.BlockSpec((tm, Pd), row),
            pl.BlockSpec((1, D), fixed),
            pl.BlockSpec((Pd, D), fixed),
            pl.BlockSpec((D, D), fixed),
            pl.BlockSpec((1, D), fixed),
            pl.BlockSpec(memory_space=pl.ANY),
        ],
        out_specs=pl.BlockSpec((tm, D), row),
        out_shape=jax.ShapeDtypeStruct((T, D), F32),
        scratch_shapes=[pltpu.VMEM((2, MOE_TOP_K * tm, D), F32), pltpu.SemaphoreType.DMA((2,))],
        compiler_params=_params(("arbitrary",)),
        name="moe_combine_ple",
    )(dest_tiles, dest_tiles, h, gates, p, g.reshape(1, D), w_proj, w_gate, g_final.reshape(1, D), y)


def _moe_ple(h, p, g_ffn, w_coarse, b_coarse, w_fine, b_fine, w_gate_up, w_down, g_ple, w_proj, w_gate,
             g_final, *, final):
    tile = min(256, h.shape[0])
    info, counts = _router(h, g_ffn, w_coarse, b_coarse, w_fine, b_fine, tm=min(512, h.shape[0]))
    dest_tiles, gates, chunk_expert, n_used, n_chunks = _plan(info, counts, tile=tile)
    xs = _dispatch(h, g_ffn, dest_tiles, n_chunks * FFN_CHUNK, tm=tile)
    y = _ffn(xs, chunk_expert, n_used, w_gate_up, w_down)
    return _combine_ple(h, y, dest_tiles, gates, p, g_ple, w_proj.astype(BF16), w_gate.astype(BF16),
                        g_final, final=final, tm=tile)


def _mixer_a(h, g, w_in, w_out, *, batch, seq):
    T, D = h.shape
    width = A_HEADS * HEAD_DIM
    w = w_in.astype(BF16)
    xn = _norm(h, g, tm=1024)
    outs, lses = [], []
    for grp, (window, r) in enumerate(A_PATTERNS):
        assert window // r == ATTN_BLOCK and seq % (r * ATTN_BLOCK) == 0
        L = seq // r
        to_streams = lambda a: a.reshape(batch, L, r, -1).transpose(0, 2, 1, 3).reshape(T, -1)
        from_streams = lambda a: a.transpose(0, 2, 1, 3).reshape(T, -1)
        xs = xn if r == 1 else to_streams(xn)
        qkv = _mm(xs, w, col0=grp * 3 * width, n_out=3 * width, tm=1024, tn=1024)
        o, lse = _attn_a_group(qkv, batch=batch, streams=r, length=L, dilation=r, heads=A_HEADS)
        outs.append(from_streams(o))
        lses.append(from_streams(lse))
    return _merge_proj(outs, lses, w_out.astype(BF16), h, heads=A_HEADS, tm=512)


def _mixer_b(h, g, w_in, w_out, lq1, lk1, lq2, lk2, subln, lambda_init, *, batch, seq):
    D = h.shape[1]
    heads = D // (2 * HEAD_DIM)
    qk, vt = _proj_qk_vt(h, g, w_in, batch=batch, seq=seq, tm=512)
    o = _attn_b(qk, vt, lq1, lk1, lq2, lk2, subln, batch=batch, seq=seq, heads=heads,
                lambda_init=lambda_init, t=256, tq=512)
    return _mm_res(o.reshape(batch * seq, -1), w_out.astype(BF16), h, tm=512)


def _mixer_c(h, g, w_in, w_out, *, batch, seq):
    D = h.shape[1]
    heads = D // HEAD_DIM
    qk, vt = _proj_qk_vt(h, g, w_in, batch=batch, seq=seq, tm=512)
    o = _attn_c(qk, vt, batch=batch, seq=seq, heads=heads, tq=512)
    return _mm_res(o.reshape(batch * seq, -1), w_out.astype(BF16), h, tm=512)


def kernel(x, p, norm_mix, norm_ffn, norm_ple, norm_final, a_w_in, a_w_out, b_w_in, b_w_out,
           b_lambda_q1, b_lambda_k1, b_lambda_q2, b_lambda_k2, b_subln, c_w_in, c_w_out,
           moe_w_coarse, moe_b_coarse, moe_w_fine, moe_b_fine, moe_w_gate_up, moe_w_down,
           ple_w_proj, ple_w_gate):
    batch, seq, D = x.shape
    depth = p.shape[0]
    T = batch * seq
    h = x.reshape(T, D)
    for i in range(depth):
        kind = i % N_MIXERS
        j = i // N_MIXERS
        if kind == 0:
            h = _mixer_a(h, norm_mix[i], a_w_in[j], a_w_out[j], batch=batch, seq=seq)
        elif kind == 1:
            lambda_init = 0.8 - 0.6 * math.exp(-0.3 * i)
            h = _mixer_b(h, norm_mix[i], b_w_in[j], b_w_out[j], b_lambda_q1[j], b_lambda_k1[j],
                         b_lambda_q2[j], b_lambda_k2[j], b_subln[j], lambda_init, batch=batch, seq=seq)
        else:
            h = _mixer_c(h, norm_mix[i], c_w_in[j], c_w_out[j], batch=batch, seq=seq)
        h = _moe_ple(h, p[i].reshape(T, -1), norm_ffn[i], moe_w_coarse[i], moe_b_coarse[i], moe_w_fine[i],
                     moe_b_fine[i], moe_w_gate_up[i], moe_w_down[i], norm_ple[i], ple_w_proj[i],
                     ple_w_gate[i], norm_final, final=(i == depth - 1))
    return h.reshape(batch, seq, D)
```

```python
import functools
import math

import jax
import jax.numpy as jnp
import numpy as np
from jax import lax
from jax.experimental import pallas as pl
from jax.experimental.pallas import tpu as pltpu

F32 = jnp.float32
BF16 = jnp.bfloat16
U32 = jnp.uint32
I32 = jnp.int32

HEAD_DIM = 64
N_MIXERS = 3
ATTN_BLOCK = 128
RMS_EPS = 1e-6
A_HEADS = 16
A_PATTERNS = ((128, 1), (512, 4), (2048, 16))
MOE_GROUPS = 8
MOE_EXPERTS_PER_GROUP = 8
MOE_TOP_K = 2
FFN_CHUNK = 256
LANES = 128
SUBLANES = 8
F32_EXP_ZERO = -104.0
LOG2_E = math.log2(math.e)
VMEM_LIMIT = 48 * 1024 * 1024


def _params(sem, vmem=VMEM_LIMIT):
    return pltpu.CompilerParams(dimension_semantics=sem, vmem_limit_bytes=vmem)


def _rms(x):
    return x * lax.rsqrt(jnp.mean(x * x, axis=-1, keepdims=True) + RMS_EPS)


def _dot_t(a, b):
    return lax.dot_general(a, b, (((1,), (1,)), ((), ())), preferred_element_type=F32)


def _dot(a, b):
    return jnp.dot(a, b, preferred_element_type=F32)


def _split_bf16(x):
    hi = x.astype(BF16)
    lo = (x - hi.astype(F32)).astype(BF16)
    return hi, lo


def _bits(x):
    return lax.bitcast_convert_type(x, U32)


def _pack_bf16_pair(a, b):
    a = a.astype(BF16).astype(F32)
    b = b.astype(BF16).astype(F32)
    return (_bits(a) >> 16) | _bits(b)


def _unpack_bf16_pair(w):
    a = lax.bitcast_convert_type(w << 16, F32).astype(BF16)
    b = lax.bitcast_convert_type(w & jnp.uint32(0xFFFF0000), F32).astype(BF16)
    return a, b


def _norm_kernel(x_ref, g_ref, o_ref):
    o_ref[...] = (_rms(x_ref[...]) * g_ref[...]).astype(o_ref.dtype)


def _norm(x, g, *, tm):
    T, D = x.shape
    tm = min(tm, T)
    return pl.pallas_call(
        _norm_kernel,
        grid=(T // tm,),
        in_specs=[pl.BlockSpec((tm, D), lambda i: (i, 0)), pl.BlockSpec((1, D), lambda i: (0, 0))],
        out_specs=pl.BlockSpec((tm, D), lambda i: (i, 0)),
        out_shape=jax.ShapeDtypeStruct((T, D), BF16),
        compiler_params=_params(("parallel",)),
        name="rmsnorm",
    )(x, g.reshape(1, D))


def _mm_kernel(a_ref, w_ref, o_ref):
    o_ref[...] = _dot(a_ref[...], w_ref[...]).astype(o_ref.dtype)


def _mm(a, w, *, col0, n_out, tm, tn):
    T, K = a.shape
    N = n_out
    tm = min(tm, T)
    tn = min(tn, N)
    first = col0 // tn
    return pl.pallas_call(
        _mm_kernel,
        grid=(T // tm, N // tn),
        in_specs=[pl.BlockSpec((tm, K), lambda i, j: (i, 0)), pl.BlockSpec((K, tn), lambda i, j: (0, first + j))],
        out_specs=pl.BlockSpec((tm, tn), lambda i, j: (i, j)),
        out_shape=jax.ShapeDtypeStruct((T, N), BF16),
        compiler_params=_params(("parallel", "arbitrary")),
        name="proj",
    )(a, w)


def _norm_mm_kernel(x_ref, g_ref, w_ref, o_ref, xn_ref):
    @pl.when(pl.program_id(1) == 0)
    def _():
        xn_ref[...] = (_rms(x_ref[...]) * g_ref[...]).astype(BF16)

    o_ref[...] = _dot(xn_ref[...], w_ref[...]).astype(o_ref.dtype)


def _norm_mm(x, g, w, *, tm, tn, out_dtype=BF16):
    T, D = x.shape
    N = w.shape[1]
    tm = min(tm, T)
    tn = min(tn, N)
    return pl.pallas_call(
        _norm_mm_kernel,
        grid=(T // tm, N // tn),
        in_specs=[
            pl.BlockSpec((tm, D), lambda i, j: (i, 0)),
            pl.BlockSpec((1, D), lambda i, j: (0, 0)),
            pl.BlockSpec((D, tn), lambda i, j: (0, j)),
        ],
        out_specs=pl.BlockSpec((tm, tn), lambda i, j: (i, j)),
        out_shape=jax.ShapeDtypeStruct((T, N), out_dtype),
        scratch_shapes=[pltpu.VMEM((tm, D), BF16)],
        compiler_params=_params(("parallel", "arbitrary")),
        name="norm_proj",
    )(x, g.reshape(1, D), w)


def _mm_res_kernel(a_ref, w_ref, r_ref, o_ref):
    o_ref[...] = r_ref[...] + _dot(a_ref[...], w_ref[...])


def _mm_res(a, w, res, *, tm):
    T, K = a.shape
    N = w.shape[1]
    tm = min(tm, T)
    return pl.pallas_call(
        _mm_res_kernel,
        grid=(T // tm,),
        in_specs=[
            pl.BlockSpec((tm, K), lambda i: (i, 0)),
            pl.BlockSpec((K, N), lambda i: (0, 0)),
            pl.BlockSpec((tm, N), lambda i: (i, 0)),
        ],
        out_specs=pl.BlockSpec((tm, N), lambda i: (i, 0)),
        out_shape=jax.ShapeDtypeStruct((T, N), F32),
        compiler_params=_params(("parallel",)),
        name="out_proj",
    )(a, w, res)


def _attn_a_kernel(bias_ref, q_ref, kp_ref, kc_ref, vp_ref, vc_ref, o_ref, lse_ref, *, heads):
    n = pl.program_id(2)
    variant = jnp.minimum(n, 1)
    blk = ATTN_BLOCK
    lane = lax.broadcasted_iota(I32, (1, LANES), 1)
    first_head = lane < HEAD_DIM
    lse_lane = lax.broadcasted_iota(I32, (blk, LANES), 1)
    lse_all = jnp.zeros((blk, LANES), F32)
    group = 4
    for h0 in range(0, heads, group):
        vals, scores = [], []
        for h in range(h0, h0 + group):
            sl = slice(LANES * (h // 2), LANES * (h // 2 + 1))
            q = q_ref[:, sl]
            mine = first_head if h % 2 == 0 else jnp.logical_not(first_head)
            keys = jnp.concatenate([kp_ref[:, sl], kc_ref[:, sl]], axis=0)
            vals.append(jnp.concatenate([vp_ref[:, sl], vc_ref[:, sl]], axis=0))
            qm = jnp.where(mine, q * (1.0 / math.sqrt(HEAD_DIM)), jnp.zeros_like(q))
            scores.append(_dot_t(qm, keys))
        probs, scales = [], []
        for h, s in zip(range(h0, h0 + group), scores):
            s = s + bias_ref[variant, h]
            m = jnp.max(s, axis=-1, keepdims=True)
            e = jnp.exp(s - m)
            den = jnp.sum(e, axis=-1, keepdims=True)
            probs.append(e.astype(BF16))
            scales.append(1.0 / den)
            lse_all = jnp.where(lse_lane == h, m + jnp.log(den), lse_all)
        outs = [_dot(p, v) * sc for p, v, sc in zip(probs, vals, scales)]
        for idx in range(0, group, 2):
            sl = slice(LANES * ((h0 + idx) // 2), LANES * ((h0 + idx) // 2 + 1))
            o_ref[:, sl] = jnp.where(first_head, outs[idx], outs[idx + 1]).astype(o_ref.dtype)
    lse_ref[...] = lse_all


def _alibi_window_bias(dilation, heads):
    blk = ATTN_BLOCK
    a = np.arange(blk)[:, None]
    b = np.arange(2 * blk)[None, :]
    delta = blk + a - b
    valid = (delta >= 0) & (delta <= blk)
    slopes = 2.0 ** (-8.0 * np.arange(1, heads + 1) / heads)
    bias = -slopes[:, None, None] * (delta * dilation).astype(np.float64)[None]
    bias = np.where(valid[None], bias, -np.inf)
    first = bias.copy()
    first[:, :, :blk] = -np.inf
    return jnp.asarray(np.stack([first, bias]), F32)


def _attn_a_group(qkv, *, batch, streams, length, dilation, heads):
    width = heads * HEAD_DIM
    nb = length // ATTN_BLOCK
    x = qkv.reshape(batch, streams, length, 3 * width)
    blk = (None, None, ATTN_BLOCK, width)
    cur = lambda which: (lambda b, c, n: (b, c, n, which))
    prev = lambda which: (lambda b, c, n: (b, c, jnp.maximum(n - 1, 0), which))
    bias = _alibi_window_bias(dilation, heads)
    o, lse = pl.pallas_call(
        functools.partial(_attn_a_kernel, heads=heads),
        grid=(batch, streams, nb),
        in_specs=[
            pl.BlockSpec(bias.shape, lambda b, c, n: (0, 0, 0, 0)),
            pl.BlockSpec(blk, cur(0)),
            pl.BlockSpec(blk, prev(1)),
            pl.BlockSpec(blk, cur(1)),
            pl.BlockSpec(blk, prev(2)),
            pl.BlockSpec(blk, cur(2)),
        ],
        out_specs=[
            pl.BlockSpec(blk, lambda b, c, n: (b, c, n, 0)),
            pl.BlockSpec((None, None, ATTN_BLOCK, LANES), lambda b, c, n: (b, c, n, 0)),
        ],
        out_shape=[
            jax.ShapeDtypeStruct((batch, streams, length, width), BF16),
            jax.ShapeDtypeStruct((batch, streams, length, LANES), F32),
        ],
        compiler_params=_params(("parallel", "parallel", "arbitrary")),
        name=f"dilated_attn_r{dilation}",
    )(bias, x, x, x, x, x)
    return o, lse


def _merge_proj_kernel(o0_ref, o1_ref, o2_ref, l0_ref, l1_ref, l2_ref, e_ref, w_ref, r_ref, out_ref):
    l0, l1, l2 = l0_ref[...], l1_ref[...], l2_ref[...]
    mx = jnp.maximum(jnp.maximum(l0, l1), l2)
    e0, e1, e2 = jnp.exp(l0 - mx), jnp.exp(l1 - mx), jnp.exp(l2 - mx)
    inv = 1.0 / (e0 + e1 + e2)
    expand = e_ref[...]
    acc = None
    for e, o_ref in ((e0, o0_ref), (e1, o1_ref), (e2, o2_ref)):
        hi, lo = _split_bf16(e * inv)
        w_full = _dot(hi, expand) + _dot(lo, expand)
        term = w_full * o_ref[...].astype(F32)
        acc = term if acc is None else acc + term
    out_ref[...] = r_ref[...] + _dot(acc.astype(BF16), w_ref[...])


def _merge_proj(outs, lses, w_out, res, *, heads, tm):
    T, width = outs[0].shape
    N = w_out.shape[1]
    tm = min(tm, T)
    expand = np.zeros((LANES, width), np.float32)
    for h in range(heads):
        expand[h, h * HEAD_DIM:(h + 1) * HEAD_DIM] = 1.0
    expand = jnp.asarray(expand, BF16)
    row = lambda i: (i, 0)
    fixed = lambda i: (0, 0)
    return pl.pallas_call(
        _merge_proj_kernel,
        grid=(T // tm,),
        in_specs=[pl.BlockSpec((tm, width), row)] * 3 + [pl.BlockSpec((tm, LANES), row)] * 3 + [
            pl.BlockSpec((LANES, width), fixed),
            pl.BlockSpec((width, N), fixed),
            pl.BlockSpec((tm, N), row),
        ],
        out_specs=pl.BlockSpec((tm, N), row),
        out_shape=jax.ShapeDtypeStruct((T, N), F32),
        compiler_params=_params(("parallel",)),
        name="merge_out_proj",
    )(*outs, *lses, expand, w_out, res)


def _proj_qk_vt_kernel(x_ref, g_ref, wqk_ref, wvt_ref, qk_ref, vt_ref, *, q_width, q_scale):
    xn = (_rms(x_ref[...]) * g_ref[...]).astype(BF16)
    qk = _dot(xn, wqk_ref[...])
    qk_ref[:, :q_width] = (qk[:, :q_width] * q_scale).astype(BF16)
    qk_ref[:, q_width:] = qk[:, q_width:].astype(BF16)
    vt_ref[...] = _dot_t(wvt_ref[...], xn).astype(BF16)


def _proj_qk_vt(h, g, w_in, *, batch, seq, tm, q_scale):
    T, D = h.shape
    width = w_in.shape[1] // 3
    tm = min(tm, seq)
    per_seq = seq // tm
    w = w_in.astype(BF16)
    wqk = w[:, :2 * width]
    wvt = w[:, 2 * width:].T
    fixed = lambda i: (0, 0)
    qk, vt = pl.pallas_call(
        functools.partial(_proj_qk_vt_kernel, q_width=width, q_scale=q_scale),
        grid=(T // tm,),
        in_specs=[
            pl.BlockSpec((tm, D), lambda i: (i, 0)),
            pl.BlockSpec((1, D), fixed),
            pl.BlockSpec((D, 2 * width), fixed),
            pl.BlockSpec((width, D), fixed),
        ],
        out_specs=[
            pl.BlockSpec((tm, 2 * width), lambda i: (i, 0)),
            pl.BlockSpec((None, width, tm), lambda i: (i // per_seq, 0, i % per_seq)),
        ],
        out_shape=[
            jax.ShapeDtypeStruct((T, 2 * width), BF16),
            jax.ShapeDtypeStruct((batch, width, seq), BF16),
        ],
        compiler_params=_params(("parallel",)),
        name="norm_proj_qk_vt",
    )(h, g.reshape(1, D), wqk, wvt)
    return qk.reshape(batch, seq, 2 * width), vt


def _head_halves(q):
    lane = lax.broadcasted_iota(I32, (1, LANES), 1)
    first = lane < HEAD_DIM
    zero = jnp.zeros_like(q)
    return jnp.where(first, q, zero), jnp.where(first, zero, q)


def _attn_b_kernel(slope_ref, q_ref, k_ref, vt_ref, lq1_ref, lk1_ref, lq2_ref, lk2_ref, sub_ref,
                   o_ref, *, t, lambda_init):
    h = pl.program_id(1)
    i = pl.program_id(2)
    slope = slope_ref[h]
    n_col = q_ref.shape[0] // t
    qs = [_head_halves(q_ref[col * t:(col + 1) * t, :]) for col in range(n_col)]
    rel = lax.broadcasted_iota(I32, (t, t), 1) - lax.broadcasted_iota(I32, (t, t), 0)
    rel_bias = slope * rel.astype(F32)

    def step(js, carry, modes):
        new = list(carry)
        chains = [(col, c) for col, mode in enumerate(modes) if mode is not None for c in range(2)]
        starts = [pl.multiple_of(j * t, t) for j in js]
        scores = [[_dot_t(k_ref[pl.ds(st, t), :], qs[col][c]) for col, c in chains] for st in starts]
        for j, st, block_scores in zip(js, starts, scores):
            vt = vt_ref[:, pl.ds(st, t)]
            probs, alphas = [], []
            for (col, c), s in zip(chains, block_scores):
                base = 3 * (2 * col + c)
                m, l = new[base], new[base + 1]
                shift = slope * ((i * n_col + col - j) * t).astype(F32)
                s = s - rel_bias
                if modes[col]:
                    s = jnp.where(rel >= 0, s, -jnp.inf)
                m_new = jnp.maximum(m, jnp.max(s, axis=0, keepdims=True) - shift)
                alpha = jnp.exp2(m - m_new)
                p = jnp.exp2(s - (m_new + shift))
                new[base] = m_new
                new[base + 1] = alpha * l + jnp.sum(p, axis=0, keepdims=True)
                probs.append(p.astype(BF16))
                alphas.append(alpha)
            for (col, c), p, alpha in zip(chains, probs, alphas):
                base = 3 * (2 * col + c)
                new[base + 2] = alpha * new[base + 2] + _dot(vt, p)
        return tuple(new)

    init = (jnp.full((1, t), -jnp.inf, F32), jnp.zeros((1, t), F32), jnp.zeros((LANES, t), F32)) * (2 * n_col)
    carry = lax.fori_loop(0, i, lambda jj, c: step([n_col * jj + b for b in range(n_col)], c, (False,) * n_col),
                          init)
    for d in range(n_col):
        modes = tuple(None if col < d else (col == d) for col in range(n_col))
        carry = step([n_col * i + d], carry, modes)
    lam = (jnp.exp(jnp.sum(lq1_ref[...] * lk1_ref[...], axis=-1, keepdims=True))
           - jnp.exp(jnp.sum(lq2_ref[...] * lk2_ref[...], axis=-1, keepdims=True)) + lambda_init)
    for col in range(n_col):
        _, l1, a1, _, l2, a2 = carry[6 * col:6 * col + 6]
        o = a1 * (1.0 / l1) - lam * (a2 * (1.0 / l2))
        o = o * lax.rsqrt(jnp.mean(o * o, axis=0, keepdims=True) + RMS_EPS)
        o = o * (sub_ref[...] * (1.0 - lambda_init))
        o_ref[col * t:(col + 1) * t, :] = o.T.astype(o_ref.dtype)


def _attn_b(qk, vt, lq1, lk1, lq2, lk2, subln, *, batch, seq, heads, lambda_init, t, tq):
    t = min(t, seq)
    tq = min(tq, seq)
    width = 2 * HEAD_DIM
    slopes = jnp.asarray(LOG2_E * 2.0 ** (-8.0 * np.arange(1, heads + 1) / heads), F32)
    vec = lambda a: a.reshape(1, -1).astype(F32)
    small = lambda n: pl.BlockSpec((1, n), lambda b, h, i, s: (0, 0))
    return pl.pallas_call(
        functools.partial(_attn_b_kernel, t=t, lambda_init=lambda_init),
        grid_spec=pltpu.PrefetchScalarGridSpec(
            num_scalar_prefetch=1,
            grid=(batch, heads, seq // tq),
            in_specs=[
                pl.BlockSpec((None, tq, width), lambda b, h, i, s: (b, i, h)),
                pl.BlockSpec((None, seq, width), lambda b, h, i, s: (b, 0, heads + h)),
                pl.BlockSpec((None, width, seq), lambda b, h, i, s: (b, h, 0)),
                small(HEAD_DIM), small(HEAD_DIM), small(HEAD_DIM), small(HEAD_DIM),
                pl.BlockSpec((width, 1), lambda b, h, i, s: (0, 0)),
            ],
            out_specs=pl.BlockSpec((None, tq, width), lambda b, h, i, s: (b, i, h)),
        ),
        out_shape=jax.ShapeDtypeStruct((batch, seq, heads * width), BF16),
        compiler_params=_params(("parallel", "parallel", "arbitrary")),
        name="diff_attn",
    )(slopes, qk, qk, vt, vec(lq1), vec(lk1), vec(lq2), vec(lk2), subln.reshape(width, 1).astype(F32))


def _attn_c_kernel(q_ref, k_ref, vt_ref, later_ref, o_ref):
    i = pl.program_id(2)
    tq = q_ref.shape[0]
    kw = LANES
    n_sub = tq // kw
    qs = _head_halves(q_ref[...])
    later = later_ref[...]
    rel = lax.broadcasted_iota(I32, (kw, tq), 1) - lax.broadcasted_iota(I32, (kw, tq), 0)

    def blocks(kbs, carry, diagonal):
        units = []
        for kb in kbs:
            start = pl.multiple_of(kb * kw, kw)
            k = k_ref[pl.ds(start, kw), :]
            vt = vt_ref[:, pl.ds(start, kw)]
            causal = (rel + (i * tq - kb * kw)) > 0 if diagonal else None
            for head in range(2):
                units.append(dict(head=head, vt=vt, causal=causal, z=_dot_t(k, qs[head])))
        for u in units:
            z = u["z"]
            softplus = jnp.maximum(z, 0.0) + jnp.log(1.0 + jnp.exp(-jnp.abs(z)))
            rest = jnp.where(u["causal"], softplus, 0.0) if diagonal else softplus
            u["logit"] = z - softplus
            u["rest"] = rest
            u["split"] = _split_bf16(rest)
        for u in units:
            hi, lo = u["split"]
            u["after"] = _dot(later, hi) + _dot(later, lo)
        c = [carry[0], carry[2]]
        weights = []
        for u in units:
            a = jnp.exp(u["logit"] - u["after"] - c[u["head"]])
            if diagonal:
                a = jnp.where(u["causal"], a, 0.0)
            weights.append(a.astype(BF16))
            c[u["head"]] = c[u["head"]] + jnp.sum(u["rest"], axis=0, keepdims=True)
        acc = [carry[1], carry[3]]
        for u, a in zip(units, weights):
            acc[u["head"]] = acc[u["head"]] + _dot(u["vt"], a)
        return c[0], acc[0], c[1], acc[1]

    def alive(carry):
        return (jnp.min(jnp.minimum(carry[0], carry[2])) < -F32_EXP_ZERO).astype(I32)

    zc = jnp.zeros((1, tq), F32)
    za = jnp.zeros((LANES, tq), F32)
    carry = blocks([i * n_sub + d for d in reversed(range(n_sub))], (zc, za, zc, za), True)

    def cond(state):
        return jnp.logical_and(state[0] >= 0, state[1] > 0)

    def body(state):
        new = blocks([state[0], state[0] - 1], state[2:], False)
        return (state[0] - 2, alive(new)) + new

    out = lax.while_loop(cond, body, (i * n_sub - 1, alive(carry)) + carry)
    row = lax.broadcasted_iota(I32, (LANES, 1), 0)
    o = jnp.where(row < HEAD_DIM, out[3], out[5])
    o_ref[...] = o.T.astype(o_ref.dtype)


def _attn_c(qk, vt, *, batch, seq, heads, tq):
    tq = min(tq, seq)
    assert seq % tq == 0 and (tq // LANES) % 2 == 0
    pairs = heads // 2
    later = jnp.asarray(np.triu(np.ones((LANES, LANES), np.float32), 1), BF16)
    return pl.pallas_call(
        _attn_c_kernel,
        grid=(batch, pairs, seq // tq),
        in_specs=[
            pl.BlockSpec((None, tq, LANES), lambda b, h, i: (b, i, h)),
            pl.BlockSpec((None, seq, LANES), lambda b, h, i: (b, 0, pairs + h)),
            pl.BlockSpec((None, LANES, seq), lambda b, h, i: (b, h, 0)),
            pl.BlockSpec((LANES, LANES), lambda b, h, i: (0, 0)),
        ],
        out_specs=pl.BlockSpec((None, tq, LANES), lambda b, h, i: (b, i, h)),
        out_shape=jax.ShapeDtypeStruct((batch, seq, heads * HEAD_DIM), BF16),
        compiler_params=_params(("parallel", "parallel", "arbitrary")),
        name="stick_breaking_attn",
    )(qk, qk, vt, later)


def _router_kernel(x_ref, g_ref, whi_ref, wlo_ref, b_ref, u_ref, info_ref, cnt_ref, run_ref,
                   *, groups, per_group):
    G, E = groups, per_group
    tm = x_ref.shape[0]
    ne = G * E

    @pl.when(pl.program_id(0) == 0)
    def _():
        run_ref[...] = jnp.zeros_like(run_ref)

    xn = _rms(x_ref[...]) * g_ref[...]
    hi, lo = _split_bf16(xn)
    whi = whi_ref[...]
    lt = _dot_t(whi, hi) + _dot_t(whi, lo) + _dot_t(wlo_ref[...], hi) + b_ref[...]
    sub = lax.broadcasted_iota(I32, (E, tm), 0)
    coarse = lt[0:G]
    cmax = jnp.max(coarse, axis=0, keepdims=True)
    g_prob = 1.0 / jnp.sum(jnp.exp(coarse - cmax), axis=0, keepdims=True)
    g_idx = jnp.min(jnp.where(coarse == cmax, sub, G), axis=0, keepdims=True)
    fine = jnp.zeros((E, tm), F32)
    for grp in range(G):
        fine = jnp.where(g_idx == grp, lt[G + grp * E:G + (grp + 1) * E], fine)
    fmax = jnp.max(fine, axis=0, keepdims=True)
    fsum = jnp.sum(jnp.exp(fine - fmax), axis=0, keepdims=True)
    i1 = jnp.min(jnp.where(fine == fmax, sub, E), axis=0, keepdims=True)
    rest = jnp.where(sub == i1, -jnp.inf, fine)
    m2 = jnp.max(rest, axis=0, keepdims=True)
    i2 = jnp.min(jnp.where(rest == m2, sub, E), axis=0, keepdims=True)
    p1 = 1.0 / fsum
    p2 = jnp.exp(m2 - fmax) / fsum
    norm = p1 + p2
    gate1 = g_prob * (p1 / norm)
    gate2 = g_prob * (p2 / norm)
    e1 = g_idx * E + i1
    e2 = g_idx * E + i2
    ex = lax.broadcasted_iota(I32, (ne, tm), 0)
    oh1 = (ex == e1).astype(F32)
    oh2 = (ex == e2).astype(F32)
    both = oh1 + oh2
    earlier = _dot(both.astype(BF16), u_ref[...]) + run_ref[:, 0:1]
    r1 = jnp.sum(oh1 * earlier, axis=0, keepdims=True)
    r2 = jnp.sum(oh2 * earlier, axis=0, keepdims=True)
    total = run_ref[...] + jnp.sum(both, axis=1, keepdims=True)
    run_ref[...] = total
    cnt_ref[...] = total
    row = lax.broadcasted_iota(I32, (SUBLANES, tm), 0)
    fields = (e1.astype(F32), e2.astype(F32), r1, r2, gate1, gate2)
    info = jnp.zeros((SUBLANES, tm), F32)
    for idx, f in enumerate(fields):
        info = jnp.where(row == idx, f, info)
    info_ref[...] = info


def _router(h, g, w_coarse, b_coarse, w_fine, b_fine, *, tm):
    T, D = h.shape
    tm = min(tm, T)
    G, ne = w_coarse.shape[1], w_fine.shape[1]
    pad = LANES - G - ne
    wt = jnp.pad(jnp.concatenate([w_coarse, w_fine], axis=1), ((0, 0), (0, pad))).T
    b = jnp.pad(jnp.concatenate([b_coarse, b_fine]), (0, pad)).reshape(LANES, 1)
    whi, wlo = _split_bf16(wt)
    before = jnp.asarray(np.triu(np.ones((tm, tm), np.float32), 1), BF16)
    fixed = lambda i: (0, 0)
    return pl.pallas_call(
        functools.partial(_router_kernel, groups=G, per_group=ne // G),
        grid=(T // tm,),
        in_specs=[
            pl.BlockSpec((tm, D), lambda i: (i, 0)),
            pl.BlockSpec((1, D), fixed),
            pl.BlockSpec((LANES, D), fixed),
            pl.BlockSpec((LANES, D), fixed),
            pl.BlockSpec((LANES, 1), fixed),
            pl.BlockSpec((tm, tm), fixed),
        ],
        out_specs=[pl.BlockSpec((SUBLANES, tm), lambda i: (0, i)), pl.BlockSpec((ne, LANES), fixed)],
        out_shape=[jax.ShapeDtypeStruct((SUBLANES, T), F32), jax.ShapeDtypeStruct((ne, LANES), F32)],
        scratch_shapes=[pltpu.VMEM((ne, LANES), F32)],
        compiler_params=_params(("arbitrary",)),
        name="moe_router",
    )(h, g.reshape(1, D), whi, wlo, b, before)


def _plan(info, counts, *, tile):
    T = info.shape[1]
    ne = counts.shape[0]
    C = FFN_CHUNK
    experts = info[0:2].astype(I32)
    ranks = info[2:4].astype(I32)
    gates = info[4:6]
    cnt = counts[:, 0].astype(I32)
    padded = (cnt + C - 1) // C * C
    pad_end = jnp.cumsum(padded)
    pad_start = pad_end - padded
    ids = jnp.arange(ne, dtype=I32)
    start_of = jnp.sum(jnp.where(experts[:, :, None] == ids, pad_start, 0), axis=-1)
    dest = ranks + start_of
    n_chunks = (MOE_TOP_K * T + ne * (C - 1) + C - 1) // C
    n_used = (pad_end[-1] // C).astype(I32)
    first_row = jnp.minimum(jnp.arange(n_chunks, dtype=I32), n_used - 1) * C
    chunk_expert = jnp.minimum(jnp.sum(pad_end[None, :] <= first_row[:, None], axis=1), ne - 1).astype(I32)
    dest_tiles = dest.reshape(MOE_TOP_K, T // tile, tile).transpose(1, 0, 2).reshape(T // tile, 1, MOE_TOP_K * tile)
    return dest_tiles, gates.T, chunk_expert, n_used.reshape(1), n_chunks


def _dispatch_kernel(dst_ref, h_ref, g_ref, xs_in, xs_out, xbuf, sem):
    del xs_in
    i = pl.program_id(0)
    n = pl.num_programs(0)
    tm, D = h_ref.shape
    slot = i % 2

    def wait(s):
        for _ in range(MOE_TOP_K):
            pltpu.make_async_copy(xbuf.at[s], xs_out.at[pl.ds(0, tm)], sem.at[s]).wait()

    @pl.when(i >= 2)
    def _():
        wait(slot)

    xn = _rms(h_ref[...]) * g_ref[...]
    xbuf[slot] = _pack_bf16_pair(xn[:, :D // 2], xn[:, D // 2:])

    def issue(r, _):
        for k in range(MOE_TOP_K):
            pltpu.make_async_copy(xbuf.at[slot, pl.ds(r, 1)], xs_out.at[pl.ds(dst_ref[0, 0, k * tm + r], 1)],
                                  sem.at[slot]).start()
        return 0

    lax.fori_loop(0, tm, issue, 0, unroll=True)

    @pl.when(i == n - 1)
    def _():
        wait(slot)

        @pl.when(i >= 1)
        def _():
            wait(1 - slot)


def _dispatch(h, g, dest_tiles, n_rows, *, tm):
    T, D = h.shape
    xs0 = jnp.zeros((n_rows, D // 2), U32)
    return pl.pallas_call(
        _dispatch_kernel,
        grid=(T // tm,),
        in_specs=[
            pl.BlockSpec((1, 1, MOE_TOP_K * tm), lambda i: (i, 0, 0), memory_space=pltpu.SMEM),
            pl.BlockSpec((tm, D), lambda i: (i, 0)),
            pl.BlockSpec((1, D), lambda i: (0, 0)),
            pl.BlockSpec(memory_space=pl.ANY),
        ],
        out_specs=pl.BlockSpec(memory_space=pl.ANY),
        out_shape=jax.ShapeDtypeStruct((n_rows, D // 2), U32),
        scratch_shapes=[pltpu.VMEM((2, tm, D // 2), U32), pltpu.SemaphoreType.DMA((2,))],
        input_output_aliases={3: 0},
        compiler_params=_params(("arbitrary",)),
        name="moe_dispatch",
    )(dest_tiles, h, g.reshape(1, D), xs0)


def _ffn_kernel(ce_ref, nu_ref, x_ref, wgu_ref, wdn_ref, y_ref, wgu_bf, wdn_bf, *, d_expert):
    c = pl.program_id(0)
    n_used = nu_ref[0]

    @pl.when(c < n_used)
    def _():
        @pl.when(jnp.logical_or(c == 0, ce_ref[c] != ce_ref[jnp.maximum(c - 1, 0)]))
        def _():
            wgu_bf[...] = wgu_ref[...].astype(BF16)
            wdn_bf[...] = wdn_ref[...].astype(BF16)

        half = x_ref.shape[1]
        xa, xb = _unpack_bf16_pair(x_ref[...])
        n_blk = 2
        w = d_expert // n_blk
        proj = lambda lo: _dot(xa, wgu_bf[:half, lo:lo + w]) + _dot(xb, wgu_bf[half:, lo:lo + w])
        gates = [proj(b * w) for b in range(n_blk)]
        ups = [proj(d_expert + b * w) for b in range(n_blk)]
        acts = [((g * jax.nn.sigmoid(g)) * u).astype(BF16) for g, u in zip(gates, ups)]
        y = _dot(acts[0], wdn_bf[:w])
        for b in range(1, n_blk):
            y = y + _dot(acts[b], wdn_bf[b * w:(b + 1) * w])
        y_ref[...] = y

    @pl.when(c >= n_used)
    def _():
        y_ref[...] = jnp.zeros_like(y_ref)


def _ffn(xs, chunk_expert, n_used, w_gate_up, w_down, layer):
    n_rows, half = xs.shape
    C = FFN_CHUNK
    n_chunks = n_rows // C
    D, F2 = w_gate_up.shape[2:]
    d_expert = w_down.shape[2]
    return pl.pallas_call(
        functools.partial(_ffn_kernel, d_expert=d_expert),
        grid_spec=pltpu.PrefetchScalarGridSpec(
            num_scalar_prefetch=2,
            grid=(n_chunks,),
            in_specs=[
                pl.BlockSpec((C, half), lambda c, ce, nu: (jnp.minimum(c, nu[0] - 1), 0)),
                pl.BlockSpec((None, None, D, F2), lambda c, ce, nu: (layer, ce[c], 0, 0)),
                pl.BlockSpec((None, None, d_expert, D), lambda c, ce, nu: (layer, ce[c], 0, 0)),
            ],
            out_specs=pl.BlockSpec((C, D), lambda c, ce, nu: (c, 0)),
            scratch_shapes=[pltpu.VMEM((D, F2), BF16), pltpu.VMEM((d_expert, D), BF16)],
        ),
        out_shape=jax.ShapeDtypeStruct((n_rows, D), F32),
        compiler_params=_params(("arbitrary",)),
        name="moe_expert_ffn",
    )(chunk_expert, n_used, xs, w_gate_up, w_down)


def _combine_ple_kernel(dst_ref, dstn_ref, h_ref, gt_ref, p_ref, g_ref, wp_ref, wg_ref, gf_ref, y_hbm,
                        o_ref, ybuf, sem, *, final):
    i = pl.program_id(0)
    n = pl.num_programs(0)
    tm, D = h_ref.shape
    rows = MOE_TOP_K * tm
    slot = i % 2

    def gather(dref, s):
        def issue(r, _):
            for k in range(MOE_TOP_K):
                pltpu.make_async_copy(y_hbm.at[pl.ds(dref[0, 0, k * tm + r], 1)],
                                      ybuf.at[s, pl.ds(k * tm + r, 1)], sem.at[s]).start()
            return 0
        lax.fori_loop(0, tm, issue, 0, unroll=True)

    @pl.when(i == 0)
    def _():
        gather(dst_ref, 0)

    @pl.when(i + 1 < n)
    def _():
        gather(dstn_ref, 1 - slot)

    pltpu.make_async_copy(y_hbm.at[pl.ds(0, rows)], ybuf.at[slot], sem.at[slot]).wait()
    gt = gt_ref[...]
    h = h_ref[...] + (gt[:, 0:1] * ybuf[slot, :tm] + gt[:, 1:2] * ybuf[slot, tm:])
    xn = (_rms(h) * g_ref[...]).astype(BF16)
    gate = jax.nn.sigmoid(_dot(xn, wg_ref[...]))
    out = h + _dot(p_ref[...].astype(BF16), wp_ref[...]) * gate
    if final:
        out = _rms(out) * gf_ref[...]
    o_ref[...] = out


def _combine_ple(h, y, dest_tiles, gates, p, layer, g, w_proj, w_gate, g_final, *, final, tm):
    T, D = h.shape
    Pd = p.shape[2]
    n_tiles = T // tm
    row = lambda i: (i, 0)
    fixed = lambda i: (0, 0)
    smem = functools.partial(pl.BlockSpec, (1, 1, MOE_TOP_K * tm), memory_space=pltpu.SMEM)
    return pl.pallas_call(
        functools.partial(_combine_ple_kernel, final=final),
        grid=(n_tiles,),
        in_specs=[
            smem(lambda i: (i, 0, 0)),
            smem(lambda i: (jnp.minimum(i + 1, n_tiles - 1), 0, 0)),
            pl.BlockSpec((tm, D), row),
            pl.BlockSpec((tm, MOE_TOP_K), row),
            pl.BlockSpec((None, tm, Pd), lambda i: (layer, i, 0)),
            pl.BlockSpec((1, D), fixed),
            pl.BlockSpec((Pd, D), fixed),
            pl.BlockSpec((D, D), fixed),
            pl.BlockSpec((1, D), fixed),
            pl.BlockSpec(memory_space=pl.ANY),
        ],
        out_specs=pl.BlockSpec((tm, D), row),
        out_shape=jax.ShapeDtypeStruct((T, D), F32),
        scratch_shapes=[pltpu.VMEM((2, MOE_TOP_K * tm, D), F32), pltpu.SemaphoreType.DMA((2,))],
        compiler_params=_params(("arbitrary",)),
        name="moe_combine_ple",
    )(dest_tiles, dest_tiles, h, gates, p, g.reshape(1, D), w_proj, w_gate, g_final.reshape(1, D), y)


def _moe_ple(h, p, layer, g_ffn, w_coarse, b_coarse, w_fine, b_fine, w_gate_up, w_down, g_ple, w_proj, w_gate,
             g_final, *, final):
    tile = min(256, h.shape[0])
    info, counts = _router(h, g_ffn, w_coarse, b_coarse, w_fine, b_fine, tm=min(512, h.shape[0]))
    dest_tiles, gates, chunk_expert, n_used, n_chunks = _plan(info, counts, tile=tile)
    xs = _dispatch(h, g_ffn, dest_tiles, n_chunks * FFN_CHUNK, tm=tile)
    y = _ffn(xs, chunk_expert, n_used, w_gate_up, w_down, layer)
    return _combine_ple(h, y, dest_tiles, gates, p, layer, g_ple, w_proj.astype(BF16), w_gate.astype(BF16),
                        g_final, final=final, tm=tile)


def _mixer_a(h, g, w_in, w_out, *, batch, seq):
    T, D = h.shape
    width = A_HEADS * HEAD_DIM
    w = w_in.astype(BF16)
    xn = _norm(h, g, tm=1024)
    outs, lses = [], []
    for grp, (window, r) in enumerate(A_PATTERNS):
        assert window // r == ATTN_BLOCK and seq % (r * ATTN_BLOCK) == 0
        L = seq // r
        to_streams = lambda a: a.reshape(batch, L, r, -1).transpose(0, 2, 1, 3).reshape(T, -1)
        from_streams = lambda a: a.transpose(0, 2, 1, 3).reshape(T, -1)
        xs = xn if r == 1 else to_streams(xn)
        qkv = _mm(xs, w, col0=grp * 3 * width, n_out=3 * width, tm=1024, tn=1024)
        o, lse = _attn_a_group(qkv, batch=batch, streams=r, length=L, dilation=r, heads=A_HEADS)
        outs.append(from_streams(o))
        lses.append(from_streams(lse))
    return _merge_proj(outs, lses, w_out.astype(BF16), h, heads=A_HEADS, tm=512)


def _mixer_b(h, g, w_in, w_out, lq1, lk1, lq2, lk2, subln, lambda_init, *, batch, seq):
    D = h.shape[1]
    heads = D // (2 * HEAD_DIM)
    qk, vt = _proj_qk_vt(h, g, w_in, batch=batch, seq=seq, tm=512, q_scale=LOG2_E / math.sqrt(HEAD_DIM))
    o = _attn_b(qk, vt, lq1, lk1, lq2, lk2, subln, batch=batch, seq=seq, heads=heads,
                lambda_init=lambda_init, t=256, tq=512)
    return _mm_res(o.reshape(batch * seq, -1), w_out.astype(BF16), h, tm=512)


def _mixer_c(h, g, w_in, w_out, *, batch, seq):
    D = h.shape[1]
    heads = D // HEAD_DIM
    qk, vt = _proj_qk_vt(h, g, w_in, batch=batch, seq=seq, tm=512, q_scale=1.0 / math.sqrt(HEAD_DIM))
    o = _attn_c(qk, vt, batch=batch, seq=seq, heads=heads, tq=512)
    return _mm_res(o.reshape(batch * seq, -1), w_out.astype(BF16), h, tm=512)


def kernel(x, p, norm_mix, norm_ffn, norm_ple, norm_final, a_w_in, a_w_out, b_w_in, b_w_out,
           b_lambda_q1, b_lambda_k1, b_lambda_q2, b_lambda_k2, b_subln, c_w_in, c_w_out,
           moe_w_coarse, moe_b_coarse, moe_w_fine, moe_b_fine, moe_w_gate_up, moe_w_down,
           ple_w_proj, ple_w_gate):
    batch, seq, D = x.shape
    depth = p.shape[0]
    T = batch * seq
    h = x.reshape(T, D)
    p_rows = p.reshape(depth, T, -1)
    for i in range(depth):
        kind = i % N_MIXERS
        j = i // N_MIXERS
        if kind == 0:
            h = _mixer_a(h, norm_mix[i], a_w_in[j], a_w_out[j], batch=batch, seq=seq)
        elif kind == 1:
            lambda_init = 0.8 - 0.6 * math.exp(-0.3 * i)
            h = _mixer_b(h, norm_mix[i], b_w_in[j], b_w_out[j], b_lambda_q1[j], b_lambda_k1[j],
                         b_lambda_q2[j], b_lambda_k2[j], b_subln[j], lambda_init, batch=batch, seq=seq)
        else:
            h = _mixer_c(h, norm_mix[i], c_w_in[j], c_w_out[j], batch=batch, seq=seq)
        h = _moe_ple(h, p_rows, i, norm_ffn[i], moe_w_coarse[i], moe_b_coarse[i], moe_w_fine[i],
                     moe_b_fine[i], moe_w_gate_up, moe_w_down, norm_ple[i], ple_w_proj[i],
                     ple_w_gate[i], norm_final, final=(i == depth - 1))
    return h.reshape(batch, seq, D)
```

```python
import functools
import math

import jax
import jax.numpy as jnp
import numpy as np
from jax import lax
from jax.experimental import pallas as pl
from jax.experimental.pallas import tpu as pltpu

F32 = jnp.float32
BF16 = jnp.bfloat16
U32 = jnp.uint32
I32 = jnp.int32

HEAD_DIM = 64
N_MIXERS = 3
ATTN_BLOCK = 128
RMS_EPS = 1e-6
A_HEADS = 16
A_PATTERNS = ((128, 1), (512, 4), (2048, 16))
MOE_GROUPS = 8
MOE_EXPERTS_PER_GROUP = 8
MOE_TOP_K = 2
FFN_CHUNK = 256
LANES = 128
SUBLANES = 8
F32_EXP2_ZERO = -151.0
LOG2_E = math.log2(math.e)
VMEM_LIMIT = 48 * 1024 * 1024


def _params(sem, vmem=VMEM_LIMIT):
    return pltpu.CompilerParams(dimension_semantics=sem, vmem_limit_bytes=vmem)


def _rms(x):
    return x * lax.rsqrt(jnp.mean(x * x, axis=-1, keepdims=True) + RMS_EPS)


def _dot_t(a, b):
    return lax.dot_general(a, b, (((1,), (1,)), ((), ())), preferred_element_type=F32)


def _dot(a, b):
    return jnp.dot(a, b, preferred_element_type=F32)


def _split_bf16(x):
    hi = x.astype(BF16)
    lo = (x - hi.astype(F32)).astype(BF16)
    return hi, lo


def _bits(x):
    return lax.bitcast_convert_type(x, U32)


def _pack_bf16_pair(a, b):
    a = a.astype(BF16).astype(F32)
    b = b.astype(BF16).astype(F32)
    return (_bits(a) >> 16) | _bits(b)


def _unpack_bf16_pair(w):
    a = lax.bitcast_convert_type(w << 16, F32).astype(BF16)
    b = lax.bitcast_convert_type(w & jnp.uint32(0xFFFF0000), F32).astype(BF16)
    return a, b


def _norm_kernel(x_ref, g_ref, o_ref):
    o_ref[...] = (_rms(x_ref[...]) * g_ref[...]).astype(o_ref.dtype)


def _norm(x, g, *, tm):
    T, D = x.shape
    tm = min(tm, T)
    return pl.pallas_call(
        _norm_kernel,
        grid=(T // tm,),
        in_specs=[pl.BlockSpec((tm, D), lambda i: (i, 0)), pl.BlockSpec((1, D), lambda i: (0, 0))],
        out_specs=pl.BlockSpec((tm, D), lambda i: (i, 0)),
        out_shape=jax.ShapeDtypeStruct((T, D), BF16),
        compiler_params=_params(("parallel",)),
        name="rmsnorm",
    )(x, g.reshape(1, D))


def _mm_kernel(a_ref, w_ref, o_ref):
    o_ref[...] = _dot(a_ref[...], w_ref[...]).astype(o_ref.dtype)


def _mm(a, w, *, col0, n_out, tm, tn):
    T, K = a.shape
    N = n_out
    tm = min(tm, T)
    tn = min(tn, N)
    first = col0 // tn
    return pl.pallas_call(
        _mm_kernel,
        grid=(T // tm, N // tn),
        in_specs=[pl.BlockSpec((tm, K), lambda i, j: (i, 0)), pl.BlockSpec((K, tn), lambda i, j: (0, first + j))],
        out_specs=pl.BlockSpec((tm, tn), lambda i, j: (i, j)),
        out_shape=jax.ShapeDtypeStruct((T, N), BF16),
        compiler_params=_params(("parallel", "arbitrary")),
        name="proj",
    )(a, w)


def _norm_mm_kernel(x_ref, g_ref, w_ref, o_ref, xn_ref):
    @pl.when(pl.program_id(1) == 0)
    def _():
        xn_ref[...] = (_rms(x_ref[...]) * g_ref[...]).astype(BF16)

    o_ref[...] = _dot(xn_ref[...], w_ref[...]).astype(o_ref.dtype)


def _norm_mm(x, g, w, *, tm, tn, out_dtype=BF16):
    T, D = x.shape
    N = w.shape[1]
    tm = min(tm, T)
    tn = min(tn, N)
    return pl.pallas_call(
        _norm_mm_kernel,
        grid=(T // tm, N // tn),
        in_specs=[
            pl.BlockSpec((tm, D), lambda i, j: (i, 0)),
            pl.BlockSpec((1, D), lambda i, j: (0, 0)),
            pl.BlockSpec((D, tn), lambda i, j: (0, j)),
        ],
        out_specs=pl.BlockSpec((tm, tn), lambda i, j: (i, j)),
        out_shape=jax.ShapeDtypeStruct((T, N), out_dtype),
        scratch_shapes=[pltpu.VMEM((tm, D), BF16)],
        compiler_params=_params(("parallel", "arbitrary")),
        name="norm_proj",
    )(x, g.reshape(1, D), w)


def _mm_res_kernel(a_ref, w_ref, r_ref, o_ref):
    o_ref[...] = r_ref[...] + _dot(a_ref[...], w_ref[...])


def _mm_res(a, w, res, *, tm):
    T, K = a.shape
    N = w.shape[1]
    tm = min(tm, T)
    return pl.pallas_call(
        _mm_res_kernel,
        grid=(T // tm,),
        in_specs=[
            pl.BlockSpec((tm, K), lambda i: (i, 0)),
            pl.BlockSpec((K, N), lambda i: (0, 0)),
            pl.BlockSpec((tm, N), lambda i: (i, 0)),
        ],
        out_specs=pl.BlockSpec((tm, N), lambda i: (i, 0)),
        out_shape=jax.ShapeDtypeStruct((T, N), F32),
        compiler_params=_params(("parallel",)),
        name="out_proj",
    )(a, w, res)


def _attn_a_kernel(bias_ref, q_ref, kp_ref, kc_ref, vp_ref, vc_ref, o_ref, lse_ref, *, heads):
    n = pl.program_id(2)
    variant = jnp.minimum(n, 1)
    blk = ATTN_BLOCK
    lane = lax.broadcasted_iota(I32, (1, LANES), 1)
    first_head = lane < HEAD_DIM
    lse_lane = lax.broadcasted_iota(I32, (blk, LANES), 1)
    lse_all = jnp.zeros((blk, LANES), F32)
    group = 4
    for h0 in range(0, heads, group):
        vals, scores = [], []
        for h in range(h0, h0 + group):
            sl = slice(LANES * (h // 2), LANES * (h // 2 + 1))
            q = q_ref[:, sl]
            mine = first_head if h % 2 == 0 else jnp.logical_not(first_head)
            keys = jnp.concatenate([kp_ref[:, sl], kc_ref[:, sl]], axis=0)
            vals.append(jnp.concatenate([vp_ref[:, sl], vc_ref[:, sl]], axis=0))
            qm = jnp.where(mine, q * (1.0 / math.sqrt(HEAD_DIM)), jnp.zeros_like(q))
            scores.append(_dot_t(qm, keys))
        probs, scales = [], []
        for h, s in zip(range(h0, h0 + group), scores):
            s = s + bias_ref[variant, h]
            m = jnp.max(s, axis=-1, keepdims=True)
            e = jnp.exp(s - m)
            den = jnp.sum(e, axis=-1, keepdims=True)
            probs.append(e.astype(BF16))
            scales.append(1.0 / den)
            lse_all = jnp.where(lse_lane == h, m + jnp.log(den), lse_all)
        outs = [_dot(p, v) * sc for p, v, sc in zip(probs, vals, scales)]
        for idx in range(0, group, 2):
            sl = slice(LANES * ((h0 + idx) // 2), LANES * ((h0 + idx) // 2 + 1))
            o_ref[:, sl] = jnp.where(first_head, outs[idx], outs[idx + 1]).astype(o_ref.dtype)
    lse_ref[...] = lse_all


def _alibi_window_bias(dilation, heads):
    blk = ATTN_BLOCK
    a = np.arange(blk)[:, None]
    b = np.arange(2 * blk)[None, :]
    delta = blk + a - b
    valid = (delta >= 0) & (delta <= blk)
    slopes = 2.0 ** (-8.0 * np.arange(1, heads + 1) / heads)
    bias = -slopes[:, None, None] * (delta * dilation).astype(np.float64)[None]
    bias = np.where(valid[None], bias, -np.inf)
    first = bias.copy()
    first[:, :, :blk] = -np.inf
    return jnp.asarray(np.stack([first, bias]), F32)


def _attn_a_group(qkv, *, batch, streams, length, dilation, heads):
    width = heads * HEAD_DIM
    nb = length // ATTN_BLOCK
    x = qkv.reshape(batch, streams, length, 3 * width)
    blk = (None, None, ATTN_BLOCK, width)
    cur = lambda which: (lambda b, c, n: (b, c, n, which))
    prev = lambda which: (lambda b, c, n: (b, c, jnp.maximum(n - 1, 0), which))
    bias = _alibi_window_bias(dilation, heads)
    o, lse = pl.pallas_call(
        functools.partial(_attn_a_kernel, heads=heads),
        grid=(batch, streams, nb),
        in_specs=[
            pl.BlockSpec(bias.shape, lambda b, c, n: (0, 0, 0, 0)),
            pl.BlockSpec(blk, cur(0)),
            pl.BlockSpec(blk, prev(1)),
            pl.BlockSpec(blk, cur(1)),
            pl.BlockSpec(blk, prev(2)),
            pl.BlockSpec(blk, cur(2)),
        ],
        out_specs=[
            pl.BlockSpec(blk, lambda b, c, n: (b, c, n, 0)),
            pl.BlockSpec((None, None, ATTN_BLOCK, LANES), lambda b, c, n: (b, c, n, 0)),
        ],
        out_shape=[
            jax.ShapeDtypeStruct((batch, streams, length, width), BF16),
            jax.ShapeDtypeStruct((batch, streams, length, LANES), F32),
        ],
        compiler_params=_params(("parallel", "parallel", "arbitrary")),
        name=f"dilated_attn_r{dilation}",
    )(bias, x, x, x, x, x)
    return o, lse


def _merge_proj_kernel(o0_ref, o1_ref, o2_ref, l0_ref, l1_ref, l2_ref, e_ref, w_ref, r_ref, out_ref):
    l0, l1, l2 = l0_ref[...], l1_ref[...], l2_ref[...]
    mx = jnp.maximum(jnp.maximum(l0, l1), l2)
    e0, e1, e2 = jnp.exp(l0 - mx), jnp.exp(l1 - mx), jnp.exp(l2 - mx)
    inv = 1.0 / (e0 + e1 + e2)
    expand = e_ref[...]
    acc = None
    for e, o_ref in ((e0, o0_ref), (e1, o1_ref), (e2, o2_ref)):
        hi, lo = _split_bf16(e * inv)
        w_full = _dot(hi, expand) + _dot(lo, expand)
        term = w_full * o_ref[...].astype(F32)
        acc = term if acc is None else acc + term
    out_ref[...] = r_ref[...] + _dot(acc.astype(BF16), w_ref[...])


def _merge_proj(outs, lses, w_out, res, *, heads, tm):
    T, width = outs[0].shape
    N = w_out.shape[1]
    tm = min(tm, T)
    expand = np.zeros((LANES, width), np.float32)
    for h in range(heads):
        expand[h, h * HEAD_DIM:(h + 1) * HEAD_DIM] = 1.0
    expand = jnp.asarray(expand, BF16)
    row = lambda i: (i, 0)
    fixed = lambda i: (0, 0)
    return pl.pallas_call(
        _merge_proj_kernel,
        grid=(T // tm,),
        in_specs=[pl.BlockSpec((tm, width), row)] * 3 + [pl.BlockSpec((tm, LANES), row)] * 3 + [
            pl.BlockSpec((LANES, width), fixed),
            pl.BlockSpec((width, N), fixed),
            pl.BlockSpec((tm, N), row),
        ],
        out_specs=pl.BlockSpec((tm, N), row),
        out_shape=jax.ShapeDtypeStruct((T, N), F32),
        compiler_params=_params(("parallel",)),
        name="merge_out_proj",
    )(*outs, *lses, expand, w_out, res)


def _proj_qk_vt_kernel(x_ref, g_ref, wqk_ref, wvt_ref, qk_ref, vt_ref, *, q_width, q_scale):
    xn = (_rms(x_ref[...]) * g_ref[...]).astype(BF16)
    qk = _dot(xn, wqk_ref[...])
    qk_ref[:, :q_width] = (qk[:, :q_width] * q_scale).astype(BF16)
    qk_ref[:, q_width:] = qk[:, q_width:].astype(BF16)
    vt_ref[...] = _dot_t(wvt_ref[...], xn).astype(BF16)


def _proj_qk_vt(h, g, w_in, *, batch, seq, tm, q_scale):
    T, D = h.shape
    width = w_in.shape[1] // 3
    tm = min(tm, seq)
    per_seq = seq // tm
    w = w_in.astype(BF16)
    wqk = w[:, :2 * width]
    wvt = w[:, 2 * width:].T
    fixed = lambda i: (0, 0)
    qk, vt = pl.pallas_call(
        functools.partial(_proj_qk_vt_kernel, q_width=width, q_scale=q_scale),
        grid=(T // tm,),
        in_specs=[
            pl.BlockSpec((tm, D), lambda i: (i, 0)),
            pl.BlockSpec((1, D), fixed),
            pl.BlockSpec((D, 2 * width), fixed),
            pl.BlockSpec((width, D), fixed),
        ],
        out_specs=[
            pl.BlockSpec((tm, 2 * width), lambda i: (i, 0)),
            pl.BlockSpec((None, width, tm), lambda i: (i // per_seq, 0, i % per_seq)),
        ],
        out_shape=[
            jax.ShapeDtypeStruct((T, 2 * width), BF16),
            jax.ShapeDtypeStruct((batch, width, seq), BF16),
        ],
        compiler_params=_params(("parallel",)),
        name="norm_proj_qk_vt",
    )(h, g.reshape(1, D), wqk, wvt)
    return qk.reshape(batch, seq, 2 * width), vt


def _head_halves(q):
    lane = lax.broadcasted_iota(I32, (1, LANES), 1)
    first = lane < HEAD_DIM
    zero = jnp.zeros_like(q)
    return jnp.where(first, q, zero), jnp.where(first, zero, q)


def _attn_b_kernel(slope_ref, q_ref, k_ref, vt_ref, lq1_ref, lk1_ref, lq2_ref, lk2_ref, sub_ref,
                   o_ref, *, t, lambda_init):
    h = pl.program_id(1)
    i = pl.program_id(2)
    slope = slope_ref[h]
    n_col = q_ref.shape[0] // t
    qs = [_head_halves(q_ref[col * t:(col + 1) * t, :]) for col in range(n_col)]
    rel = lax.broadcasted_iota(I32, (t, t), 1) - lax.broadcasted_iota(I32, (t, t), 0)
    rel_bias = slope * rel.astype(F32)

    def step(js, carry, block_modes):
        new = list(carry)
        plans = [[(col, c) for col, mode in enumerate(modes) if mode is not None for c in range(2)]
                 for modes in block_modes]
        starts = [pl.multiple_of(j * t, t) for j in js]
        scores = [[_dot_t(k_ref[pl.ds(st, t), :], qs[col][c]) for col, c in chains]
                  for st, chains in zip(starts, plans)]
        for j, st, chains, modes, block_scores in zip(js, starts, plans, block_modes, scores):
            vt = vt_ref[:, pl.ds(st, t)]
            probs, alphas = [], []
            for (col, c), s in zip(chains, block_scores):
                base = 3 * (2 * col + c)
                m, l = new[base], new[base + 1]
                shift = slope * ((i * n_col + col - j) * t).astype(F32)
                s = s - rel_bias
                if modes[col]:
                    s = jnp.where(rel >= 0, s, -jnp.inf)
                m_new = jnp.maximum(m, jnp.max(s, axis=0, keepdims=True) - shift)
                alpha = jnp.exp2(m - m_new)
                p = jnp.exp2(s - (m_new + shift))
                new[base] = m_new
                new[base + 1] = alpha * l + jnp.sum(p, axis=0, keepdims=True)
                probs.append(p.astype(BF16))
                alphas.append(alpha)
            for (col, c), p, alpha in zip(chains, probs, alphas):
                base = 3 * (2 * col + c)
                new[base + 2] = alpha * new[base + 2] + _dot(vt, p)
        return tuple(new)

    init = (jnp.full((1, t), -jnp.inf, F32), jnp.zeros((1, t), F32), jnp.zeros((LANES, t), F32)) * (2 * n_col)
    full = [(False,) * n_col] * n_col
    carry = lax.fori_loop(0, i, lambda jj, c: step([n_col * jj + b for b in range(n_col)], c, full), init)
    diag = [tuple(None if col < d else (col == d) for col in range(n_col)) for d in range(n_col)]
    carry = step([n_col * i + d for d in range(n_col)], carry, diag)
    lam = (jnp.exp(jnp.sum(lq1_ref[...] * lk1_ref[...], axis=-1, keepdims=True))
           - jnp.exp(jnp.sum(lq2_ref[...] * lk2_ref[...], axis=-1, keepdims=True)) + lambda_init)
    for col in range(n_col):
        _, l1, a1, _, l2, a2 = carry[6 * col:6 * col + 6]
        o = a1 * (1.0 / l1) - lam * (a2 * (1.0 / l2))
        o = o * lax.rsqrt(jnp.mean(o * o, axis=0, keepdims=True) + RMS_EPS)
        o = o * (sub_ref[...] * (1.0 - lambda_init))
        o_ref[col * t:(col + 1) * t, :] = o.T.astype(o_ref.dtype)


def _attn_b(qk, vt, lq1, lk1, lq2, lk2, subln, *, batch, seq, heads, lambda_init, t, tq):
    t = min(t, seq)
    tq = min(tq, seq)
    width = 2 * HEAD_DIM
    slopes = jnp.asarray(LOG2_E * 2.0 ** (-8.0 * np.arange(1, heads + 1) / heads), F32)
    vec = lambda a: a.reshape(1, -1).astype(F32)
    small = lambda n: pl.BlockSpec((1, n), lambda b, h, i, s: (0, 0))
    return pl.pallas_call(
        functools.partial(_attn_b_kernel, t=t, lambda_init=lambda_init),
        grid_spec=pltpu.PrefetchScalarGridSpec(
            num_scalar_prefetch=1,
            grid=(batch, heads, seq // tq),
            in_specs=[
                pl.BlockSpec((None, tq, width), lambda b, h, i, s: (b, i, h)),
                pl.BlockSpec((None, seq, width), lambda b, h, i, s: (b, 0, heads + h)),
                pl.BlockSpec((None, width, seq), lambda b, h, i, s: (b, h, 0)),
                small(HEAD_DIM), small(HEAD_DIM), small(HEAD_DIM), small(HEAD_DIM),
                pl.BlockSpec((width, 1), lambda b, h, i, s: (0, 0)),
            ],
            out_specs=pl.BlockSpec((None, tq, width), lambda b, h, i, s: (b, i, h)),
        ),
        out_shape=jax.ShapeDtypeStruct((batch, seq, heads * width), BF16),
        compiler_params=_params(("parallel", "parallel", "arbitrary")),
        name="diff_attn",
    )(slopes, qk, qk, vt, vec(lq1), vec(lk1), vec(lq2), vec(lk2), subln.reshape(width, 1).astype(F32))


def _attn_c_kernel(q_ref, k_ref, vt_ref, later_ref, o_ref):
    i = pl.program_id(2)
    tq = q_ref.shape[0]
    kw = LANES
    n_sub = tq // kw
    qs = _head_halves(q_ref[...])
    later = later_ref[...]
    rel = lax.broadcasted_iota(I32, (kw, tq), 1) - lax.broadcasted_iota(I32, (kw, tq), 0)

    def blocks(kbs, carry, diagonal):
        units = []
        for kb in kbs:
            start = pl.multiple_of(kb * kw, kw)
            k = k_ref[pl.ds(start, kw), :]
            vt = vt_ref[:, pl.ds(start, kw)]
            causal = (rel + (i * tq - kb * kw)) > 0 if diagonal else None
            for head in range(2):
                units.append(dict(head=head, vt=vt, causal=causal, z=_dot_t(k, qs[head])))
        for u in units:
            z = u["z"]
            softplus = jnp.maximum(z, 0.0) + jnp.log2(1.0 + jnp.exp2(-jnp.abs(z)))
            rest = jnp.where(u["causal"], softplus, 0.0) if diagonal else softplus
            u["logit"] = z - softplus
            u["rest"] = rest
            u["split"] = _split_bf16(rest)
        for u in units:
            hi, lo = u["split"]
            u["after"] = _dot(later, hi) + _dot(later, lo)
        c = [carry[0], carry[2]]
        weights = []
        for u in units:
            a = jnp.exp2(u["logit"] - u["after"] - c[u["head"]])
            if diagonal:
                a = jnp.where(u["causal"], a, 0.0)
            weights.append(a.astype(BF16))
            c[u["head"]] = c[u["head"]] + jnp.sum(u["rest"], axis=0, keepdims=True)
        acc = [carry[1], carry[3]]
        for u, a in zip(units, weights):
            acc[u["head"]] = acc[u["head"]] + _dot(u["vt"], a)
        return c[0], acc[0], c[1], acc[1]

    def alive(carry):
        return (jnp.min(jnp.minimum(carry[0], carry[2])) < -F32_EXP2_ZERO).astype(I32)

    zc = jnp.zeros((1, tq), F32)
    za = jnp.zeros((LANES, tq), F32)
    carry = blocks([i * n_sub + d for d in reversed(range(n_sub))], (zc, za, zc, za), True)

    def cond(state):
        return jnp.logical_and(state[0] >= 0, state[1] > 0)

    def body(state):
        new = blocks([state[0], state[0] - 1], state[2:], False)
        return (state[0] - 2, alive(new)) + new

    out = lax.while_loop(cond, body, (i * n_sub - 1, alive(carry)) + carry)
    row = lax.broadcasted_iota(I32, (LANES, 1), 0)
    o = jnp.where(row < HEAD_DIM, out[3], out[5])
    o_ref[...] = o.T.astype(o_ref.dtype)


def _attn_c(qk, vt, *, batch, seq, heads, tq):
    tq = min(tq, seq)
    assert seq % tq == 0 and (tq // LANES) % 2 == 0
    pairs = heads // 2
    later = jnp.asarray(np.triu(np.ones((LANES, LANES), np.float32), 1), BF16)
    return pl.pallas_call(
        _attn_c_kernel,
        grid=(batch, pairs, seq // tq),
        in_specs=[
            pl.BlockSpec((None, tq, LANES), lambda b, h, i: (b, i, h)),
            pl.BlockSpec((None, seq, LANES), lambda b, h, i: (b, 0, pairs + h)),
            pl.BlockSpec((None, LANES, seq), lambda b, h, i: (b, h, 0)),
            pl.BlockSpec((LANES, LANES), lambda b, h, i: (0, 0)),
        ],
        out_specs=pl.BlockSpec((None, tq, LANES), lambda b, h, i: (b, i, h)),
        out_shape=jax.ShapeDtypeStruct((batch, seq, heads * HEAD_DIM), BF16),
        compiler_params=_params(("parallel", "parallel", "arbitrary")),
        name="stick_breaking_attn",
    )(qk, qk, vt, later)


def _router_kernel(x_ref, g_ref, whi_ref, wlo_ref, b_ref, u_ref, info_ref, cnt_ref, run_ref,
                   *, groups, per_group):
    G, E = groups, per_group
    tm = x_ref.shape[0]
    ne = G * E

    @pl.when(pl.program_id(0) == 0)
    def _():
        run_ref[...] = jnp.zeros_like(run_ref)

    xn = _rms(x_ref[...]) * g_ref[...]
    hi, lo = _split_bf16(xn)
    whi = whi_ref[...]
    lt = _dot_t(whi, hi) + _dot_t(whi, lo) + _dot_t(wlo_ref[...], hi) + b_ref[...]
    sub = lax.broadcasted_iota(I32, (E, tm), 0)
    coarse = lt[0:G]
    cmax = jnp.max(coarse, axis=0, keepdims=True)
    g_prob = 1.0 / jnp.sum(jnp.exp(coarse - cmax), axis=0, keepdims=True)
    g_idx = jnp.min(jnp.where(coarse == cmax, sub, G), axis=0, keepdims=True)
    fine = jnp.zeros((E, tm), F32)
    for grp in range(G):
        fine = jnp.where(g_idx == grp, lt[G + grp * E:G + (grp + 1) * E], fine)
    fmax = jnp.max(fine, axis=0, keepdims=True)
    fsum = jnp.sum(jnp.exp(fine - fmax), axis=0, keepdims=True)
    i1 = jnp.min(jnp.where(fine == fmax, sub, E), axis=0, keepdims=True)
    rest = jnp.where(sub == i1, -jnp.inf, fine)
    m2 = jnp.max(rest, axis=0, keepdims=True)
    i2 = jnp.min(jnp.where(rest == m2, sub, E), axis=0, keepdims=True)
    p1 = 1.0 / fsum
    p2 = jnp.exp(m2 - fmax) / fsum
    norm = p1 + p2
    gate1 = g_prob * (p1 / norm)
    gate2 = g_prob * (p2 / norm)
    e1 = g_idx * E + i1
    e2 = g_idx * E + i2
    ex = lax.broadcasted_iota(I32, (ne, tm), 0)
    oh1 = (ex == e1).astype(F32)
    oh2 = (ex == e2).astype(F32)
    both = oh1 + oh2
    earlier = _dot(both.astype(BF16), u_ref[...]) + run_ref[:, 0:1]
    r1 = jnp.sum(oh1 * earlier, axis=0, keepdims=True)
    r2 = jnp.sum(oh2 * earlier, axis=0, keepdims=True)
    total = run_ref[...] + jnp.sum(both, axis=1, keepdims=True)
    run_ref[...] = total
    cnt_ref[...] = total
    row = lax.broadcasted_iota(I32, (SUBLANES, tm), 0)
    fields = (e1.astype(F32), e2.astype(F32), r1, r2, gate1, gate2)
    info = jnp.zeros((SUBLANES, tm), F32)
    for idx, f in enumerate(fields):
        info = jnp.where(row == idx, f, info)
    info_ref[...] = info


def _router(h, g, w_coarse, b_coarse, w_fine, b_fine, *, tm):
    T, D = h.shape
    tm = min(tm, T)
    G, ne = w_coarse.shape[1], w_fine.shape[1]
    pad = LANES - G - ne
    wt = jnp.pad(jnp.concatenate([w_coarse, w_fine], axis=1), ((0, 0), (0, pad))).T
    b = jnp.pad(jnp.concatenate([b_coarse, b_fine]), (0, pad)).reshape(LANES, 1)
    whi, wlo = _split_bf16(wt)
    before = jnp.asarray(np.triu(np.ones((tm, tm), np.float32), 1), BF16)
    fixed = lambda i: (0, 0)
    return pl.pallas_call(
        functools.partial(_router_kernel, groups=G, per_group=ne // G),
        grid=(T // tm,),
        in_specs=[
            pl.BlockSpec((tm, D), lambda i: (i, 0)),
            pl.BlockSpec((1, D), fixed),
            pl.BlockSpec((LANES, D), fixed),
            pl.BlockSpec((LANES, D), fixed),
            pl.BlockSpec((LANES, 1), fixed),
            pl.BlockSpec((tm, tm), fixed),
        ],
        out_specs=[pl.BlockSpec((SUBLANES, tm), lambda i: (0, i)), pl.BlockSpec((ne, LANES), fixed)],
        out_shape=[jax.ShapeDtypeStruct((SUBLANES, T), F32), jax.ShapeDtypeStruct((ne, LANES), F32)],
        scratch_shapes=[pltpu.VMEM((ne, LANES), F32)],
        compiler_params=_params(("arbitrary",)),
        name="moe_router",
    )(h, g.reshape(1, D), whi, wlo, b, before)


def _plan(info, counts, *, tile):
    T = info.shape[1]
    ne = counts.shape[0]
    C = FFN_CHUNK
    experts = info[0:2].astype(I32)
    ranks = info[2:4].astype(I32)
    gates = info[4:6]
    cnt = counts[:, 0].astype(I32)
    padded = (cnt + C - 1) // C * C
    pad_end = jnp.cumsum(padded)
    pad_start = pad_end - padded
    ids = jnp.arange(ne, dtype=I32)
    start_of = jnp.sum(jnp.where(experts[:, :, None] == ids, pad_start, 0), axis=-1)
    dest = ranks + start_of
    n_chunks = (MOE_TOP_K * T + ne * (C - 1) + C - 1) // C
    n_used = (pad_end[-1] // C).astype(I32)
    first_row = jnp.minimum(jnp.arange(n_chunks, dtype=I32), n_used - 1) * C
    chunk_expert = jnp.minimum(jnp.sum(pad_end[None, :] <= first_row[:, None], axis=1), ne - 1).astype(I32)
    dest_tiles = dest.reshape(MOE_TOP_K, T // tile, tile).transpose(1, 0, 2).reshape(T // tile, 1, MOE_TOP_K * tile)
    return dest_tiles, gates.T, chunk_expert, n_used.reshape(1), n_chunks


def _dispatch_kernel(dst_ref, h_ref, g_ref, xs_in, xs_out, xbuf, sem):
    del xs_in
    i = pl.program_id(0)
    n = pl.num_programs(0)
    tm, D = h_ref.shape
    slot = i % 2

    def wait(s):
        for _ in range(MOE_TOP_K):
            pltpu.make_async_copy(xbuf.at[s], xs_out.at[pl.ds(0, tm)], sem.at[s]).wait()

    @pl.when(i >= 2)
    def _():
        wait(slot)

    xn = _rms(h_ref[...]) * g_ref[...]
    xbuf[slot] = _pack_bf16_pair(xn[:, :D // 2], xn[:, D // 2:])

    def issue(r, _):
        for k in range(MOE_TOP_K):
            pltpu.make_async_copy(xbuf.at[slot, pl.ds(r, 1)], xs_out.at[pl.ds(dst_ref[0, 0, k * tm + r], 1)],
                                  sem.at[slot]).start()
        return 0

    lax.fori_loop(0, tm, issue, 0, unroll=True)

    @pl.when(i == n - 1)
    def _():
        wait(slot)

        @pl.when(i >= 1)
        def _():
            wait(1 - slot)


def _dispatch(h, g, dest_tiles, n_rows, *, tm):
    T, D = h.shape
    xs0 = jnp.zeros((n_rows, D // 2), U32)
    return pl.pallas_call(
        _dispatch_kernel,
        grid=(T // tm,),
        in_specs=[
            pl.BlockSpec((1, 1, MOE_TOP_K * tm), lambda i: (i, 0, 0), memory_space=pltpu.SMEM),
            pl.BlockSpec((tm, D), lambda i: (i, 0)),
            pl.BlockSpec((1, D), lambda i: (0, 0)),
            pl.BlockSpec(memory_space=pl.ANY),
        ],
        out_specs=pl.BlockSpec(memory_space=pl.ANY),
        out_shape=jax.ShapeDtypeStruct((n_rows, D // 2), U32),
        scratch_shapes=[pltpu.VMEM((2, tm, D // 2), U32), pltpu.SemaphoreType.DMA((2,))],
        input_output_aliases={3: 0},
        compiler_params=_params(("arbitrary",)),
        name="moe_dispatch",
    )(dest_tiles, h, g.reshape(1, D), xs0)


def _ffn_kernel(ce_ref, nu_ref, x_ref, wgu_ref, wdn_ref, y_ref, wgu_bf, wdn_bf, *, d_expert):
    c = pl.program_id(0)
    n_used = nu_ref[0]

    @pl.when(c < n_used)
    def _():
        @pl.when(jnp.logical_or(c == 0, ce_ref[c] != ce_ref[jnp.maximum(c - 1, 0)]))
        def _():
            wgu_bf[...] = wgu_ref[...].astype(BF16)
            wdn_bf[...] = wdn_ref[...].astype(BF16)

        half = x_ref.shape[1]
        xa, xb = _unpack_bf16_pair(x_ref[...])
        n_blk = 2
        w = d_expert // n_blk
        proj = lambda lo: _dot(xa, wgu_bf[:half, lo:lo + w]) + _dot(xb, wgu_bf[half:, lo:lo + w])
        gates = [proj(b * w) for b in range(n_blk)]
        ups = [proj(d_expert + b * w) for b in range(n_blk)]
        acts = [((g * jax.nn.sigmoid(g)) * u).astype(BF16) for g, u in zip(gates, ups)]
        y = _dot(acts[0], wdn_bf[:w])
        for b in range(1, n_blk):
            y = y + _dot(acts[b], wdn_bf[b * w:(b + 1) * w])
        y_ref[...] = y

    @pl.when(c >= n_used)
    def _():
        y_ref[...] = jnp.zeros_like(y_ref)


def _ffn(xs, chunk_expert, n_used, w_gate_up, w_down, layer):
    n_rows, half = xs.shape
    C = FFN_CHUNK
    n_chunks = n_rows // C
    D, F2 = w_gate_up.shape[2:]
    d_expert = w_down.shape[2]
    return pl.pallas_call(
        functools.partial(_ffn_kernel, d_expert=d_expert),
        grid_spec=pltpu.PrefetchScalarGridSpec(
            num_scalar_prefetch=2,
            grid=(n_chunks,),
            in_specs=[
                pl.BlockSpec((C, half), lambda c, ce, nu: (jnp.minimum(c, nu[0] - 1), 0)),
                pl.BlockSpec((None, None, D, F2), lambda c, ce, nu: (layer, ce[c], 0, 0)),
                pl.BlockSpec((None, None, d_expert, D), lambda c, ce, nu: (layer, ce[c], 0, 0)),
            ],
            out_specs=pl.BlockSpec((C, D), lambda c, ce, nu: (c, 0)),
            scratch_shapes=[pltpu.VMEM((D, F2), BF16), pltpu.VMEM((d_expert, D), BF16)],
        ),
        out_shape=jax.ShapeDtypeStruct((n_rows, D), F32),
        compiler_params=_params(("arbitrary",)),
        name="moe_expert_ffn",
    )(chunk_expert, n_used, xs, w_gate_up, w_down)


def _combine_ple_kernel(dst_ref, dstn_ref, h_ref, gt_ref, p_ref, g_ref, wp_ref, wg_ref, gf_ref, y_hbm,
                        o_ref, ybuf, sem, *, final):
    i = pl.program_id(0)
    n = pl.num_programs(0)
    tm, D = h_ref.shape
    rows = MOE_TOP_K * tm
    slot = i % 2

    def gather(dref, s):
        def issue(r, _):
            for k in range(MOE_TOP_K):
                pltpu.make_async_copy(y_hbm.at[pl.ds(dref[0, 0, k * tm + r], 1)],
                                      ybuf.at[s, pl.ds(k * tm + r, 1)], sem.at[s]).start()
            return 0
        lax.fori_loop(0, tm, issue, 0, unroll=True)

    @pl.when(i == 0)
    def _():
        gather(dst_ref, 0)

    def wait(s):
        pltpu.make_async_copy(y_hbm.at[pl.ds(0, rows)], ybuf.at[s], sem.at[s]).wait()

    wait(slot)
    gather(dstn_ref, 1 - slot)
    gt = gt_ref[...]
    h = h_ref[...] + (gt[:, 0:1] * ybuf[slot, :tm] + gt[:, 1:2] * ybuf[slot, tm:])
    xn = (_rms(h) * g_ref[...]).astype(BF16)
    gate = jax.nn.sigmoid(_dot(xn, wg_ref[...]))
    out = h + _dot(p_ref[...].astype(BF16), wp_ref[...]) * gate
    if final:
        out = _rms(out) * gf_ref[...]
    o_ref[...] = out

    @pl.when(i == n - 1)
    def _():
        wait(1 - slot)


def _combine_ple(h, y, dest_tiles, gates, p, layer, g, w_proj, w_gate, g_final, *, final, tm):
    T, D = h.shape
    Pd = p.shape[2]
    n_tiles = T // tm
    row = lambda i: (i, 0)
    fixed = lambda i: (0, 0)
    smem = functools.partial(pl.BlockSpec, (1, 1, MOE_TOP_K * tm), memory_space=pltpu.SMEM)
    return pl.pallas_call(
        functools.partial(_combine_ple_kernel, final=final),
        grid=(n_tiles,),
        in_specs=[
            smem(lambda i: (i, 0, 0)),
            smem(lambda i: (jnp.minimum(i + 1, n_tiles - 1), 0, 0)),
            pl.BlockSpec((tm, D), row),
            pl.BlockSpec((tm, MOE_TOP_K), row),
            pl.BlockSpec((None, tm, Pd), lambda i: (layer, i, 0)),
            pl.BlockSpec((1, D), fixed),
            pl.BlockSpec((Pd, D), fixed),
            pl.BlockSpec((D, D), fixed),
            pl.BlockSpec((1, D), fixed),
            pl.BlockSpec(memory_space=pl.ANY),
        ],
        out_specs=pl.BlockSpec((tm, D), row),
        out_shape=jax.ShapeDtypeStruct((T, D), F32),
        scratch_shapes=[pltpu.VMEM((2, MOE_TOP_K * tm, D), F32), pltpu.SemaphoreType.DMA((2,))],
        compiler_params=_params(("arbitrary",)),
        name="moe_combine_ple",
    )(dest_tiles, dest_tiles, h, gates, p, g.reshape(1, D), w_proj, w_gate, g_final.reshape(1, D), y)


def _moe_ple(h, p, layer, g_ffn, w_coarse, b_coarse, w_fine, b_fine, w_gate_up, w_down, g_ple, w_proj, w_gate,
             g_final, *, final):
    tile = min(512, h.shape[0])
    info, counts = _router(h, g_ffn, w_coarse, b_coarse, w_fine, b_fine, tm=min(512, h.shape[0]))
    dest_tiles, gates, chunk_expert, n_used, n_chunks = _plan(info, counts, tile=tile)
    xs = _dispatch(h, g_ffn, dest_tiles, n_chunks * FFN_CHUNK, tm=tile)
    y = _ffn(xs, chunk_expert, n_used, w_gate_up, w_down, layer)
    return _combine_ple(h, y, dest_tiles, gates, p, layer, g_ple, w_proj.astype(BF16), w_gate.astype(BF16),
                        g_final, final=final, tm=tile)


def _mixer_a(h, g, w_in, w_out, *, batch, seq):
    T, D = h.shape
    width = A_HEADS * HEAD_DIM
    w = w_in.astype(BF16)
    xn = _norm(h, g, tm=1024)
    outs, lses = [], []
    for grp, (window, r) in enumerate(A_PATTERNS):
        assert window // r == ATTN_BLOCK and seq % (r * ATTN_BLOCK) == 0
        L = seq // r
        to_streams = lambda a: a.reshape(batch, L, r, -1).transpose(0, 2, 1, 3).reshape(T, -1)
        from_streams = lambda a: a.transpose(0, 2, 1, 3).reshape(T, -1)
        xs = xn if r == 1 else to_streams(xn)
        qkv = _mm(xs, w, col0=grp * 3 * width, n_out=3 * width, tm=1024, tn=1024)
        o, lse = _attn_a_group(qkv, batch=batch, streams=r, length=L, dilation=r, heads=A_HEADS)
        outs.append(from_streams(o))
        lses.append(from_streams(lse))
    return _merge_proj(outs, lses, w_out.astype(BF16), h, heads=A_HEADS, tm=512)


def _mixer_b(h, g, w_in, w_out, lq1, lk1, lq2, lk2, subln, lambda_init, *, batch, seq):
    D = h.shape[1]
    heads = D // (2 * HEAD_DIM)
    qk, vt = _proj_qk_vt(h, g, w_in, batch=batch, seq=seq, tm=512, q_scale=LOG2_E / math.sqrt(HEAD_DIM))
    o = _attn_b(qk, vt, lq1, lk1, lq2, lk2, subln, batch=batch, seq=seq, heads=heads,
                lambda_init=lambda_init, t=256, tq=512)
    return _mm_res(o.reshape(batch * seq, -1), w_out.astype(BF16), h, tm=512)


def _mixer_c(h, g, w_in, w_out, *, batch, seq):
    D = h.shape[1]
    heads = D // HEAD_DIM
    qk, vt = _proj_qk_vt(h, g, w_in, batch=batch, seq=seq, tm=512, q_scale=LOG2_E / math.sqrt(HEAD_DIM))
    o = _attn_c(qk, vt, batch=batch, seq=seq, heads=heads, tq=512)
    return _mm_res(o.reshape(batch * seq, -1), w_out.astype(BF16), h, tm=512)


def kernel(x, p, norm_mix, norm_ffn, norm_ple, norm_final, a_w_in, a_w_out, b_w_in, b_w_out,
           b_lambda_q1, b_lambda_k1, b_lambda_q2, b_lambda_k2, b_subln, c_w_in, c_w_out,
           moe_w_coarse, moe_b_coarse, moe_w_fine, moe_b_fine, moe_w_gate_up, moe_w_down,
           ple_w_proj, ple_w_gate):
    batch, seq, D = x.shape
    depth = p.shape[0]
    T = batch * seq
    h = x.reshape(T, D)
    p_rows = p.reshape(depth, T, -1)
    for i in range(depth):
        kind = i % N_MIXERS
        j = i // N_MIXERS
        if kind == 0:
            h = _mixer_a(h, norm_mix[i], a_w_in[j], a_w_out[j], batch=batch, seq=seq)
        elif kind == 1:
            lambda_init = 0.8 - 0.6 * math.exp(-0.3 * i)
            h = _mixer_b(h, norm_mix[i], b_w_in[j], b_w_out[j], b_lambda_q1[j], b_lambda_k1[j],
                         b_lambda_q2[j], b_lambda_k2[j], b_subln[j], lambda_init, batch=batch, seq=seq)
        else:
            h = _mixer_c(h, norm_mix[i], c_w_in[j], c_w_out[j], batch=batch, seq=seq)
        h = _moe_ple(h, p_rows, i, norm_ffn[i], moe_w_coarse[i], moe_b_coarse[i], moe_w_fine[i],
                     moe_b_fine[i], moe_w_gate_up, moe_w_down, norm_ple[i], ple_w_proj[i],
                     ple_w_gate[i], norm_final, final=(i == depth - 1))
    return h.reshape(batch, seq, D)
```

```python
import functools
import math

import jax
import jax.numpy as jnp
import numpy as np
from jax import lax
from jax.experimental import pallas as pl
from jax.experimental.pallas import tpu as pltpu

F32 = jnp.float32
BF16 = jnp.bfloat16
U32 = jnp.uint32
I32 = jnp.int32

HEAD_DIM = 64
N_MIXERS = 3
ATTN_BLOCK = 128
RMS_EPS = 1e-6
A_HEADS = 16
A_PATTERNS = ((128, 1), (512, 4), (2048, 16))
MOE_GROUPS = 8
MOE_EXPERTS_PER_GROUP = 8
MOE_TOP_K = 2
FFN_CHUNK = 256
LANES = 128
SUBLANES = 8
F32_EXP2_ZERO = -151.0
LOG2_E = math.log2(math.e)
VMEM_LIMIT = 48 * 1024 * 1024


def _params(sem, vmem=VMEM_LIMIT):
    return pltpu.CompilerParams(dimension_semantics=sem, vmem_limit_bytes=vmem)


def _rms(x):
    return x * lax.rsqrt(jnp.mean(x * x, axis=-1, keepdims=True) + RMS_EPS)


def _dot_t(a, b):
    return lax.dot_general(a, b, (((1,), (1,)), ((), ())), preferred_element_type=F32)


def _dot(a, b):
    return jnp.dot(a, b, preferred_element_type=F32)


def _split_bf16(x):
    hi = x.astype(BF16)
    lo = (x - hi.astype(F32)).astype(BF16)
    return hi, lo


def _bits(x):
    return lax.bitcast_convert_type(x, U32)


def _pack_bf16_pair(a, b):
    a = a.astype(BF16).astype(F32)
    b = b.astype(BF16).astype(F32)
    return (_bits(a) >> 16) | _bits(b)


def _unpack_bf16_pair(w):
    a = lax.bitcast_convert_type(w << 16, F32).astype(BF16)
    b = lax.bitcast_convert_type(w & jnp.uint32(0xFFFF0000), F32).astype(BF16)
    return a, b


def _norm_kernel(x_ref, g_ref, o_ref):
    o_ref[...] = (_rms(x_ref[...]) * g_ref[...]).astype(o_ref.dtype)


def _norm(x, g, *, tm):
    T, D = x.shape
    tm = min(tm, T)
    return pl.pallas_call(
        _norm_kernel,
        grid=(T // tm,),
        in_specs=[pl.BlockSpec((tm, D), lambda i: (i, 0)), pl.BlockSpec((1, D), lambda i: (0, 0))],
        out_specs=pl.BlockSpec((tm, D), lambda i: (i, 0)),
        out_shape=jax.ShapeDtypeStruct((T, D), BF16),
        compiler_params=_params(("parallel",)),
        name="rmsnorm",
    )(x, g.reshape(1, D))


def _mm_kernel(a_ref, w_ref, o_ref):
    o_ref[...] = _dot(a_ref[...], w_ref[...]).astype(o_ref.dtype)


def _mm(a, w, *, col0, n_out, tm, tn):
    T, K = a.shape
    N = n_out
    tm = min(tm, T)
    tn = min(tn, N)
    first = col0 // tn
    return pl.pallas_call(
        _mm_kernel,
        grid=(T // tm, N // tn),
        in_specs=[pl.BlockSpec((tm, K), lambda i, j: (i, 0)), pl.BlockSpec((K, tn), lambda i, j: (0, first + j))],
        out_specs=pl.BlockSpec((tm, tn), lambda i, j: (i, j)),
        out_shape=jax.ShapeDtypeStruct((T, N), BF16),
        compiler_params=_params(("parallel", "arbitrary")),
        name="proj",
    )(a, w)


def _norm_mm_kernel(x_ref, g_ref, w_ref, o_ref, xn_ref):
    @pl.when(pl.program_id(1) == 0)
    def _():
        xn_ref[...] = (_rms(x_ref[...]) * g_ref[...]).astype(BF16)

    o_ref[...] = _dot(xn_ref[...], w_ref[...]).astype(o_ref.dtype)


def _norm_mm(x, g, w, *, tm, tn, out_dtype=BF16):
    T, D = x.shape
    N = w.shape[1]
    tm = min(tm, T)
    tn = min(tn, N)
    return pl.pallas_call(
        _norm_mm_kernel,
        grid=(T // tm, N // tn),
        in_specs=[
            pl.BlockSpec((tm, D), lambda i, j: (i, 0)),
            pl.BlockSpec((1, D), lambda i, j: (0, 0)),
            pl.BlockSpec((D, tn), lambda i, j: (0, j)),
        ],
        out_specs=pl.BlockSpec((tm, tn), lambda i, j: (i, j)),
        out_shape=jax.ShapeDtypeStruct((T, N), out_dtype),
        scratch_shapes=[pltpu.VMEM((tm, D), BF16)],
        compiler_params=_params(("parallel", "arbitrary")),
        name="norm_proj",
    )(x, g.reshape(1, D), w)


def _mm_res_kernel(a_ref, w_ref, r_ref, o_ref):
    o_ref[...] = r_ref[...] + _dot(a_ref[...], w_ref[...])


def _mm_res(a, w, res, *, tm):
    T, K = a.shape
    N = w.shape[1]
    tm = min(tm, T)
    return pl.pallas_call(
        _mm_res_kernel,
        grid=(T // tm,),
        in_specs=[
            pl.BlockSpec((tm, K), lambda i: (i, 0)),
            pl.BlockSpec((K, N), lambda i: (0, 0)),
            pl.BlockSpec((tm, N), lambda i: (i, 0)),
        ],
        out_specs=pl.BlockSpec((tm, N), lambda i: (i, 0)),
        out_shape=jax.ShapeDtypeStruct((T, N), F32),
        compiler_params=_params(("parallel",)),
        name="out_proj",
    )(a, w, res)


def _attn_a_kernel(bias_ref, q_ref, kp_ref, kc_ref, vp_ref, vc_ref, o_ref, lse_ref, *, heads):
    n = pl.program_id(2)
    variant = jnp.minimum(n, 1)
    blk = ATTN_BLOCK
    lane = lax.broadcasted_iota(I32, (1, LANES), 1)
    first_head = lane < HEAD_DIM
    lse_lane = lax.broadcasted_iota(I32, (blk, LANES), 1)
    lse_all = jnp.zeros((blk, LANES), F32)
    group = 8
    for h0 in range(0, heads, group):
        vals, scores = [], []
        for h in range(h0, h0 + group):
            sl = slice(LANES * (h // 2), LANES * (h // 2 + 1))
            q = q_ref[:, sl]
            mine = first_head if h % 2 == 0 else jnp.logical_not(first_head)
            keys = jnp.concatenate([kp_ref[:, sl], kc_ref[:, sl]], axis=0)
            vals.append(jnp.concatenate([vp_ref[:, sl], vc_ref[:, sl]], axis=0))
            qm = jnp.where(mine, q * (1.0 / math.sqrt(HEAD_DIM)), jnp.zeros_like(q))
            scores.append(_dot_t(qm, keys))
        probs, scales = [], []
        for h, s in zip(range(h0, h0 + group), scores):
            s = s + bias_ref[variant, h]
            m = jnp.max(s, axis=-1, keepdims=True)
            e = jnp.exp(s - m)
            den = jnp.sum(e, axis=-1, keepdims=True)
            probs.append(e.astype(BF16))
            scales.append(1.0 / den)
            lse_all = jnp.where(lse_lane == h, m + jnp.log(den), lse_all)
        outs = [_dot(p, v) * sc for p, v, sc in zip(probs, vals, scales)]
        for idx in range(0, group, 2):
            sl = slice(LANES * ((h0 + idx) // 2), LANES * ((h0 + idx) // 2 + 1))
            o_ref[:, sl] = jnp.where(first_head, outs[idx], outs[idx + 1]).astype(o_ref.dtype)
    lse_ref[...] = lse_all


def _alibi_window_bias(dilation, heads):
    blk = ATTN_BLOCK
    a = np.arange(blk)[:, None]
    b = np.arange(2 * blk)[None, :]
    delta = blk + a - b
    valid = (delta >= 0) & (delta <= blk)
    slopes = 2.0 ** (-8.0 * np.arange(1, heads + 1) / heads)
    bias = -slopes[:, None, None] * (delta * dilation).astype(np.float64)[None]
    bias = np.where(valid[None], bias, -np.inf)
    first = bias.copy()
    first[:, :, :blk] = -np.inf
    return jnp.asarray(np.stack([first, bias]), F32)


def _attn_a_group(qkv, *, batch, streams, length, dilation, heads):
    width = heads * HEAD_DIM
    nb = length // ATTN_BLOCK
    x = qkv.reshape(batch, streams, length, 3 * width)
    blk = (None, None, ATTN_BLOCK, width)
    cur = lambda which: (lambda b, c, n: (b, c, n, which))
    prev = lambda which: (lambda b, c, n: (b, c, jnp.maximum(n - 1, 0), which))
    bias = _alibi_window_bias(dilation, heads)
    o, lse = pl.pallas_call(
        functools.partial(_attn_a_kernel, heads=heads),
        grid=(batch, streams, nb),
        in_specs=[
            pl.BlockSpec(bias.shape, lambda b, c, n: (0, 0, 0, 0)),
            pl.BlockSpec(blk, cur(0)),
            pl.BlockSpec(blk, prev(1)),
            pl.BlockSpec(blk, cur(1)),
            pl.BlockSpec(blk, prev(2)),
            pl.BlockSpec(blk, cur(2)),
        ],
        out_specs=[
            pl.BlockSpec(blk, lambda b, c, n: (b, c, n, 0)),
            pl.BlockSpec((None, None, ATTN_BLOCK, LANES), lambda b, c, n: (b, c, n, 0)),
        ],
        out_shape=[
            jax.ShapeDtypeStruct((batch, streams, length, width), BF16),
            jax.ShapeDtypeStruct((batch, streams, length, LANES), F32),
        ],
        compiler_params=_params(("parallel", "parallel", "arbitrary")),
        name=f"dilated_attn_r{dilation}",
    )(bias, x, x, x, x, x)
    return o, lse


def _proj_a_kernel(a_ref, wq_ref, wk_ref, wvt_ref, qk_ref, vt_ref, *, q_scale):
    a = a_ref[...]
    width = wq_ref.shape[1]
    qk_ref[:, :width] = (_dot(a, wq_ref[...]) * q_scale).astype(BF16)
    qk_ref[:, width:] = _dot(a, wk_ref[...]).astype(BF16)
    vt_ref[...] = _dot_t(wvt_ref[...], a).astype(BF16)


def _proj_a(a, w, wvt, *, group, streams, length, width, tm, q_scale):
    T, D = a.shape
    tm = min(tm, length)
    per_stream = length // tm
    first = group * 3
    qk, vt = pl.pallas_call(
        functools.partial(_proj_a_kernel, q_scale=q_scale),
        grid=(T // tm,),
        in_specs=[
            pl.BlockSpec((tm, D), lambda i: (i, 0)),
            pl.BlockSpec((D, width), lambda i: (0, first)),
            pl.BlockSpec((D, width), lambda i: (0, first + 1)),
            pl.BlockSpec((None, width, D), lambda i: (group, 0, 0)),
        ],
        out_specs=[
            pl.BlockSpec((tm, 2 * width), lambda i: (i, 0)),
            pl.BlockSpec((None, width, tm), lambda i: (i // per_stream, 0, i % per_stream)),
        ],
        out_shape=[
            jax.ShapeDtypeStruct((T, 2 * width), BF16),
            jax.ShapeDtypeStruct((T // length, width, length), BF16),
        ],
        compiler_params=_params(("parallel",)),
        name="proj_qk_vt",
    )(a, w, w, wvt)
    return qk.reshape(T // length, length, 2 * width), vt


def _attn_a_t_kernel(bias_ref, q_ref, kp_ref, kc_ref, vtp_ref, vtc_ref, o_ref, lse_ref, *, heads):
    n = pl.program_id(1)
    variant = jnp.minimum(n, 1)
    blk = ATTN_BLOCK
    row = lax.broadcasted_iota(I32, (LANES, 1), 0)
    first_head = row < HEAD_DIM
    lse_row = lax.broadcasted_iota(I32, (LANES, blk), 0)
    lse_t = jnp.zeros((LANES, blk), F32)
    group = 8
    for h0 in range(0, heads, group):
        hs = range(h0, h0 + group)
        vals, scores = [], []
        for h in hs:
            sl = slice(LANES * (h // 2), LANES * (h // 2 + 1))
            qm = _head_halves(q_ref[:, sl])[h % 2]
            keys = jnp.concatenate([kp_ref[:, sl], kc_ref[:, sl]], axis=0)
            vals.append(jnp.concatenate([vtp_ref[sl, :], vtc_ref[sl, :]], axis=1))
            scores.append(_dot_t(keys, qm))
        probs, scales = [], []
        for h, s in zip(hs, scores):
            s = s + bias_ref[variant, h]
            m = jnp.max(s, axis=0, keepdims=True)
            e = jnp.exp2(s - m)
            den = jnp.sum(e, axis=0, keepdims=True)
            probs.append(e.astype(BF16))
            scales.append(1.0 / den)
            lse_t = jnp.where(lse_row == h, (m + jnp.log2(den)) * (1.0 / LOG2_E), lse_t)
        outs = [_dot(v, p) * sc for v, p, sc in zip(vals, probs, scales)]
        for idx in range(0, group, 2):
            sl = slice(LANES * ((h0 + idx) // 2), LANES * ((h0 + idx) // 2 + 1))
            o_ref[:, sl] = jnp.where(first_head, outs[idx], outs[idx + 1]).T.astype(o_ref.dtype)
    lse_ref[...] = lse_t.T


def _attn_a_t_group(qk, vt, *, dilation, heads):
    n_streams, length, _ = qk.shape
    width = heads * HEAD_DIM
    nb = length // ATTN_BLOCK
    bias = jnp.swapaxes(_alibi_window_bias(dilation, heads) * LOG2_E, 2, 3)
    blk = (None, ATTN_BLOCK, width)
    vblk = (None, width, ATTN_BLOCK)
    o, lse = pl.pallas_call(
        functools.partial(_attn_a_t_kernel, heads=heads),
        grid=(n_streams, nb),
        in_specs=[
            pl.BlockSpec(bias.shape, lambda s, n: (0, 0, 0, 0)),
            pl.BlockSpec(blk, lambda s, n: (s, n, 0)),
            pl.BlockSpec(blk, lambda s, n: (s, jnp.maximum(n - 1, 0), 1)),
            pl.BlockSpec(blk, lambda s, n: (s, n, 1)),
            pl.BlockSpec(vblk, lambda s, n: (s, 0, jnp.maximum(n - 1, 0))),
            pl.BlockSpec(vblk, lambda s, n: (s, 0, n)),
        ],
        out_specs=[
            pl.BlockSpec(blk, lambda s, n: (s, n, 0)),
            pl.BlockSpec((None, ATTN_BLOCK, LANES), lambda s, n: (s, n, 0)),
        ],
        out_shape=[
            jax.ShapeDtypeStruct((n_streams, length, width), BF16),
            jax.ShapeDtypeStruct((n_streams, length, LANES), F32),
        ],
        compiler_params=_params(("parallel", "arbitrary")),
        name=f"dilated_attn_r{dilation}",
    )(bias, qk, qk, qk, vt, vt)
    return o, lse


def _merge_proj_kernel(o0_ref, o1_ref, o2_ref, l0_ref, l1_ref, l2_ref, e_ref, w_ref, r_ref, out_ref):
    l0, l1, l2 = l0_ref[...], l1_ref[...], l2_ref[...]
    mx = jnp.maximum(jnp.maximum(l0, l1), l2)
    e0, e1, e2 = jnp.exp(l0 - mx), jnp.exp(l1 - mx), jnp.exp(l2 - mx)
    inv = 1.0 / (e0 + e1 + e2)
    expand = e_ref[...]
    acc = None
    for e, o_ref in ((e0, o0_ref), (e1, o1_ref), (e2, o2_ref)):
        hi, lo = _split_bf16(e * inv)
        w_full = _dot(hi, expand) + _dot(lo, expand)
        term = w_full * o_ref[...].astype(F32)
        acc = term if acc is None else acc + term
    out_ref[...] = r_ref[...] + _dot(acc.astype(BF16), w_ref[...])


def _merge_proj(outs, lses, w_out, res, *, heads, tm):
    T, width = outs[0].shape
    N = w_out.shape[1]
    tm = min(tm, T)
    expand = np.zeros((LANES, width), np.float32)
    for h in range(heads):
        expand[h, h * HEAD_DIM:(h + 1) * HEAD_DIM] = 1.0
    expand = jnp.asarray(expand, BF16)
    row = lambda i: (i, 0)
    fixed = lambda i: (0, 0)
    return pl.pallas_call(
        _merge_proj_kernel,
        grid=(T // tm,),
        in_specs=[pl.BlockSpec((tm, width), row)] * 3 + [pl.BlockSpec((tm, LANES), row)] * 3 + [
            pl.BlockSpec((LANES, width), fixed),
            pl.BlockSpec((width, N), fixed),
            pl.BlockSpec((tm, N), row),
        ],
        out_specs=pl.BlockSpec((tm, N), row),
        out_shape=jax.ShapeDtypeStruct((T, N), F32),
        compiler_params=_params(("parallel",)),
        name="merge_out_proj",
    )(*outs, *lses, expand, w_out, res)


def _proj_qk_vt_kernel(x_ref, g_ref, wqk_ref, wvt_ref, qk_ref, vt_ref, *, q_width, q_scale):
    xn = (_rms(x_ref[...]) * g_ref[...]).astype(BF16)
    qk = _dot(xn, wqk_ref[...])
    qk_ref[:, :q_width] = (qk[:, :q_width] * q_scale).astype(BF16)
    qk_ref[:, q_width:] = qk[:, q_width:].astype(BF16)
    vt_ref[...] = _dot_t(wvt_ref[...], xn).astype(BF16)


def _proj_qk_vt(h, g, w_in, *, batch, seq, tm, q_scale):
    T, D = h.shape
    width = w_in.shape[1] // 3
    tm = min(tm, seq)
    per_seq = seq // tm
    w = w_in.astype(BF16)
    wqk = w[:, :2 * width]
    wvt = w[:, 2 * width:].T
    fixed = lambda i: (0, 0)
    qk, vt = pl.pallas_call(
        functools.partial(_proj_qk_vt_kernel, q_width=width, q_scale=q_scale),
        grid=(T // tm,),
        in_specs=[
            pl.BlockSpec((tm, D), lambda i: (i, 0)),
            pl.BlockSpec((1, D), fixed),
            pl.BlockSpec((D, 2 * width), fixed),
            pl.BlockSpec((width, D), fixed),
        ],
        out_specs=[
            pl.BlockSpec((tm, 2 * width), lambda i: (i, 0)),
            pl.BlockSpec((None, width, tm), lambda i: (i // per_seq, 0, i % per_seq)),
        ],
        out_shape=[
            jax.ShapeDtypeStruct((T, 2 * width), BF16),
            jax.ShapeDtypeStruct((batch, width, seq), BF16),
        ],
        compiler_params=_params(("parallel",)),
        name="norm_proj_qk_vt",
    )(h, g.reshape(1, D), wqk, wvt)
    return qk.reshape(batch, seq, 2 * width), vt


def _head_halves(q):
    lane = lax.broadcasted_iota(I32, (1, LANES), 1)
    first = lane < HEAD_DIM
    zero = jnp.zeros_like(q)
    return jnp.where(first, q, zero), jnp.where(first, zero, q)


def _attn_b_kernel(slope_ref, q_ref, k_ref, vt_ref, lq1_ref, lk1_ref, lq2_ref, lk2_ref, sub_ref,
                   o_ref, *, t, lambda_init):
    h = pl.program_id(1)
    i = pl.program_id(2)
    slope = slope_ref[h]
    n_col = q_ref.shape[0] // t
    qs = [_head_halves(q_ref[col * t:(col + 1) * t, :]) for col in range(n_col)]
    rel = lax.broadcasted_iota(I32, (t, t), 1) - lax.broadcasted_iota(I32, (t, t), 0)
    rel_bias = slope * rel.astype(F32)

    def step(js, carry, block_modes):
        new = list(carry)
        plans = [[(col, c) for col, mode in enumerate(modes) if mode is not None for c in range(2)]
                 for modes in block_modes]
        starts = [pl.multiple_of(j * t, t) for j in js]
        scores = [[_dot_t(k_ref[pl.ds(st, t), :], qs[col][c]) for col, c in chains]
                  for st, chains in zip(starts, plans)]
        for j, st, chains, modes, block_scores in zip(js, starts, plans, block_modes, scores):
            vt = vt_ref[:, pl.ds(st, t)]
            probs, alphas = [], []
            for (col, c), s in zip(chains, block_scores):
                base = 3 * (2 * col + c)
                m, l = new[base], new[base + 1]
                shift = slope * ((i * n_col + col - j) * t).astype(F32)
                s = s - rel_bias
                if modes[col]:
                    s = jnp.where(rel >= 0, s, -jnp.inf)
                m_new = jnp.maximum(m, jnp.max(s, axis=0, keepdims=True) - shift)
                alpha = jnp.exp2(m - m_new)
                p = jnp.exp2(s - (m_new + shift))
                new[base] = m_new
                new[base + 1] = alpha * l + jnp.sum(p, axis=0, keepdims=True)
                probs.append(p.astype(BF16))
                alphas.append(alpha)
            for (col, c), p, alpha in zip(chains, probs, alphas):
                base = 3 * (2 * col + c)
                new[base + 2] = alpha * new[base + 2] + _dot(vt, p)
        return tuple(new)

    init = (jnp.full((1, t), -jnp.inf, F32), jnp.zeros((1, t), F32), jnp.zeros((LANES, t), F32)) * (2 * n_col)
    full = [(False,) * n_col] * n_col
    carry = lax.fori_loop(0, i, lambda jj, c: step([n_col * jj + b for b in range(n_col)], c, full), init)
    diag = [tuple(None if col < d else (col == d) for col in range(n_col)) for d in range(n_col)]
    carry = step([n_col * i + d for d in range(n_col)], carry, diag)
    lam = (jnp.exp(jnp.sum(lq1_ref[...] * lk1_ref[...], axis=-1, keepdims=True))
           - jnp.exp(jnp.sum(lq2_ref[...] * lk2_ref[...], axis=-1, keepdims=True)) + lambda_init)
    for col in range(n_col):
        _, l1, a1, _, l2, a2 = carry[6 * col:6 * col + 6]
        o = a1 * (1.0 / l1) - lam * (a2 * (1.0 / l2))
        o = o * lax.rsqrt(jnp.mean(o * o, axis=0, keepdims=True) + RMS_EPS)
        o = o * (sub_ref[...] * (1.0 - lambda_init))
        o_ref[col * t:(col + 1) * t, :] = o.T.astype(o_ref.dtype)


def _attn_b(qk, vt, lq1, lk1, lq2, lk2, subln, *, batch, seq, heads, lambda_init, t, tq):
    t = min(t, seq)
    tq = min(tq, seq)
    width = 2 * HEAD_DIM
    slopes = jnp.asarray(LOG2_E * 2.0 ** (-8.0 * np.arange(1, heads + 1) / heads), F32)
    vec = lambda a: a.reshape(1, -1).astype(F32)
    small = lambda n: pl.BlockSpec((1, n), lambda b, h, i, s: (0, 0))
    return pl.pallas_call(
        functools.partial(_attn_b_kernel, t=t, lambda_init=lambda_init),
        grid_spec=pltpu.PrefetchScalarGridSpec(
            num_scalar_prefetch=1,
            grid=(batch, heads, seq // tq),
            in_specs=[
                pl.BlockSpec((None, tq, width), lambda b, h, i, s: (b, i, h)),
                pl.BlockSpec((None, seq, width), lambda b, h, i, s: (b, 0, heads + h)),
                pl.BlockSpec((None, width, seq), lambda b, h, i, s: (b, h, 0)),
                small(HEAD_DIM), small(HEAD_DIM), small(HEAD_DIM), small(HEAD_DIM),
                pl.BlockSpec((width, 1), lambda b, h, i, s: (0, 0)),
            ],
            out_specs=pl.BlockSpec((None, tq, width), lambda b, h, i, s: (b, i, h)),
        ),
        out_shape=jax.ShapeDtypeStruct((batch, seq, heads * width), BF16),
        compiler_params=_params(("parallel", "parallel", "arbitrary")),
        name="diff_attn",
    )(slopes, qk, qk, vt, vec(lq1), vec(lk1), vec(lq2), vec(lk2), subln.reshape(width, 1).astype(F32))


def _attn_c_kernel(q_ref, k_ref, vt_ref, later_ref, o_ref):
    i = pl.program_id(2)
    tq = q_ref.shape[0]
    kw = LANES
    n_sub = tq // kw
    qs = _head_halves(q_ref[...])
    later = later_ref[...]
    rel = lax.broadcasted_iota(I32, (kw, tq), 1) - lax.broadcasted_iota(I32, (kw, tq), 0)

    def blocks(kbs, carry, diagonal):
        units = []
        for kb in kbs:
            start = pl.multiple_of(kb * kw, kw)
            k = k_ref[pl.ds(start, kw), :]
            vt = vt_ref[:, pl.ds(start, kw)]
            causal = (rel + (i * tq - kb * kw)) > 0 if diagonal else None
            for head in range(2):
                units.append(dict(head=head, vt=vt, causal=causal, z=_dot_t(k, qs[head])))
        for u in units:
            z = u["z"]
            softplus = jnp.maximum(z, 0.0) + jnp.log2(1.0 + jnp.exp2(-jnp.abs(z)))
            rest = jnp.where(u["causal"], softplus, 0.0) if diagonal else softplus
            u["logit"] = z - softplus
            u["rest"] = rest
            u["split"] = _split_bf16(rest)
        for u in units:
            hi, lo = u["split"]
            u["after"] = _dot(later, hi) + _dot(later, lo)
        c = [carry[0], carry[2]]
        weights = []
        for u in units:
            a = jnp.exp2(u["logit"] - u["after"] - c[u["head"]])
            if diagonal:
                a = jnp.where(u["causal"], a, 0.0)
            weights.append(a.astype(BF16))
            c[u["head"]] = c[u["head"]] + jnp.sum(u["rest"], axis=0, keepdims=True)
        acc = [carry[1], carry[3]]
        for u, a in zip(units, weights):
            acc[u["head"]] = acc[u["head"]] + _dot(u["vt"], a)
        return c[0], acc[0], c[1], acc[1]

    def alive(carry):
        return (jnp.min(jnp.minimum(carry[0], carry[2])) < -F32_EXP2_ZERO).astype(I32)

    zc = jnp.zeros((1, tq), F32)
    za = jnp.zeros((LANES, tq), F32)
    carry = blocks([i * n_sub + d for d in reversed(range(n_sub))], (zc, za, zc, za), True)

    def cond(state):
        return jnp.logical_and(state[0] >= 0, state[1] > 0)

    def body(state):
        new = blocks([state[0], state[0] - 1], state[2:], False)
        return (state[0] - 2, alive(new)) + new

    out = lax.while_loop(cond, body, (i * n_sub - 1, alive(carry)) + carry)
    row = lax.broadcasted_iota(I32, (LANES, 1), 0)
    o = jnp.where(row < HEAD_DIM, out[3], out[5])
    o_ref[...] = o.T.astype(o_ref.dtype)


def _attn_c(qk, vt, *, batch, seq, heads, tq):
    tq = min(tq, seq)
    assert seq % tq == 0 and (tq // LANES) % 2 == 0
    pairs = heads // 2
    later = jnp.asarray(np.triu(np.ones((LANES, LANES), np.float32), 1), BF16)
    return pl.pallas_call(
        _attn_c_kernel,
        grid=(batch, pairs, seq // tq),
        in_specs=[
            pl.BlockSpec((None, tq, LANES), lambda b, h, i: (b, i, h)),
            pl.BlockSpec((None, seq, LANES), lambda b, h, i: (b, 0, pairs + h)),
            pl.BlockSpec((None, LANES, seq), lambda b, h, i: (b, h, 0)),
            pl.BlockSpec((LANES, LANES), lambda b, h, i: (0, 0)),
        ],
        out_specs=pl.BlockSpec((None, tq, LANES), lambda b, h, i: (b, i, h)),
        out_shape=jax.ShapeDtypeStruct((batch, seq, heads * HEAD_DIM), BF16),
        compiler_params=_params(("parallel", "parallel", "arbitrary")),
        name="stick_breaking_attn",
    )(qk, qk, vt, later)


def _router_kernel(x_ref, g_ref, whi_ref, wlo_ref, b_ref, u_ref, info_ref, cnt_ref, run_ref,
                   *, groups, per_group):
    G, E = groups, per_group
    tm = x_ref.shape[0]
    ne = G * E

    @pl.when(pl.program_id(0) == 0)
    def _():
        run_ref[...] = jnp.zeros_like(run_ref)

    xn = _rms(x_ref[...]) * g_ref[...]
    hi, lo = _split_bf16(xn)
    whi = whi_ref[...]
    lt = _dot_t(whi, hi) + _dot_t(whi, lo) + _dot_t(wlo_ref[...], hi) + b_ref[...]
    sub = lax.broadcasted_iota(I32, (E, tm), 0)
    coarse = lt[0:G]
    cmax = jnp.max(coarse, axis=0, keepdims=True)
    g_prob = 1.0 / jnp.sum(jnp.exp(coarse - cmax), axis=0, keepdims=True)
    g_idx = jnp.min(jnp.where(coarse == cmax, sub, G), axis=0, keepdims=True)
    fine = jnp.zeros((E, tm), F32)
    for grp in range(G):
        fine = jnp.where(g_idx == grp, lt[G + grp * E:G + (grp + 1) * E], fine)
    fmax = jnp.max(fine, axis=0, keepdims=True)
    fsum = jnp.sum(jnp.exp(fine - fmax), axis=0, keepdims=True)
    i1 = jnp.min(jnp.where(fine == fmax, sub, E), axis=0, keepdims=True)
    rest = jnp.where(sub == i1, -jnp.inf, fine)
    m2 = jnp.max(rest, axis=0, keepdims=True)
    i2 = jnp.min(jnp.where(rest == m2, sub, E), axis=0, keepdims=True)
    p1 = 1.0 / fsum
    p2 = jnp.exp(m2 - fmax) / fsum
    norm = p1 + p2
    gate1 = g_prob * (p1 / norm)
    gate2 = g_prob * (p2 / norm)
    e1 = g_idx * E + i1
    e2 = g_idx * E + i2
    ex = lax.broadcasted_iota(I32, (ne, tm), 0)
    oh1 = (ex == e1).astype(F32)
    oh2 = (ex == e2).astype(F32)
    both = oh1 + oh2
    earlier = _dot(both.astype(BF16), u_ref[...]) + run_ref[:, 0:1]
    r1 = jnp.sum(oh1 * earlier, axis=0, keepdims=True)
    r2 = jnp.sum(oh2 * earlier, axis=0, keepdims=True)
    total = run_ref[...] + jnp.sum(both, axis=1, keepdims=True)
    run_ref[...] = total
    cnt_ref[...] = total
    row = lax.broadcasted_iota(I32, (SUBLANES, tm), 0)
    fields = (e1.astype(F32), e2.astype(F32), r1, r2, gate1, gate2)
    info = jnp.zeros((SUBLANES, tm), F32)
    for idx, f in enumerate(fields):
        info = jnp.where(row == idx, f, info)
    info_ref[...] = info


def _router(h, g, w_coarse, b_coarse, w_fine, b_fine, *, tm):
    T, D = h.shape
    tm = min(tm, T)
    G, ne = w_coarse.shape[1], w_fine.shape[1]
    pad = LANES - G - ne
    wt = jnp.pad(jnp.concatenate([w_coarse, w_fine], axis=1), ((0, 0), (0, pad))).T
    b = jnp.pad(jnp.concatenate([b_coarse, b_fine]), (0, pad)).reshape(LANES, 1)
    whi, wlo = _split_bf16(wt)
    before = jnp.asarray(np.triu(np.ones((tm, tm), np.float32), 1), BF16)
    fixed = lambda i: (0, 0)
    return pl.pallas_call(
        functools.partial(_router_kernel, groups=G, per_group=ne // G),
        grid=(T // tm,),
        in_specs=[
            pl.BlockSpec((tm, D), lambda i: (i, 0)),
            pl.BlockSpec((1, D), fixed),
            pl.BlockSpec((LANES, D), fixed),
            pl.BlockSpec((LANES, D), fixed),
            pl.BlockSpec((LANES, 1), fixed),
            pl.BlockSpec((tm, tm), fixed),
        ],
        out_specs=[pl.BlockSpec((SUBLANES, tm), lambda i: (0, i)), pl.BlockSpec((ne, LANES), fixed)],
        out_shape=[jax.ShapeDtypeStruct((SUBLANES, T), F32), jax.ShapeDtypeStruct((ne, LANES), F32)],
        scratch_shapes=[pltpu.VMEM((ne, LANES), F32)],
        compiler_params=_params(("arbitrary",)),
        name="moe_router",
    )(h, g.reshape(1, D), whi, wlo, b, before)


def _plan(info, counts, *, tile):
    T = info.shape[1]
    ne = counts.shape[0]
    C = FFN_CHUNK
    experts = info[0:2].astype(I32)
    ranks = info[2:4].astype(I32)
    gates = info[4:6]
    cnt = counts[:, 0].astype(I32)
    padded = (cnt + C - 1) // C * C
    pad_end = jnp.cumsum(padded)
    pad_start = pad_end - padded
    ids = jnp.arange(ne, dtype=I32)
    start_of = jnp.sum(jnp.where(experts[:, :, None] == ids, pad_start, 0), axis=-1)
    dest = ranks + start_of
    n_chunks = (MOE_TOP_K * T + ne * (C - 1) + C - 1) // C
    n_used = (pad_end[-1] // C).astype(I32)
    first_row = jnp.minimum(jnp.arange(n_chunks, dtype=I32), n_used - 1) * C
    chunk_expert = jnp.minimum(jnp.sum(pad_end[None, :] <= first_row[:, None], axis=1), ne - 1).astype(I32)
    dest_tiles = dest.reshape(MOE_TOP_K, T // tile, tile).transpose(1, 0, 2).reshape(T // tile, 1, MOE_TOP_K * tile)
    return dest_tiles, gates.T, chunk_expert, n_used.reshape(1), n_chunks


def _dispatch_kernel(dst_ref, h_ref, g_ref, xs_in, xs_out, xbuf, sem):
    del xs_in
    i = pl.program_id(0)
    n = pl.num_programs(0)
    tm, D = h_ref.shape
    slot = i % 2

    def wait(s):
        for _ in range(MOE_TOP_K):
            pltpu.make_async_copy(xbuf.at[s], xs_out.at[pl.ds(0, tm)], sem.at[s]).wait()

    @pl.when(i >= 2)
    def _():
        wait(slot)

    xn = _rms(h_ref[...]) * g_ref[...]
    xbuf[slot] = _pack_bf16_pair(xn[:, :D // 2], xn[:, D // 2:])

    def issue(r, _):
        for k in range(MOE_TOP_K):
            pltpu.make_async_copy(xbuf.at[slot, pl.ds(r, 1)], xs_out.at[pl.ds(dst_ref[0, 0, k * tm + r], 1)],
                                  sem.at[slot]).start()
        return 0

    lax.fori_loop(0, tm, issue, 0, unroll=True)

    @pl.when(i == n - 1)
    def _():
        wait(slot)

        @pl.when(i >= 1)
        def _():
            wait(1 - slot)


def _dispatch(h, g, dest_tiles, n_rows, *, tm):
    T, D = h.shape
    xs0 = jnp.zeros((n_rows, D // 2), U32)
    return pl.pallas_call(
        _dispatch_kernel,
        grid=(T // tm,),
        in_specs=[
            pl.BlockSpec((1, 1, MOE_TOP_K * tm), lambda i: (i, 0, 0), memory_space=pltpu.SMEM),
            pl.BlockSpec((tm, D), lambda i: (i, 0)),
            pl.BlockSpec((1, D), lambda i: (0, 0)),
            pl.BlockSpec(memory_space=pl.ANY),
        ],
        out_specs=pl.BlockSpec(memory_space=pl.ANY),
        out_shape=jax.ShapeDtypeStruct((n_rows, D // 2), U32),
        scratch_shapes=[pltpu.VMEM((2, tm, D // 2), U32), pltpu.SemaphoreType.DMA((2,))],
        input_output_aliases={3: 0},
        compiler_params=_params(("arbitrary",)),
        name="moe_dispatch",
    )(dest_tiles, h, g.reshape(1, D), xs0)


def _ffn_kernel(ce_ref, nu_ref, x_ref, wgu_ref, wdn_ref, y_ref, wgu_bf, wdn_bf, *, d_expert):
    c = pl.program_id(0)
    n_used = nu_ref[0]

    @pl.when(c < n_used)
    def _():
        @pl.when(jnp.logical_or(c == 0, ce_ref[c] != ce_ref[jnp.maximum(c - 1, 0)]))
        def _():
            wgu_bf[...] = wgu_ref[...].astype(BF16)
            wdn_bf[...] = wdn_ref[...].astype(BF16)

        half = x_ref.shape[1]
        xa, xb = _unpack_bf16_pair(x_ref[...])
        n_blk = 2
        w = d_expert // n_blk
        proj = lambda lo: _dot(xa, wgu_bf[:half, lo:lo + w]) + _dot(xb, wgu_bf[half:, lo:lo + w])
        gates = [proj(b * w) for b in range(n_blk)]
        ups = [proj(d_expert + b * w) for b in range(n_blk)]
        acts = [((g * jax.nn.sigmoid(g)) * u).astype(BF16) for g, u in zip(gates, ups)]
        y = _dot(acts[0], wdn_bf[:w])
        for b in range(1, n_blk):
            y = y + _dot(acts[b], wdn_bf[b * w:(b + 1) * w])
        y_ref[...] = y

    @pl.when(c >= n_used)
    def _():
        y_ref[...] = jnp.zeros_like(y_ref)


def _ffn(xs, chunk_expert, n_used, w_gate_up, w_down, layer):
    n_rows, half = xs.shape
    C = FFN_CHUNK
    n_chunks = n_rows // C
    D, F2 = w_gate_up.shape[2:]
    d_expert = w_down.shape[2]
    return pl.pallas_call(
        functools.partial(_ffn_kernel, d_expert=d_expert),
        grid_spec=pltpu.PrefetchScalarGridSpec(
            num_scalar_prefetch=2,
            grid=(n_chunks,),
            in_specs=[
                pl.BlockSpec((C, half), lambda c, ce, nu: (jnp.minimum(c, nu[0] - 1), 0)),
                pl.BlockSpec((None, None, D, F2), lambda c, ce, nu: (layer, ce[c], 0, 0)),
                pl.BlockSpec((None, None, d_expert, D), lambda c, ce, nu: (layer, ce[c], 0, 0)),
            ],
            out_specs=pl.BlockSpec((C, D), lambda c, ce, nu: (c, 0)),
            scratch_shapes=[pltpu.VMEM((D, F2), BF16), pltpu.VMEM((d_expert, D), BF16)],
        ),
        out_shape=jax.ShapeDtypeStruct((n_rows, D), F32),
        compiler_params=_params(("arbitrary",)),
        name="moe_expert_ffn",
    )(chunk_expert, n_used, xs, w_gate_up, w_down)


def _combine_ple_kernel(dst_ref, dstn_ref, h_ref, gt_ref, p_ref, g_ref, wp_ref, wg_ref, gf_ref, y_hbm,
                        o_ref, ybuf, sem, *, final):
    i = pl.program_id(0)
    n = pl.num_programs(0)
    tm, D = h_ref.shape
    rows = MOE_TOP_K * tm
    slot = i % 2

    def gather(dref, s):
        def issue(r, _):
            for k in range(MOE_TOP_K):
                pltpu.make_async_copy(y_hbm.at[pl.ds(dref[0, 0, k * tm + r], 1)],
                                      ybuf.at[s, pl.ds(k * tm + r, 1)], sem.at[s]).start()
            return 0
        lax.fori_loop(0, tm, issue, 0, unroll=True)

    @pl.when(i == 0)
    def _():
        gather(dst_ref, 0)

    def wait(s):
        pltpu.make_async_copy(y_hbm.at[pl.ds(0, rows)], ybuf.at[s], sem.at[s]).wait()

    wait(slot)
    gather(dstn_ref, 1 - slot)
    gt = gt_ref[...]
    h = h_ref[...] + (gt[:, 0:1] * ybuf[slot, :tm] + gt[:, 1:2] * ybuf[slot, tm:])
    xn = (_rms(h) * g_ref[...]).astype(BF16)
    gate = jax.nn.sigmoid(_dot(xn, wg_ref[...]))
    out = h + _dot(p_ref[...].astype(BF16), wp_ref[...]) * gate
    if final:
        out = _rms(out) * gf_ref[...]
    o_ref[...] = out

    @pl.when(i == n - 1)
    def _():
        wait(1 - slot)


def _combine_ple(h, y, dest_tiles, gates, p, layer, g, w_proj, w_gate, g_final, *, final, tm):
    T, D = h.shape
    Pd = p.shape[2]
    n_tiles = T // tm
    row = lambda i: (i, 0)
    fixed = lambda i: (0, 0)
    smem = functools.partial(pl.BlockSpec, (1, 1, MOE_TOP_K * tm), memory_space=pltpu.SMEM)
    return pl.pallas_call(
        functools.partial(_combine_ple_kernel, final=final),
        grid=(n_tiles,),
        in_specs=[
            smem(lambda i: (i, 0, 0)),
            smem(lambda i: (jnp.minimum(i + 1, n_tiles - 1), 0, 0)),
            pl.BlockSpec((tm, D), row),
            pl.BlockSpec((tm, MOE_TOP_K), row),
            pl.BlockSpec((None, tm, Pd), lambda i: (layer, i, 0)),
            pl.BlockSpec((1, D), fixed),
            pl.BlockSpec((Pd, D), fixed),
            pl.BlockSpec((D, D), fixed),
            pl.BlockSpec((1, D), fixed),
            pl.BlockSpec(memory_space=pl.ANY),
        ],
        out_specs=pl.BlockSpec((tm, D), row),
        out_shape=jax.ShapeDtypeStruct((T, D), F32),
        scratch_shapes=[pltpu.VMEM((2, MOE_TOP_K * tm, D), F32), pltpu.SemaphoreType.DMA((2,))],
        compiler_params=_params(("arbitrary",)),
        name="moe_combine_ple",
    )(dest_tiles, dest_tiles, h, gates, p, g.reshape(1, D), w_proj, w_gate, g_final.reshape(1, D), y)


def _moe_ple(h, p, layer, g_ffn, w_coarse, b_coarse, w_fine, b_fine, w_gate_up, w_down, g_ple, w_proj, w_gate,
             g_final, *, final):
    tile = min(512, h.shape[0])
    info, counts = _router(h, g_ffn, w_coarse, b_coarse, w_fine, b_fine, tm=min(512, h.shape[0]))
    dest_tiles, gates, chunk_expert, n_used, n_chunks = _plan(info, counts, tile=tile)
    xs = _dispatch(h, g_ffn, dest_tiles, n_chunks * FFN_CHUNK, tm=tile)
    y = _ffn(xs, chunk_expert, n_used, w_gate_up, w_down, layer)
    return _combine_ple(h, y, dest_tiles, gates, p, layer, g_ple, w_proj.astype(BF16), w_gate.astype(BF16),
                        g_final, final=final, tm=tile)


def _mixer_a(h, g, w_in, w_out, *, batch, seq):
    T, D = h.shape
    width = A_HEADS * HEAD_DIM
    n_groups = len(A_PATTERNS)
    w = w_in.astype(BF16)
    wvt = jnp.swapaxes(w.reshape(D, n_groups, 3, width)[:, :, 2], 0, 1).swapaxes(1, 2)
    xn = _norm(h, g, tm=1024)
    outs, lses = [], []
    for grp, (window, r) in enumerate(A_PATTERNS):
        assert window // r == ATTN_BLOCK and seq % (r * ATTN_BLOCK) == 0
        L = seq // r
        to_streams = lambda a: a.reshape(batch, L, r, -1).transpose(0, 2, 1, 3).reshape(T, -1)
        from_streams = lambda a: a.reshape(batch, r, L, -1).transpose(0, 2, 1, 3).reshape(T, -1)
        xs = xn if r == 1 else to_streams(xn)
        qk, vt = _proj_a(xs, w, wvt, group=grp, streams=batch * r, length=L, width=width, tm=512,
                         q_scale=LOG2_E / math.sqrt(HEAD_DIM))
        o, lse = _attn_a_t_group(qk, vt, dilation=r, heads=A_HEADS)
        outs.append(from_streams(o))
        lses.append(from_streams(lse))
    return _merge_proj(outs, lses, w_out.astype(BF16), h, heads=A_HEADS, tm=512)


def _mixer_b(h, g, w_in, w_out, lq1, lk1, lq2, lk2, subln, lambda_init, *, batch, seq):
    D = h.shape[1]
    heads = D // (2 * HEAD_DIM)
    qk, vt = _proj_qk_vt(h, g, w_in, batch=batch, seq=seq, tm=512, q_scale=LOG2_E / math.sqrt(HEAD_DIM))
    o = _attn_b(qk, vt, lq1, lk1, lq2, lk2, subln, batch=batch, seq=seq, heads=heads,
                lambda_init=lambda_init, t=256, tq=512)
    return _mm_res(o.reshape(batch * seq, -1), w_out.astype(BF16), h, tm=512)


def _mixer_c(h, g, w_in, w_out, *, batch, seq):
    D = h.shape[1]
    heads = D // HEAD_DIM
    qk, vt = _proj_qk_vt(h, g, w_in, batch=batch, seq=seq, tm=512, q_scale=LOG2_E / math.sqrt(HEAD_DIM))
    o = _attn_c(qk, vt, batch=batch, seq=seq, heads=heads, tq=512)
    return _mm_res(o.reshape(batch * seq, -1), w_out.astype(BF16), h, tm=512)


def kernel(x, p, norm_mix, norm_ffn, norm_ple, norm_final, a_w_in, a_w_out, b_w_in, b_w_out,
           b_lambda_q1, b_lambda_k1, b_lambda_q2, b_lambda_k2, b_subln, c_w_in, c_w_out,
           moe_w_coarse, moe_b_coarse, moe_w_fine, moe_b_fine, moe_w_gate_up, moe_w_down,
           ple_w_proj, ple_w_gate):
    batch, seq, D = x.shape
    depth = p.shape[0]
    T = batch * seq
    h = x.reshape(T, D)
    p_rows = p.reshape(depth, T, -1)
    for i in range(depth):
        kind = i % N_MIXERS
        j = i // N_MIXERS
        if kind == 0:
            h = _mixer_a(h, norm_mix[i], a_w_in[j], a_w_out[j], batch=batch, seq=seq)
        elif kind == 1:
            lambda_init = 0.8 - 0.6 * math.exp(-0.3 * i)
            h = _mixer_b(h, norm_mix[i], b_w_in[j], b_w_out[j], b_lambda_q1[j], b_lambda_k1[j],
                         b_lambda_q2[j], b_lambda_k2[j], b_subln[j], lambda_init, batch=batch, seq=seq)
        else:
            h = _mixer_c(h, norm_mix[i], c_w_in[j], c_w_out[j], batch=batch, seq=seq)
        h = _moe_ple(h, p_rows, i, norm_ffn[i], moe_w_coarse[i], moe_b_coarse[i], moe_w_fine[i],
                     moe_b_fine[i], moe_w_gate_up, moe_w_down, norm_ple[i], ple_w_proj[i],
                     ple_w_gate[i], norm_final, final=(i == depth - 1))
    return h.reshape(batch, seq, D)
```

```python
import functools
import math

import jax
import jax.numpy as jnp
import numpy as np
from jax import lax
from jax.experimental import pallas as pl
from jax.experimental.pallas import tpu as pltpu

F32 = jnp.float32
BF16 = jnp.bfloat16
U32 = jnp.uint32
I32 = jnp.int32

HEAD_DIM = 64
N_MIXERS = 3
ATTN_BLOCK = 128
RMS_EPS = 1e-6
A_HEADS = 16
A_PATTERNS = ((128, 1), (512, 4), (2048, 16))
MOE_GROUPS = 8
MOE_EXPERTS_PER_GROUP = 8
MOE_TOP_K = 2
FFN_CHUNK = 256
LANES = 128
SUBLANES = 8
F32_EXP2_ZERO = -151.0
LOG2_E = math.log2(math.e)
VMEM_LIMIT = 48 * 1024 * 1024


def _params(sem, vmem=VMEM_LIMIT):
    return pltpu.CompilerParams(dimension_semantics=sem, vmem_limit_bytes=vmem)


def _rms(x):
    return x * lax.rsqrt(jnp.mean(x * x, axis=-1, keepdims=True) + RMS_EPS)


def _dot_t(a, b):
    return lax.dot_general(a, b, (((1,), (1,)), ((), ())), preferred_element_type=F32)


def _dot(a, b):
    return jnp.dot(a, b, preferred_element_type=F32)


def _split_bf16(x):
    hi = x.astype(BF16)
    lo = (x - hi.astype(F32)).astype(BF16)
    return hi, lo


def _bits(x):
    return lax.bitcast_convert_type(x, U32)


def _pack_bf16_pair(a, b):
    a = a.astype(BF16).astype(F32)
    b = b.astype(BF16).astype(F32)
    return (_bits(a) >> 16) | _bits(b)


def _unpack_bf16_pair(w):
    a = lax.bitcast_convert_type(w << 16, F32).astype(BF16)
    b = lax.bitcast_convert_type(w & jnp.uint32(0xFFFF0000), F32).astype(BF16)
    return a, b


def _norm_kernel(x_ref, g_ref, o_ref):
    o_ref[...] = (_rms(x_ref[...]) * g_ref[...]).astype(o_ref.dtype)


def _norm(x, g, *, tm):
    T, D = x.shape
    tm = min(tm, T)
    return pl.pallas_call(
        _norm_kernel,
        grid=(T // tm,),
        in_specs=[pl.BlockSpec((tm, D), lambda i: (i, 0)), pl.BlockSpec((1, D), lambda i: (0, 0))],
        out_specs=pl.BlockSpec((tm, D), lambda i: (i, 0)),
        out_shape=jax.ShapeDtypeStruct((T, D), BF16),
        compiler_params=_params(("parallel",)),
        name="rmsnorm",
    )(x, g.reshape(1, D))


def _mm_kernel(a_ref, w_ref, o_ref):
    o_ref[...] = _dot(a_ref[...], w_ref[...]).astype(o_ref.dtype)


def _mm(a, w, *, col0, n_out, tm, tn):
    T, K = a.shape
    N = n_out
    tm = min(tm, T)
    tn = min(tn, N)
    first = col0 // tn
    return pl.pallas_call(
        _mm_kernel,
        grid=(T // tm, N // tn),
        in_specs=[pl.BlockSpec((tm, K), lambda i, j: (i, 0)), pl.BlockSpec((K, tn), lambda i, j: (0, first + j))],
        out_specs=pl.BlockSpec((tm, tn), lambda i, j: (i, j)),
        out_shape=jax.ShapeDtypeStruct((T, N), BF16),
        compiler_params=_params(("parallel", "arbitrary")),
        name="proj",
    )(a, w)


def _norm_mm_kernel(x_ref, g_ref, w_ref, o_ref, xn_ref):
    @pl.when(pl.program_id(1) == 0)
    def _():
        xn_ref[...] = (_rms(x_ref[...]) * g_ref[...]).astype(BF16)

    o_ref[...] = _dot(xn_ref[...], w_ref[...]).astype(o_ref.dtype)


def _norm_mm(x, g, w, *, tm, tn, out_dtype=BF16):
    T, D = x.shape
    N = w.shape[1]
    tm = min(tm, T)
    tn = min(tn, N)
    return pl.pallas_call(
        _norm_mm_kernel,
        grid=(T // tm, N // tn),
        in_specs=[
            pl.BlockSpec((tm, D), lambda i, j: (i, 0)),
            pl.BlockSpec((1, D), lambda i, j: (0, 0)),
            pl.BlockSpec((D, tn), lambda i, j: (0, j)),
        ],
        out_specs=pl.BlockSpec((tm, tn), lambda i, j: (i, j)),
        out_shape=jax.ShapeDtypeStruct((T, N), out_dtype),
        scratch_shapes=[pltpu.VMEM((tm, D), BF16)],
        compiler_params=_params(("parallel", "arbitrary")),
        name="norm_proj",
    )(x, g.reshape(1, D), w)


def _mm_res_kernel(a_ref, w_ref, r_ref, o_ref):
    o_ref[...] = r_ref[...] + _dot(a_ref[...], w_ref[...])


def _mm_res(a, w, res, *, tm):
    T, K = a.shape
    N = w.shape[1]
    tm = min(tm, T)
    return pl.pallas_call(
        _mm_res_kernel,
        grid=(T // tm,),
        in_specs=[
            pl.BlockSpec((tm, K), lambda i: (i, 0)),
            pl.BlockSpec((K, N), lambda i: (0, 0)),
            pl.BlockSpec((tm, N), lambda i: (i, 0)),
        ],
        out_specs=pl.BlockSpec((tm, N), lambda i: (i, 0)),
        out_shape=jax.ShapeDtypeStruct((T, N), F32),
        compiler_params=_params(("parallel",)),
        name="out_proj",
    )(a, w, res)


def _attn_a_kernel(bias_ref, q_ref, kp_ref, kc_ref, vp_ref, vc_ref, o_ref, lse_ref, *, heads):
    n = pl.program_id(2)
    variant = jnp.minimum(n, 1)
    blk = ATTN_BLOCK
    lane = lax.broadcasted_iota(I32, (1, LANES), 1)
    first_head = lane < HEAD_DIM
    lse_lane = lax.broadcasted_iota(I32, (blk, LANES), 1)
    lse_all = jnp.zeros((blk, LANES), F32)
    group = 8
    for h0 in range(0, heads, group):
        vals, scores = [], []
        for h in range(h0, h0 + group):
            sl = slice(LANES * (h // 2), LANES * (h // 2 + 1))
            q = q_ref[:, sl]
            mine = first_head if h % 2 == 0 else jnp.logical_not(first_head)
            keys = jnp.concatenate([kp_ref[:, sl], kc_ref[:, sl]], axis=0)
            vals.append(jnp.concatenate([vp_ref[:, sl], vc_ref[:, sl]], axis=0))
            qm = jnp.where(mine, q * (1.0 / math.sqrt(HEAD_DIM)), jnp.zeros_like(q))
            scores.append(_dot_t(qm, keys))
        probs, scales = [], []
        for h, s in zip(range(h0, h0 + group), scores):
            s = s + bias_ref[variant, h]
            m = jnp.max(s, axis=-1, keepdims=True)
            e = jnp.exp(s - m)
            den = jnp.sum(e, axis=-1, keepdims=True)
            probs.append(e.astype(BF16))
            scales.append(1.0 / den)
            lse_all = jnp.where(lse_lane == h, m + jnp.log(den), lse_all)
        outs = [_dot(p, v) * sc for p, v, sc in zip(probs, vals, scales)]
        for idx in range(0, group, 2):
            sl = slice(LANES * ((h0 + idx) // 2), LANES * ((h0 + idx) // 2 + 1))
            o_ref[:, sl] = jnp.where(first_head, outs[idx], outs[idx + 1]).astype(o_ref.dtype)
    lse_ref[...] = lse_all


def _alibi_window_bias(dilation, heads):
    blk = ATTN_BLOCK
    a = np.arange(blk)[:, None]
    b = np.arange(2 * blk)[None, :]
    delta = blk + a - b
    valid = (delta >= 0) & (delta <= blk)
    slopes = 2.0 ** (-8.0 * np.arange(1, heads + 1) / heads)
    bias = -slopes[:, None, None] * (delta * dilation).astype(np.float64)[None]
    bias = np.where(valid[None], bias, -np.inf)
    first = bias.copy()
    first[:, :, :blk] = -np.inf
    return jnp.asarray(np.stack([first, bias]), F32)


def _attn_a_group(qkv, *, batch, streams, length, dilation, heads):
    width = heads * HEAD_DIM
    nb = length // ATTN_BLOCK
    x = qkv.reshape(batch, streams, length, 3 * width)
    blk = (None, None, ATTN_BLOCK, width)
    cur = lambda which: (lambda b, c, n: (b, c, n, which))
    prev = lambda which: (lambda b, c, n: (b, c, jnp.maximum(n - 1, 0), which))
    bias = _alibi_window_bias(dilation, heads)
    o, lse = pl.pallas_call(
        functools.partial(_attn_a_kernel, heads=heads),
        grid=(batch, streams, nb),
        in_specs=[
            pl.BlockSpec(bias.shape, lambda b, c, n: (0, 0, 0, 0)),
            pl.BlockSpec(blk, cur(0)),
            pl.BlockSpec(blk, prev(1)),
            pl.BlockSpec(blk, cur(1)),
            pl.BlockSpec(blk, prev(2)),
            pl.BlockSpec(blk, cur(2)),
        ],
        out_specs=[
            pl.BlockSpec(blk, lambda b, c, n: (b, c, n, 0)),
            pl.BlockSpec((None, None, ATTN_BLOCK, LANES), lambda b, c, n: (b, c, n, 0)),
        ],
        out_shape=[
            jax.ShapeDtypeStruct((batch, streams, length, width), BF16),
            jax.ShapeDtypeStruct((batch, streams, length, LANES), F32),
        ],
        compiler_params=_params(("parallel", "parallel", "arbitrary")),
        name=f"dilated_attn_r{dilation}",
    )(bias, x, x, x, x, x)
    return o, lse


def _proj_a_kernel(a_ref, wq_ref, wk_ref, wvt_ref, qk_ref, vt_ref, *, q_scale):
    a = a_ref[...]
    width = wq_ref.shape[1]
    qk_ref[:, :width] = (_dot(a, wq_ref[...]) * q_scale).astype(BF16)
    qk_ref[:, width:] = _dot(a, wk_ref[...]).astype(BF16)
    vt_ref[...] = _dot_t(wvt_ref[...], a).astype(BF16)


def _proj_a(a, w, wvt, *, group, streams, length, width, tm, q_scale):
    T, D = a.shape
    tm = min(tm, length)
    per_stream = length // tm
    first = group * 3
    qk, vt = pl.pallas_call(
        functools.partial(_proj_a_kernel, q_scale=q_scale),
        grid=(T // tm,),
        in_specs=[
            pl.BlockSpec((tm, D), lambda i: (i, 0)),
            pl.BlockSpec((D, width), lambda i: (0, first)),
            pl.BlockSpec((D, width), lambda i: (0, first + 1)),
            pl.BlockSpec((None, width, D), lambda i: (group, 0, 0)),
        ],
        out_specs=[
            pl.BlockSpec((tm, 2 * width), lambda i: (i, 0)),
            pl.BlockSpec((None, width, tm), lambda i: (i // per_stream, 0, i % per_stream)),
        ],
        out_shape=[
            jax.ShapeDtypeStruct((T, 2 * width), BF16),
            jax.ShapeDtypeStruct((T // length, width, length), BF16),
        ],
        compiler_params=_params(("parallel",)),
        name="proj_qk_vt",
    )(a, w, w, wvt)
    return qk.reshape(T // length, length, 2 * width), vt


def _attn_a_t_kernel(bias_ref, q_ref, kp_ref, kc_ref, vtp_ref, vtc_ref, o_ref, lse_ref, *, heads):
    n = pl.program_id(1)
    variant = jnp.minimum(n, 1)
    blk = ATTN_BLOCK
    row = lax.broadcasted_iota(I32, (LANES, 1), 0)
    first_head = row < HEAD_DIM
    lse_row = lax.broadcasted_iota(I32, (LANES, blk), 0)
    lse_t = jnp.zeros((LANES, blk), F32)
    group = 8
    for h0 in range(0, heads, group):
        hs = range(h0, h0 + group)
        vals, scores = [], []
        for h in hs:
            sl = slice(LANES * (h // 2), LANES * (h // 2 + 1))
            qm = _head_halves(q_ref[:, sl])[h % 2]
            keys = jnp.concatenate([kp_ref[:, sl], kc_ref[:, sl]], axis=0)
            vals.append(jnp.concatenate([vtp_ref[sl, :], vtc_ref[sl, :]], axis=1))
            scores.append(_dot_t(keys, qm))
        probs, scales = [], []
        for h, s in zip(hs, scores):
            s = s + bias_ref[variant, h]
            m = jnp.max(s, axis=0, keepdims=True)
            e = jnp.exp2(s - m)
            den = jnp.sum(e, axis=0, keepdims=True)
            probs.append(e.astype(BF16))
            scales.append(1.0 / den)
            lse_t = jnp.where(lse_row == h, (m + jnp.log2(den)) * (1.0 / LOG2_E), lse_t)
        outs = [_dot(v, p) * sc for v, p, sc in zip(vals, probs, scales)]
        for idx in range(0, group, 2):
            sl = slice(LANES * ((h0 + idx) // 2), LANES * ((h0 + idx) // 2 + 1))
            o_ref[:, sl] = jnp.where(first_head, outs[idx], outs[idx + 1]).T.astype(o_ref.dtype)
    lse_ref[...] = lse_t.T


def _attn_a_t_group(qk, vt, *, dilation, heads):
    n_streams, length, _ = qk.shape
    width = heads * HEAD_DIM
    nb = length // ATTN_BLOCK
    bias = jnp.swapaxes(_alibi_window_bias(dilation, heads) * LOG2_E, 2, 3)
    blk = (None, ATTN_BLOCK, width)
    vblk = (None, width, ATTN_BLOCK)
    o, lse = pl.pallas_call(
        functools.partial(_attn_a_t_kernel, heads=heads),
        grid=(n_streams, nb),
        in_specs=[
            pl.BlockSpec(bias.shape, lambda s, n: (0, 0, 0, 0)),
            pl.BlockSpec(blk, lambda s, n: (s, n, 0)),
            pl.BlockSpec(blk, lambda s, n: (s, jnp.maximum(n - 1, 0), 1)),
            pl.BlockSpec(blk, lambda s, n: (s, n, 1)),
            pl.BlockSpec(vblk, lambda s, n: (s, 0, jnp.maximum(n - 1, 0))),
            pl.BlockSpec(vblk, lambda s, n: (s, 0, n)),
        ],
        out_specs=[
            pl.BlockSpec(blk, lambda s, n: (s, n, 0)),
            pl.BlockSpec((None, ATTN_BLOCK, LANES), lambda s, n: (s, n, 0)),
        ],
        out_shape=[
            jax.ShapeDtypeStruct((n_streams, length, width), BF16),
            jax.ShapeDtypeStruct((n_streams, length, LANES), F32),
        ],
        compiler_params=_params(("parallel", "arbitrary")),
        name=f"dilated_attn_r{dilation}",
    )(bias, qk, qk, qk, vt, vt)
    return o, lse


def _merge_proj_kernel(o0_ref, o1_ref, o2_ref, l0_ref, l1_ref, l2_ref, e_ref, w_ref, r_ref, out_ref):
    l0, l1, l2 = l0_ref[...], l1_ref[...], l2_ref[...]
    mx = jnp.maximum(jnp.maximum(l0, l1), l2)
    e0, e1, e2 = jnp.exp(l0 - mx), jnp.exp(l1 - mx), jnp.exp(l2 - mx)
    inv = 1.0 / (e0 + e1 + e2)
    expand = e_ref[...]
    acc = None
    for e, o_ref in ((e0, o0_ref), (e1, o1_ref), (e2, o2_ref)):
        hi, lo = _split_bf16(e * inv)
        w_full = _dot(hi, expand) + _dot(lo, expand)
        term = w_full * o_ref[...].astype(F32)
        acc = term if acc is None else acc + term
    out_ref[...] = r_ref[...] + _dot(acc.astype(BF16), w_ref[...])


def _merge_proj(outs, lses, w_out, res, *, heads, tm):
    T, width = outs[0].shape
    N = w_out.shape[1]
    tm = min(tm, T)
    expand = np.zeros((LANES, width), np.float32)
    for h in range(heads):
        expand[h, h * HEAD_DIM:(h + 1) * HEAD_DIM] = 1.0
    expand = jnp.asarray(expand, BF16)
    row = lambda i: (i, 0)
    fixed = lambda i: (0, 0)
    return pl.pallas_call(
        _merge_proj_kernel,
        grid=(T // tm,),
        in_specs=[pl.BlockSpec((tm, width), row)] * 3 + [pl.BlockSpec((tm, LANES), row)] * 3 + [
            pl.BlockSpec((LANES, width), fixed),
            pl.BlockSpec((width, N), fixed),
            pl.BlockSpec((tm, N), row),
        ],
        out_specs=pl.BlockSpec((tm, N), row),
        out_shape=jax.ShapeDtypeStruct((T, N), F32),
        compiler_params=_params(("parallel",)),
        name="merge_out_proj",
    )(*outs, *lses, expand, w_out, res)


def _proj_qk_vt_kernel(x_ref, g_ref, wqk_ref, wvt_ref, qk_ref, vt_ref, *, q_width, q_scale):
    xn = (_rms(x_ref[...]) * g_ref[...]).astype(BF16)
    qk = _dot(xn, wqk_ref[...])
    qk_ref[:, :q_width] = (qk[:, :q_width] * q_scale).astype(BF16)
    qk_ref[:, q_width:] = qk[:, q_width:].astype(BF16)
    vt_ref[...] = _dot_t(wvt_ref[...], xn).astype(BF16)


def _proj_qk_vt(h, g, w_in, *, batch, seq, tm, q_scale):
    T, D = h.shape
    width = w_in.shape[1] // 3
    tm = min(tm, seq)
    per_seq = seq // tm
    w = w_in.astype(BF16)
    wqk = w[:, :2 * width]
    wvt = w[:, 2 * width:].T
    fixed = lambda i: (0, 0)
    qk, vt = pl.pallas_call(
        functools.partial(_proj_qk_vt_kernel, q_width=width, q_scale=q_scale),
        grid=(T // tm,),
        in_specs=[
            pl.BlockSpec((tm, D), lambda i: (i, 0)),
            pl.BlockSpec((1, D), fixed),
            pl.BlockSpec((D, 2 * width), fixed),
            pl.BlockSpec((width, D), fixed),
        ],
        out_specs=[
            pl.BlockSpec((tm, 2 * width), lambda i: (i, 0)),
            pl.BlockSpec((None, width, tm), lambda i: (i // per_seq, 0, i % per_seq)),
        ],
        out_shape=[
            jax.ShapeDtypeStruct((T, 2 * width), BF16),
            jax.ShapeDtypeStruct((batch, width, seq), BF16),
        ],
        compiler_params=_params(("parallel",)),
        name="norm_proj_qk_vt",
    )(h, g.reshape(1, D), wqk, wvt)
    return qk.reshape(batch, seq, 2 * width), vt


def _head_halves(q):
    lane = lax.broadcasted_iota(I32, (1, LANES), 1)
    first = lane < HEAD_DIM
    zero = jnp.zeros_like(q)
    return jnp.where(first, q, zero), jnp.where(first, zero, q)


def _attn_b_kernel(slope_ref, q_ref, k_ref, vt_ref, lq1_ref, lk1_ref, lq2_ref, lk2_ref, sub_ref,
                   o_ref, *, t, lambda_init):
    h = pl.program_id(1)
    i = pl.program_id(2)
    slope = slope_ref[h]
    n_col = q_ref.shape[0] // t
    qs = [_head_halves(q_ref[col * t:(col + 1) * t, :]) for col in range(n_col)]
    rel = lax.broadcasted_iota(I32, (t, t), 1) - lax.broadcasted_iota(I32, (t, t), 0)
    rel_bias = slope * rel.astype(F32)

    def step(js, carry, block_modes):
        new = list(carry)
        plans = [[(col, c) for col, mode in enumerate(modes) if mode is not None for c in range(2)]
                 for modes in block_modes]
        starts = [pl.multiple_of(j * t, t) for j in js]
        scores = [[_dot_t(k_ref[pl.ds(st, t), :], qs[col][c]) for col, c in chains]
                  for st, chains in zip(starts, plans)]
        for j, st, chains, modes, block_scores in zip(js, starts, plans, block_modes, scores):
            vt = vt_ref[:, pl.ds(st, t)]
            probs, alphas = [], []
            for (col, c), s in zip(chains, block_scores):
                base = 3 * (2 * col + c)
                m, l = new[base], new[base + 1]
                shift = slope * ((i * n_col + col - j) * t).astype(F32)
                s = s - rel_bias
                if modes[col]:
                    s = jnp.where(rel >= 0, s, -jnp.inf)
                m_new = jnp.maximum(m, jnp.max(s, axis=0, keepdims=True) - shift)
                alpha = jnp.exp2(m - m_new)
                p = jnp.exp2(s - (m_new + shift))
                new[base] = m_new
                new[base + 1] = alpha * l + jnp.sum(p, axis=0, keepdims=True)
                probs.append(p.astype(BF16))
                alphas.append(alpha)
            for (col, c), p, alpha in zip(chains, probs, alphas):
                base = 3 * (2 * col + c)
                new[base + 2] = alpha * new[base + 2] + _dot(vt, p)
        return tuple(new)

    init = (jnp.full((1, t), -jnp.inf, F32), jnp.zeros((1, t), F32), jnp.zeros((LANES, t), F32)) * (2 * n_col)
    full = [(False,) * n_col] * n_col
    wide = 2 * n_col
    carry = lax.fori_loop(0, i // 2, lambda jj, c: step([wide * jj + b for b in range(wide)], c, full * 2), init)
    carry = lax.cond(i % 2 == 1,
                     lambda c: step([n_col * (i - 1) + b for b in range(n_col)], c, full),
                     lambda c: c, carry)
    diag = [tuple(None if col < d else (col == d) for col in range(n_col)) for d in range(n_col)]
    carry = step([n_col * i + d for d in range(n_col)], carry, diag)
    lam = (jnp.exp(jnp.sum(lq1_ref[...] * lk1_ref[...], axis=-1, keepdims=True))
           - jnp.exp(jnp.sum(lq2_ref[...] * lk2_ref[...], axis=-1, keepdims=True)) + lambda_init)
    for col in range(n_col):
        _, l1, a1, _, l2, a2 = carry[6 * col:6 * col + 6]
        o = a1 * (1.0 / l1) - lam * (a2 * (1.0 / l2))
        o = o * lax.rsqrt(jnp.mean(o * o, axis=0, keepdims=True) + RMS_EPS)
        o = o * (sub_ref[...] * (1.0 - lambda_init))
        o_ref[col * t:(col + 1) * t, :] = o.T.astype(o_ref.dtype)


def _attn_b(qk, vt, lq1, lk1, lq2, lk2, subln, *, batch, seq, heads, lambda_init, t, tq):
    t = min(t, seq)
    tq = min(tq, seq)
    width = 2 * HEAD_DIM
    slopes = jnp.asarray(LOG2_E * 2.0 ** (-8.0 * np.arange(1, heads + 1) / heads), F32)
    vec = lambda a: a.reshape(1, -1).astype(F32)
    small = lambda n: pl.BlockSpec((1, n), lambda b, h, i, s: (0, 0))
    return pl.pallas_call(
        functools.partial(_attn_b_kernel, t=t, lambda_init=lambda_init),
        grid_spec=pltpu.PrefetchScalarGridSpec(
            num_scalar_prefetch=1,
            grid=(batch, heads, seq // tq),
            in_specs=[
                pl.BlockSpec((None, tq, width), lambda b, h, i, s: (b, i, h)),
                pl.BlockSpec((None, seq, width), lambda b, h, i, s: (b, 0, heads + h)),
                pl.BlockSpec((None, width, seq), lambda b, h, i, s: (b, h, 0)),
                small(HEAD_DIM), small(HEAD_DIM), small(HEAD_DIM), small(HEAD_DIM),
                pl.BlockSpec((width, 1), lambda b, h, i, s: (0, 0)),
            ],
            out_specs=pl.BlockSpec((None, tq, width), lambda b, h, i, s: (b, i, h)),
        ),
        out_shape=jax.ShapeDtypeStruct((batch, seq, heads * width), BF16),
        compiler_params=_params(("parallel", "parallel", "arbitrary")),
        name="diff_attn",
    )(slopes, qk, qk, vt, vec(lq1), vec(lk1), vec(lq2), vec(lk2), subln.reshape(width, 1).astype(F32))


def _attn_c_kernel(q_ref, k_ref, vt_ref, later_ref, o_ref):
    i = pl.program_id(2)
    tq = q_ref.shape[0]
    kw = LANES
    n_sub = tq // kw
    qs = _head_halves(q_ref[...])
    later = later_ref[...]
    rel = lax.broadcasted_iota(I32, (kw, tq), 1) - lax.broadcasted_iota(I32, (kw, tq), 0)

    def blocks(kbs, carry, diag=None):
        diagonal = diag is not None
        units = []
        for idx, kb in enumerate(kbs):
            lo = diag[idx] * kw if diagonal else 0
            start = pl.multiple_of(kb * kw, kw)
            k = k_ref[pl.ds(start, kw), :]
            vt = vt_ref[:, pl.ds(start, kw)]
            causal = (rel[:, lo:] + (i * tq - kb * kw)) > 0 if diagonal else None
            for head in range(2):
                units.append(dict(head=head, vt=vt, causal=causal, lo=lo,
                                  z=_dot_t(k, qs[head][lo:])))
        for u in units:
            z = u["z"]
            softplus = jnp.maximum(z, 0.0) + jnp.log2(1.0 + jnp.exp2(-jnp.abs(z)))
            rest = jnp.where(u["causal"], softplus, 0.0) if diagonal else softplus
            u["logit"] = z - softplus
            u["rest"] = rest
            u["split"] = _split_bf16(rest)
        for u in units:
            hi, lo = u["split"]
            u["after"] = _dot(later, hi) + _dot(later, lo)
        def add_from(full, lo, delta):
            if lo == 0:
                return full + delta
            return jnp.concatenate([full[:, :lo], full[:, lo:] + delta], axis=1)

        c = [carry[0], carry[2]]
        weights = []
        for u in units:
            a = jnp.exp2(u["logit"] - u["after"] - c[u["head"]][:, u["lo"]:])
            if diagonal:
                a = jnp.where(u["causal"], a, 0.0)
            weights.append(a.astype(BF16))
            c[u["head"]] = add_from(c[u["head"]], u["lo"], jnp.sum(u["rest"], axis=0, keepdims=True))
        acc = [carry[1], carry[3]]
        for u, a in zip(units, weights):
            acc[u["head"]] = add_from(acc[u["head"]], u["lo"], _dot(u["vt"], a))
        return c[0], acc[0], c[1], acc[1]

    def alive(carry):
        return (jnp.min(jnp.minimum(carry[0], carry[2])) < -F32_EXP2_ZERO).astype(I32)

    zc = jnp.zeros((1, tq), F32)
    za = jnp.zeros((LANES, tq), F32)
    order = list(reversed(range(n_sub)))
    carry = blocks([i * n_sub + d for d in order], (zc, za, zc, za), diag=order)

    def cond(state):
        return jnp.logical_and(state[0] >= 0, state[1] > 0)

    def body(state):
        new = blocks([state[0], state[0] - 1], state[2:])
        return (state[0] - 2, alive(new)) + new

    out = lax.while_loop(cond, body, (i * n_sub - 1, alive(carry)) + carry)
    row = lax.broadcasted_iota(I32, (LANES, 1), 0)
    o = jnp.where(row < HEAD_DIM, out[3], out[5])
    o_ref[...] = o.T.astype(o_ref.dtype)


def _attn_c(qk, vt, *, batch, seq, heads, tq):
    tq = min(tq, seq)
    assert seq % tq == 0 and (tq // LANES) % 2 == 0
    pairs = heads // 2
    later = jnp.asarray(np.triu(np.ones((LANES, LANES), np.float32), 1), BF16)
    return pl.pallas_call(
        _attn_c_kernel,
        grid=(batch, pairs, seq // tq),
        in_specs=[
            pl.BlockSpec((None, tq, LANES), lambda b, h, i: (b, i, h)),
            pl.BlockSpec((None, seq, LANES), lambda b, h, i: (b, 0, pairs + h)),
            pl.BlockSpec((None, LANES, seq), lambda b, h, i: (b, h, 0)),
            pl.BlockSpec((LANES, LANES), lambda b, h, i: (0, 0)),
        ],
        out_specs=pl.BlockSpec((None, tq, LANES), lambda b, h, i: (b, i, h)),
        out_shape=jax.ShapeDtypeStruct((batch, seq, heads * HEAD_DIM), BF16),
        compiler_params=_params(("parallel", "parallel", "arbitrary")),
        name="stick_breaking_attn",
    )(qk, qk, vt, later)


def _router_kernel(x_ref, g_ref, whi_ref, wlo_ref, b_ref, u_ref, info_ref, cnt_ref, run_ref,
                   *, groups, per_group):
    G, E = groups, per_group
    tm = x_ref.shape[0]
    ne = G * E

    @pl.when(pl.program_id(0) == 0)
    def _():
        run_ref[...] = jnp.zeros_like(run_ref)

    xn = _rms(x_ref[...]) * g_ref[...]
    hi, lo = _split_bf16(xn)
    whi = whi_ref[...]
    lt = _dot_t(whi, hi) + _dot_t(whi, lo) + _dot_t(wlo_ref[...], hi) + b_ref[...]
    sub = lax.broadcasted_iota(I32, (E, tm), 0)
    coarse = lt[0:G]
    cmax = jnp.max(coarse, axis=0, keepdims=True)
    g_prob = 1.0 / jnp.sum(jnp.exp(coarse - cmax), axis=0, keepdims=True)
    g_idx = jnp.min(jnp.where(coarse == cmax, sub, G), axis=0, keepdims=True)
    fine = jnp.zeros((E, tm), F32)
    for grp in range(G):
        fine = jnp.where(g_idx == grp, lt[G + grp * E:G + (grp + 1) * E], fine)
    fmax = jnp.max(fine, axis=0, keepdims=True)
    fsum = jnp.sum(jnp.exp(fine - fmax), axis=0, keepdims=True)
    i1 = jnp.min(jnp.where(fine == fmax, sub, E), axis=0, keepdims=True)
    rest = jnp.where(sub == i1, -jnp.inf, fine)
    m2 = jnp.max(rest, axis=0, keepdims=True)
    i2 = jnp.min(jnp.where(rest == m2, sub, E), axis=0, keepdims=True)
    p1 = 1.0 / fsum
    p2 = jnp.exp(m2 - fmax) / fsum
    norm = p1 + p2
    gate1 = g_prob * (p1 / norm)
    gate2 = g_prob * (p2 / norm)
    e1 = g_idx * E + i1
    e2 = g_idx * E + i2
    ex = lax.broadcasted_iota(I32, (ne, tm), 0)
    oh1 = (ex == e1).astype(F32)
    oh2 = (ex == e2).astype(F32)
    both = oh1 + oh2
    earlier = _dot(both.astype(BF16), u_ref[...]) + run_ref[:, 0:1]
    r1 = jnp.sum(oh1 * earlier, axis=0, keepdims=True)
    r2 = jnp.sum(oh2 * earlier, axis=0, keepdims=True)
    total = run_ref[...] + jnp.sum(both, axis=1, keepdims=True)
    run_ref[...] = total
    cnt_ref[...] = total
    row = lax.broadcasted_iota(I32, (SUBLANES, tm), 0)
    fields = (e1.astype(F32), e2.astype(F32), r1, r2, gate1, gate2)
    info = jnp.zeros((SUBLANES, tm), F32)
    for idx, f in enumerate(fields):
        info = jnp.where(row == idx, f, info)
    info_ref[...] = info


def _router(h, g, w_coarse, b_coarse, w_fine, b_fine, *, tm):
    T, D = h.shape
    tm = min(tm, T)
    G, ne = w_coarse.shape[1], w_fine.shape[1]
    pad = LANES - G - ne
    wt = jnp.pad(jnp.concatenate([w_coarse, w_fine], axis=1), ((0, 0), (0, pad))).T
    b = jnp.pad(jnp.concatenate([b_coarse, b_fine]), (0, pad)).reshape(LANES, 1)
    whi, wlo = _split_bf16(wt)
    before = jnp.asarray(np.triu(np.ones((tm, tm), np.float32), 1), BF16)
    fixed = lambda i: (0, 0)
    return pl.pallas_call(
        functools.partial(_router_kernel, groups=G, per_group=ne // G),
        grid=(T // tm,),
        in_specs=[
            pl.BlockSpec((tm, D), lambda i: (i, 0)),
            pl.BlockSpec((1, D), fixed),
            pl.BlockSpec((LANES, D), fixed),
            pl.BlockSpec((LANES, D), fixed),
            pl.BlockSpec((LANES, 1), fixed),
            pl.BlockSpec((tm, tm), fixed),
        ],
        out_specs=[pl.BlockSpec((SUBLANES, tm), lambda i: (0, i)), pl.BlockSpec((ne, LANES), fixed)],
        out_shape=[jax.ShapeDtypeStruct((SUBLANES, T), F32), jax.ShapeDtypeStruct((ne, LANES), F32)],
        scratch_shapes=[pltpu.VMEM((ne, LANES), F32)],
        compiler_params=_params(("arbitrary",)),
        name="moe_router",
    )(h, g.reshape(1, D), whi, wlo, b, before)


def _plan(info, counts, *, tile):
    T = info.shape[1]
    ne = counts.shape[0]
    C = FFN_CHUNK
    experts = info[0:2].astype(I32)
    ranks = info[2:4].astype(I32)
    gates = info[4:6]
    cnt = counts[:, 0].astype(I32)
    padded = (cnt + C - 1) // C * C
    pad_end = jnp.cumsum(padded)
    pad_start = pad_end - padded
    ids = jnp.arange(ne, dtype=I32)
    start_of = jnp.sum(jnp.where(experts[:, :, None] == ids, pad_start, 0), axis=-1)
    dest = ranks + start_of
    n_chunks = (MOE_TOP_K * T + ne * (C - 1) + C - 1) // C
    n_used = (pad_end[-1] // C).astype(I32)
    first_row = jnp.minimum(jnp.arange(n_chunks, dtype=I32), n_used - 1) * C
    chunk_expert = jnp.minimum(jnp.sum(pad_end[None, :] <= first_row[:, None], axis=1), ne - 1).astype(I32)
    dest_tiles = dest.reshape(MOE_TOP_K, T // tile, tile).transpose(1, 0, 2).reshape(T // tile, 1, MOE_TOP_K * tile)
    return dest_tiles, gates.T, chunk_expert, n_used.reshape(1), n_chunks


def _dispatch_kernel(dst_ref, h_ref, g_ref, xs_in, xs_out, xbuf, sem):
    del xs_in
    i = pl.program_id(0)
    n = pl.num_programs(0)
    tm, D = h_ref.shape
    slot = i % 2

    def wait(s):
        for _ in range(MOE_TOP_K):
            pltpu.make_async_copy(xbuf.at[s], xs_out.at[pl.ds(0, tm)], sem.at[s]).wait()

    @pl.when(i >= 2)
    def _():
        wait(slot)

    xn = _rms(h_ref[...]) * g_ref[...]
    xbuf[slot] = _pack_bf16_pair(xn[:, :D // 2], xn[:, D // 2:])

    def issue(r, _):
        for k in range(MOE_TOP_K):
            pltpu.make_async_copy(xbuf.at[slot, pl.ds(r, 1)], xs_out.at[pl.ds(dst_ref[0, 0, k * tm + r], 1)],
                                  sem.at[slot]).start()
        return 0

    lax.fori_loop(0, tm, issue, 0, unroll=True)

    @pl.when(i == n - 1)
    def _():
        wait(slot)

        @pl.when(i >= 1)
        def _():
            wait(1 - slot)


def _dispatch(h, g, dest_tiles, n_rows, *, tm):
    T, D = h.shape
    xs0 = jnp.zeros((n_rows, D // 2), U32)
    return pl.pallas_call(
        _dispatch_kernel,
        grid=(T // tm,),
        in_specs=[
            pl.BlockSpec((1, 1, MOE_TOP_K * tm), lambda i: (i, 0, 0), memory_space=pltpu.SMEM),
            pl.BlockSpec((tm, D), lambda i: (i, 0)),
            pl.BlockSpec((1, D), lambda i: (0, 0)),
            pl.BlockSpec(memory_space=pl.ANY),
        ],
        out_specs=pl.BlockSpec(memory_space=pl.ANY),
        out_shape=jax.ShapeDtypeStruct((n_rows, D // 2), U32),
        scratch_shapes=[pltpu.VMEM((2, tm, D // 2), U32), pltpu.SemaphoreType.DMA((2,))],
        input_output_aliases={3: 0},
        compiler_params=_params(("arbitrary",)),
        name="moe_dispatch",
    )(dest_tiles, h, g.reshape(1, D), xs0)


def _ffn_kernel(ce_ref, nu_ref, x_ref, wgu_ref, wdn_ref, y_ref, wgu_bf, wdn_bf, *, d_expert):
    c = pl.program_id(0)
    n_used = nu_ref[0]

    @pl.when(c < n_used)
    def _():
        @pl.when(jnp.logical_or(c == 0, ce_ref[c] != ce_ref[jnp.maximum(c - 1, 0)]))
        def _():
            wgu_bf[...] = wgu_ref[...].astype(BF16)
            wdn_bf[...] = wdn_ref[...].astype(BF16)

        half = x_ref.shape[1]
        xa, xb = _unpack_bf16_pair(x_ref[...])
        n_blk = 2
        w = d_expert // n_blk
        proj = lambda lo: _dot(xa, wgu_bf[:half, lo:lo + w]) + _dot(xb, wgu_bf[half:, lo:lo + w])
        gates = [proj(b * w) for b in range(n_blk)]
        ups = [proj(d_expert + b * w) for b in range(n_blk)]
        acts = [((g * jax.nn.sigmoid(g)) * u).astype(BF16) for g, u in zip(gates, ups)]
        y = _dot(acts[0], wdn_bf[:w])
        for b in range(1, n_blk):
            y = y + _dot(acts[b], wdn_bf[b * w:(b + 1) * w])
        y_ref[...] = y

    @pl.when(c >= n_used)
    def _():
        y_ref[...] = jnp.zeros_like(y_ref)


def _ffn(xs, chunk_expert, n_used, w_gate_up, w_down, layer):
    n_rows, half = xs.shape
    C = FFN_CHUNK
    n_chunks = n_rows // C
    D, F2 = w_gate_up.shape[2:]
    d_expert = w_down.shape[2]
    return pl.pallas_call(
        functools.partial(_ffn_kernel, d_expert=d_expert),
        grid_spec=pltpu.PrefetchScalarGridSpec(
            num_scalar_prefetch=2,
            grid=(n_chunks,),
            in_specs=[
                pl.BlockSpec((C, half), lambda c, ce, nu: (jnp.minimum(c, nu[0] - 1), 0)),
                pl.BlockSpec((None, None, D, F2), lambda c, ce, nu: (layer, ce[c], 0, 0)),
                pl.BlockSpec((None, None, d_expert, D), lambda c, ce, nu: (layer, ce[c], 0, 0)),
            ],
            out_specs=pl.BlockSpec((C, D), lambda c, ce, nu: (c, 0)),
            scratch_shapes=[pltpu.VMEM((D, F2), BF16), pltpu.VMEM((d_expert, D), BF16)],
        ),
        out_shape=jax.ShapeDtypeStruct((n_rows, D), F32),
        compiler_params=_params(("arbitrary",)),
        name="moe_expert_ffn",
    )(chunk_expert, n_used, xs, w_gate_up, w_down)


def _combine_ple_kernel(dst_ref, dstn_ref, h_ref, gt_ref, p_ref, g_ref, wp_ref, wg_ref, gf_ref, y_hbm,
                        o_ref, ybuf, sem, *, final):
    i = pl.program_id(0)
    n = pl.num_programs(0)
    tm, D = h_ref.shape
    rows = MOE_TOP_K * tm
    slot = i % 2

    def gather(dref, s):
        def issue(r, _):
            for k in range(MOE_TOP_K):
                pltpu.make_async_copy(y_hbm.at[pl.ds(dref[0, 0, k * tm + r], 1)],
                                      ybuf.at[s, pl.ds(k * tm + r, 1)], sem.at[s]).start()
            return 0
        lax.fori_loop(0, tm, issue, 0, unroll=True)

    @pl.when(i == 0)
    def _():
        gather(dst_ref, 0)

    def wait(s):
        pltpu.make_async_copy(y_hbm.at[pl.ds(0, rows)], ybuf.at[s], sem.at[s]).wait()

    wait(slot)
    gather(dstn_ref, 1 - slot)
    gt = gt_ref[...]
    h = h_ref[...] + (gt[:, 0:1] * ybuf[slot, :tm] + gt[:, 1:2] * ybuf[slot, tm:])
    xn = (_rms(h) * g_ref[...]).astype(BF16)
    gate = jax.nn.sigmoid(_dot(xn, wg_ref[...]))
    out = h + _dot(p_ref[...].astype(BF16), wp_ref[...]) * gate
    if final:
        out = _rms(out) * gf_ref[...]
    o_ref[...] = out

    @pl.when(i == n - 1)
    def _():
        wait(1 - slot)


def _combine_ple(h, y, dest_tiles, gates, p, layer, g, w_proj, w_gate, g_final, *, final, tm):
    T, D = h.shape
    Pd = p.shape[2]
    n_tiles = T // tm
    row = lambda i: (i, 0)
    fixed = lambda i: (0, 0)
    smem = functools.partial(pl.BlockSpec, (1, 1, MOE_TOP_K * tm), memory_space=pltpu.SMEM)
    return pl.pallas_call(
        functools.partial(_combine_ple_kernel, final=final),
        grid=(n_tiles,),
        in_specs=[
            smem(lambda i: (i, 0, 0)),
            smem(lambda i: (jnp.minimum(i + 1, n_tiles - 1), 0, 0)),
            pl.BlockSpec((tm, D), row),
            pl.BlockSpec((tm, MOE_TOP_K), row),
            pl.BlockSpec((None, tm, Pd), lambda i: (layer, i, 0)),
            pl.BlockSpec((1, D), fixed),
            pl.BlockSpec((Pd, D), fixed),
            pl.BlockSpec((D, D), fixed),
            pl.BlockSpec((1, D), fixed),
            pl.BlockSpec(memory_space=pl.ANY),
        ],
        out_specs=pl.BlockSpec((tm, D), row),
        out_shape=jax.ShapeDtypeStruct((T, D), F32),
        scratch_shapes=[pltpu.VMEM((2, MOE_TOP_K * tm, D), F32), pltpu.SemaphoreType.DMA((2,))],
        compiler_params=_params(("arbitrary",)),
        name="moe_combine_ple",
    )(dest_tiles, dest_tiles, h, gates, p, g.reshape(1, D), w_proj, w_gate, g_final.reshape(1, D), y)


def _moe_ple(h, p, layer, g_ffn, w_coarse, b_coarse, w_fine, b_fine, w_gate_up, w_down, g_ple, w_proj, w_gate,
             g_final, *, final):
    tile = min(512, h.shape[0])
    info, counts = _router(h, g_ffn, w_coarse, b_coarse, w_fine, b_fine, tm=min(512, h.shape[0]))
    dest_tiles, gates, chunk_expert, n_used, n_chunks = _plan(info, counts, tile=tile)
    xs = _dispatch(h, g_ffn, dest_tiles, n_chunks * FFN_CHUNK, tm=tile)
    y = _ffn(xs, chunk_expert, n_used, w_gate_up, w_down, layer)
    return _combine_ple(h, y, dest_tiles, gates, p, layer, g_ple, w_proj.astype(BF16), w_gate.astype(BF16),
                        g_final, final=final, tm=tile)


def _mixer_a(h, g, w_in, w_out, *, batch, seq):
    T, D = h.shape
    width = A_HEADS * HEAD_DIM
    n_groups = len(A_PATTERNS)
    w = w_in.astype(BF16)
    wvt = jnp.swapaxes(w.reshape(D, n_groups, 3, width)[:, :, 2], 0, 1).swapaxes(1, 2)
    xn = _norm(h, g, tm=1024)
    outs, lses = [], []
    for grp, (window, r) in enumerate(A_PATTERNS):
        assert window // r == ATTN_BLOCK and seq % (r * ATTN_BLOCK) == 0
        L = seq // r
        to_streams = lambda a: a.reshape(batch, L, r, -1).transpose(0, 2, 1, 3).reshape(T, -1)
        from_streams = lambda a: a.reshape(batch, r, L, -1).transpose(0, 2, 1, 3).reshape(T, -1)
        xs = xn if r == 1 else to_streams(xn)
        qk, vt = _proj_a(xs, w, wvt, group=grp, streams=batch * r, length=L, width=width, tm=512,
                         q_scale=LOG2_E / math.sqrt(HEAD_DIM))
        o, lse = _attn_a_t_group(qk, vt, dilation=r, heads=A_HEADS)
        outs.append(from_streams(o))
        lses.append(from_streams(lse))
    return _merge_proj(outs, lses, w_out.astype(BF16), h, heads=A_HEADS, tm=512)


def _mixer_b(h, g, w_in, w_out, lq1, lk1, lq2, lk2, subln, lambda_init, *, batch, seq):
    D = h.shape[1]
    heads = D // (2 * HEAD_DIM)
    qk, vt = _proj_qk_vt(h, g, w_in, batch=batch, seq=seq, tm=512, q_scale=LOG2_E / math.sqrt(HEAD_DIM))
    o = _attn_b(qk, vt, lq1, lk1, lq2, lk2, subln, batch=batch, seq=seq, heads=heads,
                lambda_init=lambda_init, t=256, tq=512)
    return _mm_res(o.reshape(batch * seq, -1), w_out.astype(BF16), h, tm=512)


def _mixer_c(h, g, w_in, w_out, *, batch, seq):
    D = h.shape[1]
    heads = D // HEAD_DIM
    qk, vt = _proj_qk_vt(h, g, w_in, batch=batch, seq=seq, tm=512, q_scale=LOG2_E / math.sqrt(HEAD_DIM))
    o = _attn_c(qk, vt, batch=batch, seq=seq, heads=heads, tq=512)
    return _mm_res(o.reshape(batch * seq, -1), w_out.astype(BF16), h, tm=512)


def kernel(x, p, norm_mix, norm_ffn, norm_ple, norm_final, a_w_in, a_w_out, b_w_in, b_w_out,
           b_lambda_q1, b_lambda_k1, b_lambda_q2, b_lambda_k2, b_subln, c_w_in, c_w_out,
           moe_w_coarse, moe_b_coarse, moe_w_fine, moe_b_fine, moe_w_gate_up, moe_w_down,
           ple_w_proj, ple_w_gate):
    batch, seq, D = x.shape
    depth = p.shape[0]
    T = batch * seq
    h = x.reshape(T, D)
    p_rows = p.reshape(depth, T, -1)
    for i in range(depth):
        kind = i % N_MIXERS
        j = i // N_MIXERS
        if kind == 0:
            h = _mixer_a(h, norm_mix[i], a_w_in[j], a_w_out[j], batch=batch, seq=seq)
        elif kind == 1:
            lambda_init = 0.8 - 0.6 * math.exp(-0.3 * i)
            h = _mixer_b(h, norm_mix[i], b_w_in[j], b_w_out[j], b_lambda_q1[j], b_lambda_k1[j],
                         b_lambda_q2[j], b_lambda_k2[j], b_subln[j], lambda_init, batch=batch, seq=seq)
        else:
            h = _mixer_c(h, norm_mix[i], c_w_in[j], c_w_out[j], batch=batch, seq=seq)
        h = _moe_ple(h, p_rows, i, norm_ffn[i], moe_w_coarse[i], moe_b_coarse[i], moe_w_fine[i],
                     moe_b_fine[i], moe_w_gate_up, moe_w_down, norm_ple[i], ple_w_proj[i],
                     ple_w_gate[i], norm_final, final=(i == depth - 1))
    return h.reshape(batch, seq, D)
```

```python
import functools
import math

import jax
import jax.numpy as jnp
import numpy as np
from jax import lax
from jax.experimental import pallas as pl
from jax.experimental.pallas import tpu as pltpu

F32 = jnp.float32
BF16 = jnp.bfloat16
U32 = jnp.uint32
I32 = jnp.int32

HEAD_DIM = 64
N_MIXERS = 3
ATTN_BLOCK = 128
RMS_EPS = 1e-6
A_HEADS = 16
A_PATTERNS = ((128, 1), (512, 4), (2048, 16))
MOE_GROUPS = 8
MOE_EXPERTS_PER_GROUP = 8
MOE_TOP_K = 2
FFN_CHUNK = 256
LANES = 128
SUBLANES = 8
F32_EXP2_ZERO = -151.0
LOG2_E = math.log2(math.e)
VMEM_LIMIT = 48 * 1024 * 1024


def _params(sem, vmem=VMEM_LIMIT):
    return pltpu.CompilerParams(dimension_semantics=sem, vmem_limit_bytes=vmem)


def _rms(x):
    return x * lax.rsqrt(jnp.mean(x * x, axis=-1, keepdims=True) + RMS_EPS)


def _dot_t(a, b):
    return lax.dot_general(a, b, (((1,), (1,)), ((), ())), preferred_element_type=F32)


def _dot(a, b):
    return jnp.dot(a, b, preferred_element_type=F32)


def _split_bf16(x):
    hi = x.astype(BF16)
    lo = (x - hi.astype(F32)).astype(BF16)
    return hi, lo


def _bits(x):
    return lax.bitcast_convert_type(x, U32)


def _pack_bf16_pair(a, b):
    a = a.astype(BF16).astype(F32)
    b = b.astype(BF16).astype(F32)
    return (_bits(a) >> 16) | _bits(b)


def _unpack_bf16_pair(w):
    a = lax.bitcast_convert_type(w << 16, F32).astype(BF16)
    b = lax.bitcast_convert_type(w & jnp.uint32(0xFFFF0000), F32).astype(BF16)
    return a, b


def _norm_kernel(x_ref, g_ref, o_ref):
    o_ref[...] = (_rms(x_ref[...]) * g_ref[...]).astype(o_ref.dtype)


def _norm(x, g, *, tm):
    T, D = x.shape
    tm = min(tm, T)
    return pl.pallas_call(
        _norm_kernel,
        grid=(T // tm,),
        in_specs=[pl.BlockSpec((tm, D), lambda i: (i, 0)), pl.BlockSpec((1, D), lambda i: (0, 0))],
        out_specs=pl.BlockSpec((tm, D), lambda i: (i, 0)),
        out_shape=jax.ShapeDtypeStruct((T, D), BF16),
        compiler_params=_params(("parallel",)),
        name="rmsnorm",
    )(x, g.reshape(1, D))


def _mm_res_kernel(a_ref, w_ref, r_ref, o_ref):
    o_ref[...] = r_ref[...] + _dot(a_ref[...], w_ref[...])


def _mm_res(a, w, res, *, tm):
    T, K = a.shape
    N = w.shape[1]
    tm = min(tm, T)
    return pl.pallas_call(
        _mm_res_kernel,
        grid=(T // tm,),
        in_specs=[
            pl.BlockSpec((tm, K), lambda i: (i, 0)),
            pl.BlockSpec((K, N), lambda i: (0, 0)),
            pl.BlockSpec((tm, N), lambda i: (i, 0)),
        ],
        out_specs=pl.BlockSpec((tm, N), lambda i: (i, 0)),
        out_shape=jax.ShapeDtypeStruct((T, N), F32),
        compiler_params=_params(("parallel",)),
        name="out_proj",
    )(a, w, res)


def _alibi_window_bias(dilation, heads):
    blk = ATTN_BLOCK
    a = np.arange(blk)[:, None]
    b = np.arange(2 * blk)[None, :]
    delta = blk + a - b
    valid = (delta >= 0) & (delta <= blk)
    slopes = 2.0 ** (-8.0 * np.arange(1, heads + 1) / heads)
    bias = -slopes[:, None, None] * (delta * dilation).astype(np.float64)[None]
    bias = np.where(valid[None], bias, -np.inf)
    first = bias.copy()
    first[:, :, :blk] = -np.inf
    return jnp.asarray(np.stack([first, bias]), F32)


def _proj_a_kernel(a_ref, wq_ref, wk_ref, wvt_ref, qk_ref, vt_ref, *, q_scale):
    a = a_ref[...]
    width = wq_ref.shape[1]
    qk_ref[:, :width] = (_dot(a, wq_ref[...]) * q_scale).astype(BF16)
    qk_ref[:, width:] = _dot(a, wk_ref[...]).astype(BF16)
    vt_ref[...] = _dot_t(wvt_ref[...], a).astype(BF16)


def _proj_a(a, w, wvt, *, group, length, width, tm, q_scale):
    T, D = a.shape
    tm = min(tm, length)
    per_stream = length // tm
    first = group * 3
    qk, vt = pl.pallas_call(
        functools.partial(_proj_a_kernel, q_scale=q_scale),
        grid=(T // tm,),
        in_specs=[
            pl.BlockSpec((tm, D), lambda i: (i, 0)),
            pl.BlockSpec((D, width), lambda i: (0, first)),
            pl.BlockSpec((D, width), lambda i: (0, first + 1)),
            pl.BlockSpec((None, width, D), lambda i: (group, 0, 0)),
        ],
        out_specs=[
            pl.BlockSpec((tm, 2 * width), lambda i: (i, 0)),
            pl.BlockSpec((None, width, tm), lambda i: (i // per_stream, 0, i % per_stream)),
        ],
        out_shape=[
            jax.ShapeDtypeStruct((T, 2 * width), BF16),
            jax.ShapeDtypeStruct((T // length, width, length), BF16),
        ],
        compiler_params=_params(("parallel",)),
        name="proj_qk_vt",
    )(a, w, w, wvt)
    return qk.reshape(T // length, length, 2 * width), vt


def _attn_a_t_kernel(bias_ref, q_ref, kp_ref, kc_ref, vtp_ref, vtc_ref, o_ref, lse_ref, *, heads):
    n = pl.program_id(1)
    variant = jnp.minimum(n, 1)
    blk = ATTN_BLOCK
    row = lax.broadcasted_iota(I32, (LANES, 1), 0)
    first_head = row < HEAD_DIM
    lse_row = lax.broadcasted_iota(I32, (LANES, blk), 0)
    lse_t = jnp.zeros((LANES, blk), F32)
    group = 8
    for h0 in range(0, heads, group):
        hs = range(h0, h0 + group)
        vals, scores = [], []
        for h in hs:
            sl = slice(LANES * (h // 2), LANES * (h // 2 + 1))
            qm = _head_halves(q_ref[:, sl])[h % 2]
            keys = jnp.concatenate([kp_ref[:, sl], kc_ref[:, sl]], axis=0)
            vals.append(jnp.concatenate([vtp_ref[sl, :], vtc_ref[sl, :]], axis=1))
            scores.append(_dot_t(keys, qm))
        probs, scales = [], []
        for h, s in zip(hs, scores):
            s = s + bias_ref[variant, h]
            m = jnp.max(s, axis=0, keepdims=True)
            e = jnp.exp2(s - m)
            den = jnp.sum(e, axis=0, keepdims=True)
            probs.append(e.astype(BF16))
            scales.append(1.0 / den)
            lse_t = jnp.where(lse_row == h, (m + jnp.log2(den)) * (1.0 / LOG2_E), lse_t)
        outs = [_dot(v, p) * sc for v, p, sc in zip(vals, probs, scales)]
        for idx in range(0, group, 2):
            sl = slice(LANES * ((h0 + idx) // 2), LANES * ((h0 + idx) // 2 + 1))
            o_ref[:, sl] = jnp.where(first_head, outs[idx], outs[idx + 1]).T.astype(o_ref.dtype)
    lse_ref[...] = lse_t.T


def _attn_a_t_group(qk, vt, *, dilation, heads):
    n_streams, length, _ = qk.shape
    width = heads * HEAD_DIM
    nb = length // ATTN_BLOCK
    bias = jnp.swapaxes(_alibi_window_bias(dilation, heads) * LOG2_E, 2, 3)
    blk = (None, ATTN_BLOCK, width)
    vblk = (None, width, ATTN_BLOCK)
    o, lse = pl.pallas_call(
        functools.partial(_attn_a_t_kernel, heads=heads),
        grid=(n_streams, nb),
        in_specs=[
            pl.BlockSpec(bias.shape, lambda s, n: (0, 0, 0, 0)),
            pl.BlockSpec(blk, lambda s, n: (s, n, 0)),
            pl.BlockSpec(blk, lambda s, n: (s, jnp.maximum(n - 1, 0), 1)),
            pl.BlockSpec(blk, lambda s, n: (s, n, 1)),
            pl.BlockSpec(vblk, lambda s, n: (s, 0, jnp.maximum(n - 1, 0))),
            pl.BlockSpec(vblk, lambda s, n: (s, 0, n)),
        ],
        out_specs=[
            pl.BlockSpec(blk, lambda s, n: (s, n, 0)),
            pl.BlockSpec((None, ATTN_BLOCK, LANES), lambda s, n: (s, n, 0)),
        ],
        out_shape=[
            jax.ShapeDtypeStruct((n_streams, length, width), BF16),
            jax.ShapeDtypeStruct((n_streams, length, LANES), F32),
        ],
        compiler_params=_params(("parallel", "arbitrary")),
        name=f"dilated_attn_r{dilation}",
    )(bias, qk, qk, qk, vt, vt)
    return o, lse


def _merge_proj_kernel(o0_ref, o1_ref, o2_ref, l0_ref, l1_ref, l2_ref, e_ref, w_ref, r_ref, out_ref):
    l0, l1, l2 = l0_ref[...], l1_ref[...], l2_ref[...]
    mx = jnp.maximum(jnp.maximum(l0, l1), l2)
    e0, e1, e2 = jnp.exp(l0 - mx), jnp.exp(l1 - mx), jnp.exp(l2 - mx)
    inv = 1.0 / (e0 + e1 + e2)
    expand = e_ref[...]
    acc = None
    for e, o_ref in ((e0, o0_ref), (e1, o1_ref), (e2, o2_ref)):
        hi, lo = _split_bf16(e * inv)
        w_full = _dot(hi, expand) + _dot(lo, expand)
        term = w_full * o_ref[...].astype(F32)
        acc = term if acc is None else acc + term
    out_ref[...] = r_ref[...] + _dot(acc.astype(BF16), w_ref[...])


def _merge_proj(outs, lses, w_out, res, *, heads, tm):
    T, width = outs[0].shape
    N = w_out.shape[1]
    tm = min(tm, T)
    expand = np.zeros((LANES, width), np.float32)
    for h in range(heads):
        expand[h, h * HEAD_DIM:(h + 1) * HEAD_DIM] = 1.0
    expand = jnp.asarray(expand, BF16)
    row = lambda i: (i, 0)
    fixed = lambda i: (0, 0)
    return pl.pallas_call(
        _merge_proj_kernel,
        grid=(T // tm,),
        in_specs=[pl.BlockSpec((tm, width), row)] * 3 + [pl.BlockSpec((tm, LANES), row)] * 3 + [
            pl.BlockSpec((LANES, width), fixed),
            pl.BlockSpec((width, N), fixed),
            pl.BlockSpec((tm, N), row),
        ],
        out_specs=pl.BlockSpec((tm, N), row),
        out_shape=jax.ShapeDtypeStruct((T, N), F32),
        compiler_params=_params(("parallel",)),
        name="merge_out_proj",
    )(*outs, *lses, expand, w_out, res)


def _proj_qk_vt_kernel(x_ref, g_ref, wqk_ref, wvt_ref, qk_ref, vt_ref, *, q_width, q_scale):
    xn = (_rms(x_ref[...]) * g_ref[...]).astype(BF16)
    qk = _dot(xn, wqk_ref[...])
    qk_ref[:, :q_width] = (qk[:, :q_width] * q_scale).astype(BF16)
    qk_ref[:, q_width:] = qk[:, q_width:].astype(BF16)
    vt_ref[...] = _dot_t(wvt_ref[...], xn).astype(BF16)


def _proj_qk_vt(h, g, w_in, *, batch, seq, tm, q_scale):
    T, D = h.shape
    width = w_in.shape[1] // 3
    tm = min(tm, seq)
    per_seq = seq // tm
    w = w_in.astype(BF16)
    wqk = w[:, :2 * width]
    wvt = w[:, 2 * width:].T
    fixed = lambda i: (0, 0)
    qk, vt = pl.pallas_call(
        functools.partial(_proj_qk_vt_kernel, q_width=width, q_scale=q_scale),
        grid=(T // tm,),
        in_specs=[
            pl.BlockSpec((tm, D), lambda i: (i, 0)),
            pl.BlockSpec((1, D), fixed),
            pl.BlockSpec((D, 2 * width), fixed),
            pl.BlockSpec((width, D), fixed),
        ],
        out_specs=[
            pl.BlockSpec((tm, 2 * width), lambda i: (i, 0)),
            pl.BlockSpec((None, width, tm), lambda i: (i // per_seq, 0, i % per_seq)),
        ],
        out_shape=[
            jax.ShapeDtypeStruct((T, 2 * width), BF16),
            jax.ShapeDtypeStruct((batch, width, seq), BF16),
        ],
        compiler_params=_params(("parallel",)),
        name="norm_proj_qk_vt",
    )(h, g.reshape(1, D), wqk, wvt)
    return qk.reshape(batch, seq, 2 * width), vt


def _head_halves(q):
    lane = lax.broadcasted_iota(I32, (1, LANES), 1)
    first = lane < HEAD_DIM
    zero = jnp.zeros_like(q)
    return jnp.where(first, q, zero), jnp.where(first, zero, q)


def _attn_b_kernel(slope_ref, q_ref, k_ref, vt_ref, lq1_ref, lk1_ref, lq2_ref, lk2_ref, sub_ref,
                   o_ref, *, t, lambda_init):
    h = pl.program_id(1)
    i = pl.program_id(2)
    slope = slope_ref[h]
    n_col = q_ref.shape[0] // t
    qs = [_head_halves(q_ref[col * t:(col + 1) * t, :]) for col in range(n_col)]
    rel = lax.broadcasted_iota(I32, (t, t), 1) - lax.broadcasted_iota(I32, (t, t), 0)
    rel_bias = slope * rel.astype(F32)

    def step(js, carry, block_modes):
        new = list(carry)
        plans = [[(col, c) for col, mode in enumerate(modes) if mode is not None for c in range(2)]
                 for modes in block_modes]
        starts = [pl.multiple_of(j * t, t) for j in js]
        scores = [[_dot_t(k_ref[pl.ds(st, t), :], qs[col][c]) for col, c in chains]
                  for st, chains in zip(starts, plans)]
        for j, st, chains, modes, block_scores in zip(js, starts, plans, block_modes, scores):
            vt = vt_ref[:, pl.ds(st, t)]
            probs, alphas = [], []
            for (col, c), s in zip(chains, block_scores):
                base = 3 * (2 * col + c)
                m, l = new[base], new[base + 1]
                shift = slope * ((i * n_col + col - j) * t).astype(F32)
                s = s - rel_bias
                if modes[col]:
                    s = jnp.where(rel >= 0, s, -jnp.inf)
                m_new = jnp.maximum(m, jnp.max(s, axis=0, keepdims=True) - shift)
                alpha = jnp.exp2(m - m_new)
                p = jnp.exp2(s - (m_new + shift))
                new[base] = m_new
                new[base + 1] = alpha * l + jnp.sum(p, axis=0, keepdims=True)
                probs.append(p.astype(BF16))
                alphas.append(alpha)
            for (col, c), p, alpha in zip(chains, probs, alphas):
                base = 3 * (2 * col + c)
                new[base + 2] = alpha * new[base + 2] + _dot(vt, p)
        return tuple(new)

    init = (jnp.full((1, t), -jnp.inf, F32), jnp.zeros((1, t), F32), jnp.zeros((LANES, t), F32)) * (2 * n_col)
    full = [(False,) * n_col] * n_col
    wide = 2 * n_col
    carry = lax.fori_loop(0, i // 2, lambda jj, c: step([wide * jj + b for b in range(wide)], c, full * 2), init)
    carry = lax.cond(i % 2 == 1,
                     lambda c: step([n_col * (i - 1) + b for b in range(n_col)], c, full),
                     lambda c: c, carry)
    diag = [tuple(None if col < d else (col == d) for col in range(n_col)) for d in range(n_col)]
    carry = step([n_col * i + d for d in range(n_col)], carry, diag)
    lam = (jnp.exp(jnp.sum(lq1_ref[...] * lk1_ref[...], axis=-1, keepdims=True))
           - jnp.exp(jnp.sum(lq2_ref[...] * lk2_ref[...], axis=-1, keepdims=True)) + lambda_init)
    for col in range(n_col):
        _, l1, a1, _, l2, a2 = carry[6 * col:6 * col + 6]
        o = a1 * (1.0 / l1) - lam * (a2 * (1.0 / l2))
        o = o * lax.rsqrt(jnp.mean(o * o, axis=0, keepdims=True) + RMS_EPS)
        o = o * (sub_ref[...] * (1.0 - lambda_init))
        o_ref[col * t:(col + 1) * t, :] = o.T.astype(o_ref.dtype)


def _attn_b(qk, vt, lq1, lk1, lq2, lk2, subln, *, batch, seq, heads, lambda_init, t, tq):
    t = min(t, seq)
    tq = min(tq, seq)
    width = 2 * HEAD_DIM
    slopes = jnp.asarray(LOG2_E * 2.0 ** (-8.0 * np.arange(1, heads + 1) / heads), F32)
    vec = lambda a: a.reshape(1, -1).astype(F32)
    small = lambda n: pl.BlockSpec((1, n), lambda b, h, i, s: (0, 0))
    return pl.pallas_call(
        functools.partial(_attn_b_kernel, t=t, lambda_init=lambda_init),
        grid_spec=pltpu.PrefetchScalarGridSpec(
            num_scalar_prefetch=1,
            grid=(batch, heads, seq // tq),
            in_specs=[
                pl.BlockSpec((None, tq, width), lambda b, h, i, s: (b, i, h)),
                pl.BlockSpec((None, seq, width), lambda b, h, i, s: (b, 0, heads + h)),
                pl.BlockSpec((None, width, seq), lambda b, h, i, s: (b, h, 0)),
                small(HEAD_DIM), small(HEAD_DIM), small(HEAD_DIM), small(HEAD_DIM),
                pl.BlockSpec((width, 1), lambda b, h, i, s: (0, 0)),
            ],
            out_specs=pl.BlockSpec((None, tq, width), lambda b, h, i, s: (b, i, h)),
        ),
        out_shape=jax.ShapeDtypeStruct((batch, seq, heads * width), BF16),
        compiler_params=_params(("parallel", "parallel", "arbitrary")),
        name="diff_attn",
    )(slopes, qk, qk, vt, vec(lq1), vec(lk1), vec(lq2), vec(lk2), subln.reshape(width, 1).astype(F32))


def _attn_c_kernel(q_ref, k_ref, vt_ref, later_ref, o_ref):
    i = pl.program_id(2)
    tq = q_ref.shape[0]
    kw = LANES
    n_sub = tq // kw
    qs = _head_halves(q_ref[...])
    later = later_ref[...]
    rel = lax.broadcasted_iota(I32, (kw, tq), 1) - lax.broadcasted_iota(I32, (kw, tq), 0)

    def blocks(kbs, carry, diag=None):
        diagonal = diag is not None
        units = []
        for idx, kb in enumerate(kbs):
            lo = diag[idx] * kw if diagonal else 0
            start = pl.multiple_of(kb * kw, kw)
            k = k_ref[pl.ds(start, kw), :]
            vt = vt_ref[:, pl.ds(start, kw)]
            causal = (rel[:, lo:] + (i * tq - kb * kw)) > 0 if diagonal else None
            for head in range(2):
                units.append(dict(head=head, vt=vt, causal=causal, lo=lo,
                                  z=_dot_t(k, qs[head][lo:])))
        for u in units:
            z = u["z"]
            softplus = jnp.maximum(z, 0.0) + jnp.log2(1.0 + jnp.exp2(-jnp.abs(z)))
            rest = jnp.where(u["causal"], softplus, 0.0) if diagonal else softplus
            u["logit"] = z - softplus
            u["rest"] = rest
            u["split"] = _split_bf16(rest)
        for u in units:
            hi, lo = u["split"]
            u["after"] = _dot(later, hi) + _dot(later, lo)
        def add_from(full, lo, delta):
            if lo == 0:
                return full + delta
            return jnp.concatenate([full[:, :lo], full[:, lo:] + delta], axis=1)

        c = [carry[0], carry[2]]
        weights = []
        for u in units:
            a = jnp.exp2(u["logit"] - u["after"] - c[u["head"]][:, u["lo"]:])
            if diagonal:
                a = jnp.where(u["causal"], a, 0.0)
            weights.append(a.astype(BF16))
            c[u["head"]] = add_from(c[u["head"]], u["lo"], jnp.sum(u["rest"], axis=0, keepdims=True))
        acc = [carry[1], carry[3]]
        for u, a in zip(units, weights):
            acc[u["head"]] = add_from(acc[u["head"]], u["lo"], _dot(u["vt"], a))
        return c[0], acc[0], c[1], acc[1]

    def alive(carry):
        return (jnp.min(jnp.minimum(carry[0], carry[2])) < -F32_EXP2_ZERO).astype(I32)

    zc = jnp.zeros((1, tq), F32)
    za = jnp.zeros((LANES, tq), F32)
    order = list(reversed(range(n_sub)))
    carry = blocks([i * n_sub + d for d in order], (zc, za, zc, za), diag=order)

    def cond(state):
        return jnp.logical_and(state[0] >= 0, state[1] > 0)

    def body(state):
        new = blocks([state[0], state[0] - 1], state[2:])
        return (state[0] - 2, alive(new)) + new

    out = lax.while_loop(cond, body, (i * n_sub - 1, alive(carry)) + carry)
    row = lax.broadcasted_iota(I32, (LANES, 1), 0)
    o = jnp.where(row < HEAD_DIM, out[3], out[5])
    o_ref[...] = o.T.astype(o_ref.dtype)


def _attn_c(qk, vt, *, batch, seq, heads, tq):
    tq = min(tq, seq)
    assert seq % tq == 0 and (tq // LANES) % 2 == 0
    pairs = heads // 2
    later = jnp.asarray(np.triu(np.ones((LANES, LANES), np.float32), 1), BF16)
    return pl.pallas_call(
        _attn_c_kernel,
        grid=(batch, pairs, seq // tq),
        in_specs=[
            pl.BlockSpec((None, tq, LANES), lambda b, h, i: (b, i, h)),
            pl.BlockSpec((None, seq, LANES), lambda b, h, i: (b, 0, pairs + h)),
            pl.BlockSpec((None, LANES, seq), lambda b, h, i: (b, h, 0)),
            pl.BlockSpec((LANES, LANES), lambda b, h, i: (0, 0)),
        ],
        out_specs=pl.BlockSpec((None, tq, LANES), lambda b, h, i: (b, i, h)),
        out_shape=jax.ShapeDtypeStruct((batch, seq, heads * HEAD_DIM), BF16),
        compiler_params=_params(("parallel", "parallel", "arbitrary")),
        name="stick_breaking_attn",
    )(qk, qk, vt, later)


def _router_kernel(x_ref, g_ref, whi_ref, wlo_ref, b_ref, u_ref, info_ref, cnt_ref, run_ref,
                   *, groups, per_group):
    G, E = groups, per_group
    tm = x_ref.shape[0]
    ne = G * E

    @pl.when(pl.program_id(0) == 0)
    def _():
        run_ref[...] = jnp.zeros_like(run_ref)

    xn = _rms(x_ref[...]) * g_ref[...]
    hi, lo = _split_bf16(xn)
    whi = whi_ref[...]
    lt = _dot_t(whi, hi) + _dot_t(whi, lo) + _dot_t(wlo_ref[...], hi) + b_ref[...]
    sub = lax.broadcasted_iota(I32, (E, tm), 0)
    coarse = lt[0:G]
    cmax = jnp.max(coarse, axis=0, keepdims=True)
    g_prob = 1.0 / jnp.sum(jnp.exp(coarse - cmax), axis=0, keepdims=True)
    g_idx = jnp.min(jnp.where(coarse == cmax, sub, G), axis=0, keepdims=True)
    fine = jnp.zeros((E, tm), F32)
    for grp in range(G):
        fine = jnp.where(g_idx == grp, lt[G + grp * E:G + (grp + 1) * E], fine)
    fmax = jnp.max(fine, axis=0, keepdims=True)
    fsum = jnp.sum(jnp.exp(fine - fmax), axis=0, keepdims=True)
    i1 = jnp.min(jnp.where(fine == fmax, sub, E), axis=0, keepdims=True)
    rest = jnp.where(sub == i1, -jnp.inf, fine)
    m2 = jnp.max(rest, axis=0, keepdims=True)
    i2 = jnp.min(jnp.where(rest == m2, sub, E), axis=0, keepdims=True)
    p1 = 1.0 / fsum
    p2 = jnp.exp(m2 - fmax) / fsum
    norm = p1 + p2
    gate1 = g_prob * (p1 / norm)
    gate2 = g_prob * (p2 / norm)
    e1 = g_idx * E + i1
    e2 = g_idx * E + i2
    ex = lax.broadcasted_iota(I32, (ne, tm), 0)
    oh1 = (ex == e1).astype(F32)
    oh2 = (ex == e2).astype(F32)
    both = oh1 + oh2
    earlier = _dot(both.astype(BF16), u_ref[...]) + run_ref[:, 0:1]
    r1 = jnp.sum(oh1 * earlier, axis=0, keepdims=True)
    r2 = jnp.sum(oh2 * earlier, axis=0, keepdims=True)
    total = run_ref[...] + jnp.sum(both, axis=1, keepdims=True)
    run_ref[...] = total
    cnt_ref[...] = total
    row = lax.broadcasted_iota(I32, (SUBLANES, tm), 0)
    fields = (e1.astype(F32), e2.astype(F32), r1, r2, gate1, gate2)
    info = jnp.zeros((SUBLANES, tm), F32)
    for idx, f in enumerate(fields):
        info = jnp.where(row == idx, f, info)
    info_ref[...] = info


def _router(h, g, w_coarse, b_coarse, w_fine, b_fine, *, tm):
    T, D = h.shape
    tm = min(tm, T)
    G, ne = w_coarse.shape[1], w_fine.shape[1]
    pad = LANES - G - ne
    wt = jnp.pad(jnp.concatenate([w_coarse, w_fine], axis=1), ((0, 0), (0, pad))).T
    b = jnp.pad(jnp.concatenate([b_coarse, b_fine]), (0, pad)).reshape(LANES, 1)
    whi, wlo = _split_bf16(wt)
    before = jnp.asarray(np.triu(np.ones((tm, tm), np.float32), 1), BF16)
    fixed = lambda i: (0, 0)
    return pl.pallas_call(
        functools.partial(_router_kernel, groups=G, per_group=ne // G),
        grid=(T // tm,),
        in_specs=[
            pl.BlockSpec((tm, D), lambda i: (i, 0)),
            pl.BlockSpec((1, D), fixed),
            pl.BlockSpec((LANES, D), fixed),
            pl.BlockSpec((LANES, D), fixed),
            pl.BlockSpec((LANES, 1), fixed),
            pl.BlockSpec((tm, tm), fixed),
        ],
        out_specs=[pl.BlockSpec((SUBLANES, tm), lambda i: (0, i)), pl.BlockSpec((ne, LANES), fixed)],
        out_shape=[jax.ShapeDtypeStruct((SUBLANES, T), F32), jax.ShapeDtypeStruct((ne, LANES), F32)],
        scratch_shapes=[pltpu.VMEM((ne, LANES), F32)],
        compiler_params=_params(("arbitrary",)),
        name="moe_router",
    )(h, g.reshape(1, D), whi, wlo, b, before)


def _plan(info, counts, *, tile):
    T = info.shape[1]
    ne = counts.shape[0]
    C = FFN_CHUNK
    experts = info[0:2].astype(I32)
    ranks = info[2:4].astype(I32)
    gates = info[4:6]
    cnt = counts[:, 0].astype(I32)
    padded = (cnt + C - 1) // C * C
    pad_end = jnp.cumsum(padded)
    pad_start = pad_end - padded
    ids = jnp.arange(ne, dtype=I32)
    start_of = jnp.sum(jnp.where(experts[:, :, None] == ids, pad_start, 0), axis=-1)
    dest = ranks + start_of
    n_chunks = (MOE_TOP_K * T + ne * (C - 1) + C - 1) // C
    n_used = (pad_end[-1] // C).astype(I32)
    first_row = jnp.minimum(jnp.arange(n_chunks, dtype=I32), n_used - 1) * C
    chunk_expert = jnp.minimum(jnp.sum(pad_end[None, :] <= first_row[:, None], axis=1), ne - 1).astype(I32)
    dest_tiles = dest.reshape(MOE_TOP_K, T // tile, tile).transpose(1, 0, 2).reshape(T // tile, 1, MOE_TOP_K * tile)
    return dest_tiles, gates.T, chunk_expert, n_used.reshape(1), n_chunks


def _dispatch_kernel(dst_ref, h_ref, g_ref, xs_in, xs_out, xbuf, sem):
    del xs_in
    i = pl.program_id(0)
    n = pl.num_programs(0)
    tm, D = h_ref.shape
    slot = i % 2

    def wait(s):
        for _ in range(MOE_TOP_K):
            pltpu.make_async_copy(xbuf.at[s], xs_out.at[pl.ds(0, tm)], sem.at[s]).wait()

    @pl.when(i >= 2)
    def _():
        wait(slot)

    xn = _rms(h_ref[...]) * g_ref[...]
    xbuf[slot] = _pack_bf16_pair(xn[:, :D // 2], xn[:, D // 2:])

    def issue(r, _):
        for k in range(MOE_TOP_K):
            pltpu.make_async_copy(xbuf.at[slot, pl.ds(r, 1)], xs_out.at[pl.ds(dst_ref[0, 0, k * tm + r], 1)],
                                  sem.at[slot]).start()
        return 0

    lax.fori_loop(0, tm, issue, 0, unroll=True)

    @pl.when(i == n - 1)
    def _():
        wait(slot)

        @pl.when(i >= 1)
        def _():
            wait(1 - slot)


def _dispatch(h, g, dest_tiles, n_rows, *, tm):
    T, D = h.shape
    xs0 = jnp.zeros((n_rows, D // 2), U32)
    return pl.pallas_call(
        _dispatch_kernel,
        grid=(T // tm,),
        in_specs=[
            pl.BlockSpec((1, 1, MOE_TOP_K * tm), lambda i: (i, 0, 0), memory_space=pltpu.SMEM),
            pl.BlockSpec((tm, D), lambda i: (i, 0)),
            pl.BlockSpec((1, D), lambda i: (0, 0)),
            pl.BlockSpec(memory_space=pl.ANY),
        ],
        out_specs=pl.BlockSpec(memory_space=pl.ANY),
        out_shape=jax.ShapeDtypeStruct((n_rows, D // 2), U32),
        scratch_shapes=[pltpu.VMEM((2, tm, D // 2), U32), pltpu.SemaphoreType.DMA((2,))],
        input_output_aliases={3: 0},
        compiler_params=_params(("arbitrary",)),
        name="moe_dispatch",
    )(dest_tiles, h, g.reshape(1, D), xs0)


def _ffn_kernel(ce_ref, nu_ref, x_ref, wgu_ref, wdn_ref, y_ref, wgu_bf, wdn_bf, *, d_expert):
    c = pl.program_id(0)
    n_used = nu_ref[0]

    @pl.when(c < n_used)
    def _():
        @pl.when(jnp.logical_or(c == 0, ce_ref[c] != ce_ref[jnp.maximum(c - 1, 0)]))
        def _():
            wgu_bf[...] = wgu_ref[...].astype(BF16)
            wdn_bf[...] = wdn_ref[...].astype(BF16)

        half = x_ref.shape[1]
        xa, xb = _unpack_bf16_pair(x_ref[...])
        n_blk = 2
        w = d_expert // n_blk
        proj = lambda lo: _dot(xa, wgu_bf[:half, lo:lo + w]) + _dot(xb, wgu_bf[half:, lo:lo + w])
        gates = [proj(b * w) for b in range(n_blk)]
        ups = [proj(d_expert + b * w) for b in range(n_blk)]
        acts = [((g * jax.nn.sigmoid(g)) * u).astype(BF16) for g, u in zip(gates, ups)]
        y = _dot(acts[0], wdn_bf[:w])
        for b in range(1, n_blk):
            y = y + _dot(acts[b], wdn_bf[b * w:(b + 1) * w])
        y_ref[...] = y

    @pl.when(c >= n_used)
    def _():
        y_ref[...] = jnp.zeros_like(y_ref)


def _ffn(xs, chunk_expert, n_used, w_gate_up, w_down, layer):
    n_rows, half = xs.shape
    C = FFN_CHUNK
    n_chunks = n_rows // C
    D, F2 = w_gate_up.shape[2:]
    d_expert = w_down.shape[2]
    return pl.pallas_call(
        functools.partial(_ffn_kernel, d_expert=d_expert),
        grid_spec=pltpu.PrefetchScalarGridSpec(
            num_scalar_prefetch=2,
            grid=(n_chunks,),
            in_specs=[
                pl.BlockSpec((C, half), lambda c, ce, nu: (jnp.minimum(c, nu[0] - 1), 0)),
                pl.BlockSpec((None, None, D, F2), lambda c, ce, nu: (layer, ce[c], 0, 0)),
                pl.BlockSpec((None, None, d_expert, D), lambda c, ce, nu: (layer, ce[c], 0, 0)),
            ],
            out_specs=pl.BlockSpec((C, D), lambda c, ce, nu: (c, 0)),
            scratch_shapes=[pltpu.VMEM((D, F2), BF16), pltpu.VMEM((d_expert, D), BF16)],
        ),
        out_shape=jax.ShapeDtypeStruct((n_rows, D), F32),
        compiler_params=_params(("arbitrary",)),
        name="moe_expert_ffn",
    )(chunk_expert, n_used, xs, w_gate_up, w_down)


def _combine_ple_kernel(dst_ref, dstn_ref, h_ref, gt_ref, p_ref, g_ref, wp_ref, wg_ref, gf_ref, y_hbm,
                        o_ref, ybuf, sem, *, final):
    i = pl.program_id(0)
    n = pl.num_programs(0)
    tm, D = h_ref.shape
    rows = MOE_TOP_K * tm
    slot = i % 2

    def gather(dref, s):
        def issue(r, _):
            for k in range(MOE_TOP_K):
                pltpu.make_async_copy(y_hbm.at[pl.ds(dref[0, 0, k * tm + r], 1)],
                                      ybuf.at[s, pl.ds(k * tm + r, 1)], sem.at[s]).start()
            return 0
        lax.fori_loop(0, tm, issue, 0, unroll=True)

    @pl.when(i == 0)
    def _():
        gather(dst_ref, 0)

    def wait(s):
        pltpu.make_async_copy(y_hbm.at[pl.ds(0, rows)], ybuf.at[s], sem.at[s]).wait()

    wait(slot)
    gather(dstn_ref, 1 - slot)
    gt = gt_ref[...]
    h = h_ref[...] + (gt[:, 0:1] * ybuf[slot, :tm] + gt[:, 1:2] * ybuf[slot, tm:])
    xn = (_rms(h) * g_ref[...]).astype(BF16)
    gate = jax.nn.sigmoid(_dot(xn, wg_ref[...]))
    out = h + _dot(p_ref[...].astype(BF16), wp_ref[...]) * gate
    if final:
        out = _rms(out) * gf_ref[...]
    o_ref[...] = out

    @pl.when(i == n - 1)
    def _():
        wait(1 - slot)


def _combine_ple(h, y, dest_tiles, gates, p, layer, g, w_proj, w_gate, g_final, *, final, tm):
    T, D = h.shape
    Pd = p.shape[2]
    n_tiles = T // tm
    row = lambda i: (i, 0)
    fixed = lambda i: (0, 0)
    smem = functools.partial(pl.BlockSpec, (1, 1, MOE_TOP_K * tm), memory_space=pltpu.SMEM)
    return pl.pallas_call(
        functools.partial(_combine_ple_kernel, final=final),
        grid=(n_tiles,),
        in_specs=[
            smem(lambda i: (i, 0, 0)),
            smem(lambda i: (jnp.minimum(i + 1, n_tiles - 1), 0, 0)),
            pl.BlockSpec((tm, D), row),
            pl.BlockSpec((tm, MOE_TOP_K), row),
            pl.BlockSpec((None, tm, Pd), lambda i: (layer, i, 0)),
            pl.BlockSpec((1, D), fixed),
            pl.BlockSpec((Pd, D), fixed),
            pl.BlockSpec((D, D), fixed),
            pl.BlockSpec((1, D), fixed),
            pl.BlockSpec(memory_space=pl.ANY),
        ],
        out_specs=pl.BlockSpec((tm, D), row),
        out_shape=jax.ShapeDtypeStruct((T, D), F32),
        scratch_shapes=[pltpu.VMEM((2, MOE_TOP_K * tm, D), F32), pltpu.SemaphoreType.DMA((2,))],
        compiler_params=_params(("arbitrary",)),
        name="moe_combine_ple",
    )(dest_tiles, dest_tiles, h, gates, p, g.reshape(1, D), w_proj, w_gate, g_final.reshape(1, D), y)


def _moe_ple(h, p, layer, g_ffn, w_coarse, b_coarse, w_fine, b_fine, w_gate_up, w_down, g_ple, w_proj, w_gate,
             g_final, *, final):
    tile = min(512, h.shape[0])
    info, counts = _router(h, g_ffn, w_coarse, b_coarse, w_fine, b_fine, tm=min(512, h.shape[0]))
    dest_tiles, gates, chunk_expert, n_used, n_chunks = _plan(info, counts, tile=tile)
    xs = _dispatch(h, g_ffn, dest_tiles, n_chunks * FFN_CHUNK, tm=tile)
    y = _ffn(xs, chunk_expert, n_used, w_gate_up, w_down, layer)
    return _combine_ple(h, y, dest_tiles, gates, p, layer, g_ple, w_proj.astype(BF16), w_gate.astype(BF16),
                        g_final, final=final, tm=tile)


def _mixer_a(h, g, w_in, w_out, *, batch, seq):
    T, D = h.shape
    width = A_HEADS * HEAD_DIM
    n_groups = len(A_PATTERNS)
    w = w_in.astype(BF16)
    wvt = jnp.swapaxes(w.reshape(D, n_groups, 3, width)[:, :, 2], 0, 1).swapaxes(1, 2)
    xn = _norm(h, g, tm=1024)
    outs, lses = [], []
    for grp, (window, r) in enumerate(A_PATTERNS):
        assert window // r == ATTN_BLOCK and seq % (r * ATTN_BLOCK) == 0
        L = seq // r
        to_streams = lambda a: a.reshape(batch, L, r, -1).transpose(0, 2, 1, 3).reshape(T, -1)
        from_streams = lambda a: a.reshape(batch, r, L, -1).transpose(0, 2, 1, 3).reshape(T, -1)
        xs = xn if r == 1 else to_streams(xn)
        qk, vt = _proj_a(xs, w, wvt, group=grp, length=L, width=width, tm=512,
                         q_scale=LOG2_E / math.sqrt(HEAD_DIM))
        o, lse = _attn_a_t_group(qk, vt, dilation=r, heads=A_HEADS)
        outs.append(from_streams(o))
        lses.append(from_streams(lse))
    return _merge_proj(outs, lses, w_out.astype(BF16), h, heads=A_HEADS, tm=512)


def _mixer_b(h, g, w_in, w_out, lq1, lk1, lq2, lk2, subln, lambda_init, *, batch, seq):
    D = h.shape[1]
    heads = D // (2 * HEAD_DIM)
    qk, vt = _proj_qk_vt(h, g, w_in, batch=batch, seq=seq, tm=512, q_scale=LOG2_E / math.sqrt(HEAD_DIM))
    o = _attn_b(qk, vt, lq1, lk1, lq2, lk2, subln, batch=batch, seq=seq, heads=heads,
                lambda_init=lambda_init, t=256, tq=512)
    return _mm_res(o.reshape(batch * seq, -1), w_out.astype(BF16), h, tm=512)


def _mixer_c(h, g, w_in, w_out, *, batch, seq):
    D = h.shape[1]
    heads = D // HEAD_DIM
    qk, vt = _proj_qk_vt(h, g, w_in, batch=batch, seq=seq, tm=512, q_scale=LOG2_E / math.sqrt(HEAD_DIM))
    o = _attn_c(qk, vt, batch=batch, seq=seq, heads=heads, tq=512)
    return _mm_res(o.reshape(batch * seq, -1), w_out.astype(BF16), h, tm=512)


def kernel(x, p, norm_mix, norm_ffn, norm_ple, norm_final, a_w_in, a_w_out, b_w_in, b_w_out,
           b_lambda_q1, b_lambda_k1, b_lambda_q2, b_lambda_k2, b_subln, c_w_in, c_w_out,
           moe_w_coarse, moe_b_coarse, moe_w_fine, moe_b_fine, moe_w_gate_up, moe_w_down,
           ple_w_proj, ple_w_gate):
    batch, seq, D = x.shape
    depth = p.shape[0]
    T = batch * seq
    h = x.reshape(T, D)
    p_rows = p.reshape(depth, T, -1)
    for i in range(depth):
        kind = i % N_MIXERS
        j = i // N_MIXERS
        if kind == 0:
            h = _mixer_a(h, norm_mix[i], a_w_in[j], a_w_out[j], batch=batch, seq=seq)
        elif kind == 1:
            lambda_init = 0.8 - 0.6 * math.exp(-0.3 * i)
            h = _mixer_b(h, norm_mix[i], b_w_in[j], b_w_out[j], b_lambda_q1[j], b_lambda_k1[j],
                         b_lambda_q2[j], b_lambda_k2[j], b_subln[j], lambda_init, batch=batch, seq=seq)
        else:
            h = _mixer_c(h, norm_mix[i], c_w_in[j], c_w_out[j], batch=batch, seq=seq)
        h = _moe_ple(h, p_rows, i, norm_ffn[i], moe_w_coarse[i], moe_b_coarse[i], moe_w_fine[i],
                     moe_b_fine[i], moe_w_gate_up, moe_w_down, norm_ple[i], ple_w_proj[i],
                     ple_w_gate[i], norm_final, final=(i == depth - 1))
    return h.reshape(batch, seq, D)
```

```python
import functools
import math

import jax
import jax.numpy as jnp
import numpy as np
from jax import lax
from jax.experimental import pallas as pl
from jax.experimental.pallas import tpu as pltpu

F32 = jnp.float32
BF16 = jnp.bfloat16
U32 = jnp.uint32
I32 = jnp.int32

HEAD_DIM = 64
N_MIXERS = 3
ATTN_BLOCK = 128
RMS_EPS = 1e-6
A_HEADS = 16
A_PATTERNS = ((128, 1), (512, 4), (2048, 16))
MOE_GROUPS = 8
MOE_EXPERTS_PER_GROUP = 8
MOE_TOP_K = 2
FFN_CHUNK = 256
LANES = 128
SUBLANES = 8
F32_EXP2_ZERO = -151.0
LOG2_E = math.log2(math.e)
VMEM_LIMIT = 48 * 1024 * 1024


def _params(sem, vmem=VMEM_LIMIT):
    return pltpu.CompilerParams(dimension_semantics=sem, vmem_limit_bytes=vmem)


def _rms(x):
    return x * lax.rsqrt(jnp.mean(x * x, axis=-1, keepdims=True) + RMS_EPS)


def _dot_t(a, b):
    return lax.dot_general(a, b, (((1,), (1,)), ((), ())), preferred_element_type=F32)


def _dot(a, b):
    return jnp.dot(a, b, preferred_element_type=F32)


def _split_bf16(x):
    hi = x.astype(BF16)
    lo = (x - hi.astype(F32)).astype(BF16)
    return hi, lo


def _bits(x):
    return lax.bitcast_convert_type(x, U32)


def _pack_bf16_pair(a, b):
    a = a.astype(BF16).astype(F32)
    b = b.astype(BF16).astype(F32)
    return (_bits(a) >> 16) | _bits(b)


def _unpack_bf16_pair(w):
    a = lax.bitcast_convert_type(w << 16, F32).astype(BF16)
    b = lax.bitcast_convert_type(w & jnp.uint32(0xFFFF0000), F32).astype(BF16)
    return a, b


def _norm_kernel(x_ref, g_ref, o_ref):
    o_ref[...] = (_rms(x_ref[...]) * g_ref[...]).astype(o_ref.dtype)


def _norm(x, g, *, tm):
    T, D = x.shape
    tm = min(tm, T)
    return pl.pallas_call(
        _norm_kernel,
        grid=(T // tm,),
        in_specs=[pl.BlockSpec((tm, D), lambda i: (i, 0)), pl.BlockSpec((1, D), lambda i: (0, 0))],
        out_specs=pl.BlockSpec((tm, D), lambda i: (i, 0)),
        out_shape=jax.ShapeDtypeStruct((T, D), BF16),
        compiler_params=_params(("parallel",)),
        name="rmsnorm",
    )(x, g.reshape(1, D))


def _mm_res_kernel(a_ref, w_ref, r_ref, o_ref):
    o_ref[...] = r_ref[...] + _dot(a_ref[...], w_ref[...])


def _mm_res(a, w, res, *, tm):
    T, K = a.shape
    N = w.shape[1]
    tm = min(tm, T)
    return pl.pallas_call(
        _mm_res_kernel,
        grid=(T // tm,),
        in_specs=[
            pl.BlockSpec((tm, K), lambda i: (i, 0)),
            pl.BlockSpec((K, N), lambda i: (0, 0)),
            pl.BlockSpec((tm, N), lambda i: (i, 0)),
        ],
        out_specs=pl.BlockSpec((tm, N), lambda i: (i, 0)),
        out_shape=jax.ShapeDtypeStruct((T, N), F32),
        compiler_params=_params(("parallel",)),
        name="out_proj",
    )(a, w, res)


def _alibi_window_bias(dilation, heads):
    blk = ATTN_BLOCK
    a = np.arange(blk)[:, None]
    b = np.arange(2 * blk)[None, :]
    delta = blk + a - b
    valid = (delta >= 0) & (delta <= blk)
    slopes = 2.0 ** (-8.0 * np.arange(1, heads + 1) / heads)
    bias = -slopes[:, None, None] * (delta * dilation).astype(np.float64)[None]
    bias = np.where(valid[None], bias, -np.inf)
    first = bias.copy()
    first[:, :, :blk] = -np.inf
    return jnp.asarray(np.stack([first, bias]), F32)


def _proj_a_kernel(a_ref, wq_ref, wk_ref, wvt_ref, qk_ref, vt_ref, *, q_scale):
    a = a_ref[...]
    width = wq_ref.shape[1]
    qk_ref[:, :width] = (_dot(a, wq_ref[...]) * q_scale).astype(BF16)
    qk_ref[:, width:] = _dot(a, wk_ref[...]).astype(BF16)
    vt_ref[...] = _dot_t(wvt_ref[...], a).astype(BF16)


def _proj_a(a, w, wvt, *, group, length, width, tm, q_scale):
    T, D = a.shape
    tm = min(tm, length)
    per_stream = length // tm
    first = group * 3
    qk, vt = pl.pallas_call(
        functools.partial(_proj_a_kernel, q_scale=q_scale),
        grid=(T // tm,),
        in_specs=[
            pl.BlockSpec((tm, D), lambda i: (i, 0)),
            pl.BlockSpec((D, width), lambda i: (0, first)),
            pl.BlockSpec((D, width), lambda i: (0, first + 1)),
            pl.BlockSpec((None, width, D), lambda i: (group, 0, 0)),
        ],
        out_specs=[
            pl.BlockSpec((tm, 2 * width), lambda i: (i, 0)),
            pl.BlockSpec((None, width, tm), lambda i: (i // per_stream, 0, i % per_stream)),
        ],
        out_shape=[
            jax.ShapeDtypeStruct((T, 2 * width), BF16),
            jax.ShapeDtypeStruct((T // length, width, length), BF16),
        ],
        compiler_params=_params(("parallel",)),
        name="proj_qk_vt",
    )(a, w, w, wvt)
    return qk.reshape(T // length, length, 2 * width), vt


def _attn_a_t_kernel(bias_ref, q_ref, kp_ref, kc_ref, vtp_ref, vtc_ref, o_ref, lse_ref, *, heads):
    n = pl.program_id(1)
    variant = jnp.minimum(n, 1)
    blk = ATTN_BLOCK
    row = lax.broadcasted_iota(I32, (LANES, 1), 0)
    first_head = row < HEAD_DIM
    lse_row = lax.broadcasted_iota(I32, (LANES, blk), 0)
    lse_t = jnp.zeros((LANES, blk), F32)
    group = 8
    for h0 in range(0, heads, group):
        hs = range(h0, h0 + group)
        vals, scores = [], []
        for h in hs:
            sl = slice(LANES * (h // 2), LANES * (h // 2 + 1))
            qm = _head_halves(q_ref[:, sl])[h % 2]
            keys = jnp.concatenate([kp_ref[:, sl], kc_ref[:, sl]], axis=0)
            vals.append(jnp.concatenate([vtp_ref[sl, :], vtc_ref[sl, :]], axis=1))
            scores.append(_dot_t(keys, qm))
        probs, scales = [], []
        for h, s in zip(hs, scores):
            s = s + bias_ref[variant, h]
            m = jnp.max(s, axis=0, keepdims=True)
            e = jnp.exp2(s - m)
            den = jnp.sum(e, axis=0, keepdims=True)
            probs.append(e.astype(BF16))
            scales.append(1.0 / den)
            lse_t = jnp.where(lse_row == h, (m + jnp.log2(den)) * (1.0 / LOG2_E), lse_t)
        outs = [_dot(v, p) * sc for v, p, sc in zip(vals, probs, scales)]
        for idx in range(0, group, 2):
            sl = slice(LANES * ((h0 + idx) // 2), LANES * ((h0 + idx) // 2 + 1))
            o_ref[:, sl] = jnp.where(first_head, outs[idx], outs[idx + 1]).T.astype(o_ref.dtype)
    lse_ref[...] = lse_t.T


def _attn_a_t_group(qk, vt, *, dilation, heads):
    n_streams, length, _ = qk.shape
    width = heads * HEAD_DIM
    nb = length // ATTN_BLOCK
    bias = jnp.swapaxes(_alibi_window_bias(dilation, heads) * LOG2_E, 2, 3)
    blk = (None, ATTN_BLOCK, width)
    vblk = (None, width, ATTN_BLOCK)
    o, lse = pl.pallas_call(
        functools.partial(_attn_a_t_kernel, heads=heads),
        grid=(n_streams, nb),
        in_specs=[
            pl.BlockSpec(bias.shape, lambda s, n: (0, 0, 0, 0)),
            pl.BlockSpec(blk, lambda s, n: (s, n, 0)),
            pl.BlockSpec(blk, lambda s, n: (s, jnp.maximum(n - 1, 0), 1)),
            pl.BlockSpec(blk, lambda s, n: (s, n, 1)),
            pl.BlockSpec(vblk, lambda s, n: (s, 0, jnp.maximum(n - 1, 0))),
            pl.BlockSpec(vblk, lambda s, n: (s, 0, n)),
        ],
        out_specs=[
            pl.BlockSpec(blk, lambda s, n: (s, n, 0)),
            pl.BlockSpec((None, ATTN_BLOCK, LANES), lambda s, n: (s, n, 0)),
        ],
        out_shape=[
            jax.ShapeDtypeStruct((n_streams, length, width), BF16),
            jax.ShapeDtypeStruct((n_streams, length, LANES), F32),
        ],
        compiler_params=_params(("parallel", "arbitrary")),
        name=f"dilated_attn_r{dilation}",
    )(bias, qk, qk, qk, vt, vt)
    return o, lse


def _merge_proj_kernel(o0_ref, o1_ref, o2_ref, l0_ref, l1_ref, l2_ref, e_ref, w_ref, r_ref, out_ref):
    l0, l1, l2 = l0_ref[...], l1_ref[...], l2_ref[...]
    mx = jnp.maximum(jnp.maximum(l0, l1), l2)
    e0, e1, e2 = jnp.exp(l0 - mx), jnp.exp(l1 - mx), jnp.exp(l2 - mx)
    inv = 1.0 / (e0 + e1 + e2)
    expand = e_ref[...]
    acc = None
    for e, o_ref in ((e0, o0_ref), (e1, o1_ref), (e2, o2_ref)):
        hi, lo = _split_bf16(e * inv)
        w_full = _dot(hi, expand) + _dot(lo, expand)
        term = w_full * o_ref[...].astype(F32)
        acc = term if acc is None else acc + term
    out_ref[...] = r_ref[...] + _dot(acc.astype(BF16), w_ref[...])


def _merge_proj(outs, lses, w_out, res, *, heads, tm):
    T, width = outs[0].shape
    N = w_out.shape[1]
    tm = min(tm, T)
    expand = np.zeros((LANES, width), np.float32)
    for h in range(heads):
        expand[h, h * HEAD_DIM:(h + 1) * HEAD_DIM] = 1.0
    expand = jnp.asarray(expand, BF16)
    row = lambda i: (i, 0)
    fixed = lambda i: (0, 0)
    return pl.pallas_call(
        _merge_proj_kernel,
        grid=(T // tm,),
        in_specs=[pl.BlockSpec((tm, width), row)] * 3 + [pl.BlockSpec((tm, LANES), row)] * 3 + [
            pl.BlockSpec((LANES, width), fixed),
            pl.BlockSpec((width, N), fixed),
            pl.BlockSpec((tm, N), row),
        ],
        out_specs=pl.BlockSpec((tm, N), row),
        out_shape=jax.ShapeDtypeStruct((T, N), F32),
        compiler_params=_params(("parallel",)),
        name="merge_out_proj",
    )(*outs, *lses, expand, w_out, res)


def _proj_qk_vt_kernel(x_ref, g_ref, wqk_ref, wvt_ref, qk_ref, vt_ref, *, q_width, q_scale):
    xn = (_rms(x_ref[...]) * g_ref[...]).astype(BF16)
    qk = _dot(xn, wqk_ref[...])
    qk_ref[:, :q_width] = (qk[:, :q_width] * q_scale).astype(BF16)
    qk_ref[:, q_width:] = qk[:, q_width:].astype(BF16)
    vt_ref[...] = _dot_t(wvt_ref[...], xn).astype(BF16)


def _proj_qk_vt(h, g, w_in, *, batch, seq, tm, q_scale):
    T, D = h.shape
    width = w_in.shape[1] // 3
    tm = min(tm, seq)
    per_seq = seq // tm
    w = w_in.astype(BF16)
    wqk = w[:, :2 * width]
    wvt = w[:, 2 * width:].T
    fixed = lambda i: (0, 0)
    qk, vt = pl.pallas_call(
        functools.partial(_proj_qk_vt_kernel, q_width=width, q_scale=q_scale),
        grid=(T // tm,),
        in_specs=[
            pl.BlockSpec((tm, D), lambda i: (i, 0)),
            pl.BlockSpec((1, D), fixed),
            pl.BlockSpec((D, 2 * width), fixed),
            pl.BlockSpec((width, D), fixed),
        ],
        out_specs=[
            pl.BlockSpec((tm, 2 * width), lambda i: (i, 0)),
            pl.BlockSpec((None, width, tm), lambda i: (i // per_seq, 0, i % per_seq)),
        ],
        out_shape=[
            jax.ShapeDtypeStruct((T, 2 * width), BF16),
            jax.ShapeDtypeStruct((batch, width, seq), BF16),
        ],
        compiler_params=_params(("parallel",)),
        name="norm_proj_qk_vt",
    )(h, g.reshape(1, D), wqk, wvt)
    return qk.reshape(batch, seq, 2 * width), vt


def _head_halves(q):
    lane = lax.broadcasted_iota(I32, (1, LANES), 1)
    first = lane < HEAD_DIM
    zero = jnp.zeros_like(q)
    return jnp.where(first, q, zero), jnp.where(first, zero, q)


def _attn_b_kernel(slope_ref, q_ref, k_ref, vt_ref, lq1_ref, lk1_ref, lq2_ref, lk2_ref, sub_ref,
                   o_ref, *, t, lambda_init):
    h = pl.program_id(1)
    i = pl.program_id(2)
    slope = slope_ref[h]
    n_col = q_ref.shape[0] // t
    qs = [_head_halves(q_ref[col * t:(col + 1) * t, :]) for col in range(n_col)]
    rel = lax.broadcasted_iota(I32, (t, t), 1) - lax.broadcasted_iota(I32, (t, t), 0)
    rel_bias = slope * rel.astype(F32)

    def step(js, carry, block_modes):
        new = list(carry)
        plans = [[(col, c) for col, mode in enumerate(modes) if mode is not None for c in range(2)]
                 for modes in block_modes]
        starts = [pl.multiple_of(j * t, t) for j in js]
        scores = [[_dot_t(k_ref[pl.ds(st, t), :], qs[col][c]) for col, c in chains]
                  for st, chains in zip(starts, plans)]
        for j, st, chains, modes, block_scores in zip(js, starts, plans, block_modes, scores):
            vt = vt_ref[:, pl.ds(st, t)]
            probs, alphas = [], []
            for (col, c), s in zip(chains, block_scores):
                base = 3 * (2 * col + c)
                m, l = new[base], new[base + 1]
                shift = slope * ((i * n_col + col - j) * t).astype(F32)
                s = s - rel_bias
                if modes[col]:
                    s = jnp.where(rel >= 0, s, -jnp.inf)
                m_new = jnp.maximum(m, jnp.max(s, axis=0, keepdims=True) - shift)
                alpha = jnp.exp2(m - m_new)
                p = jnp.exp2(s - (m_new + shift))
                new[base] = m_new
                new[base + 1] = alpha * l + jnp.sum(p, axis=0, keepdims=True)
                probs.append(p.astype(BF16))
                alphas.append(alpha)
            for (col, c), p, alpha in zip(chains, probs, alphas):
                base = 3 * (2 * col + c)
                new[base + 2] = alpha * new[base + 2] + _dot(vt, p)
        return tuple(new)

    init = (jnp.full((1, t), -jnp.inf, F32), jnp.zeros((1, t), F32), jnp.zeros((LANES, t), F32)) * (2 * n_col)
    full = [(False,) * n_col] * n_col
    wide = 2 * n_col
    carry = lax.fori_loop(0, i // 2, lambda jj, c: step([wide * jj + b for b in range(wide)], c, full * 2), init)
    carry = lax.cond(i % 2 == 1,
                     lambda c: step([n_col * (i - 1) + b for b in range(n_col)], c, full),
                     lambda c: c, carry)
    diag = [tuple(None if col < d else (col == d) for col in range(n_col)) for d in range(n_col)]
    carry = step([n_col * i + d for d in range(n_col)], carry, diag)
    lam = (jnp.exp(jnp.sum(lq1_ref[...] * lk1_ref[...], axis=-1, keepdims=True))
           - jnp.exp(jnp.sum(lq2_ref[...] * lk2_ref[...], axis=-1, keepdims=True)) + lambda_init)
    for col in range(n_col):
        _, l1, a1, _, l2, a2 = carry[6 * col:6 * col + 6]
        o = a1 * (1.0 / l1) - lam * (a2 * (1.0 / l2))
        o = o * lax.rsqrt(jnp.mean(o * o, axis=0, keepdims=True) + RMS_EPS)
        o = o * (sub_ref[...] * (1.0 - lambda_init))
        o_ref[col * t:(col + 1) * t, :] = o.T.astype(o_ref.dtype)


def _attn_b(qk, vt, lq1, lk1, lq2, lk2, subln, *, batch, seq, heads, lambda_init, t, tq):
    t = min(t, seq)
    tq = min(tq, seq)
    width = 2 * HEAD_DIM
    slopes = jnp.asarray(LOG2_E * 2.0 ** (-8.0 * np.arange(1, heads + 1) / heads), F32)
    vec = lambda a: a.reshape(1, -1).astype(F32)
    small = lambda n: pl.BlockSpec((1, n), lambda b, h, i, s: (0, 0))
    return pl.pallas_call(
        functools.partial(_attn_b_kernel, t=t, lambda_init=lambda_init),
        grid_spec=pltpu.PrefetchScalarGridSpec(
            num_scalar_prefetch=1,
            grid=(batch, heads, seq // tq),
            in_specs=[
                pl.BlockSpec((None, tq, width), lambda b, h, i, s: (b, i, h)),
                pl.BlockSpec((None, seq, width), lambda b, h, i, s: (b, 0, heads + h)),
                pl.BlockSpec((None, width, seq), lambda b, h, i, s: (b, h, 0)),
                small(HEAD_DIM), small(HEAD_DIM), small(HEAD_DIM), small(HEAD_DIM),
                pl.BlockSpec((width, 1), lambda b, h, i, s: (0, 0)),
            ],
            out_specs=pl.BlockSpec((None, tq, width), lambda b, h, i, s: (b, i, h)),
        ),
        out_shape=jax.ShapeDtypeStruct((batch, seq, heads * width), BF16),
        compiler_params=_params(("parallel", "parallel", "arbitrary")),
        name="diff_attn",
    )(slopes, qk, qk, vt, vec(lq1), vec(lk1), vec(lq2), vec(lk2), subln.reshape(width, 1).astype(F32))


def _attn_c_kernel(q_ref, k_ref, vt_ref, later_ref, o_ref):
    i = pl.program_id(2)
    tq = q_ref.shape[0]
    kw = LANES
    n_sub = tq // kw
    qs = _head_halves(q_ref[...])
    later = later_ref[...]
    rel = lax.broadcasted_iota(I32, (kw, tq), 1) - lax.broadcasted_iota(I32, (kw, tq), 0)

    def blocks(kbs, carry, diag=None):
        diagonal = diag is not None
        units = []
        for idx, kb in enumerate(kbs):
            lo = diag[idx] * kw if diagonal else 0
            start = pl.multiple_of(kb * kw, kw)
            k = k_ref[pl.ds(start, kw), :]
            vt = vt_ref[:, pl.ds(start, kw)]
            causal = (rel[:, lo:] + (i * tq - kb * kw)) > 0 if diagonal else None
            for head in range(2):
                units.append(dict(head=head, vt=vt, causal=causal, lo=lo,
                                  z=_dot_t(k, qs[head][lo:])))
        for u in units:
            z = u["z"]
            softplus = jnp.maximum(z, 0.0) + jnp.log2(1.0 + jnp.exp2(-jnp.abs(z)))
            rest = jnp.where(u["causal"], softplus, 0.0) if diagonal else softplus
            u["logit"] = z - softplus
            u["rest"] = rest
            u["split"] = _split_bf16(rest)
        for u in units:
            hi, lo = u["split"]
            u["after"] = _dot(later, hi) + _dot(later, lo)
        def add_from(full, lo, delta):
            if lo == 0:
                return full + delta
            return jnp.concatenate([full[:, :lo], full[:, lo:] + delta], axis=1)

        c = [carry[0], carry[2]]
        weights = []
        for u in units:
            a = jnp.exp2(u["logit"] - u["after"] - c[u["head"]][:, u["lo"]:])
            if diagonal:
                a = jnp.where(u["causal"], a, 0.0)
            weights.append(a.astype(BF16))
            c[u["head"]] = add_from(c[u["head"]], u["lo"], jnp.sum(u["rest"], axis=0, keepdims=True))
        acc = [carry[1], carry[3]]
        for u, a in zip(units, weights):
            acc[u["head"]] = add_from(acc[u["head"]], u["lo"], _dot(u["vt"], a))
        return c[0], acc[0], c[1], acc[1]

    def alive(carry):
        return (jnp.min(jnp.minimum(carry[0], carry[2])) < -F32_EXP2_ZERO).astype(I32)

    zc = jnp.zeros((1, tq), F32)
    za = jnp.zeros((LANES, tq), F32)
    order = list(reversed(range(n_sub)))
    carry = blocks([i * n_sub + d for d in order], (zc, za, zc, za), diag=order)

    def cond(state):
        return jnp.logical_and(state[0] >= 0, state[1] > 0)

    def body(state):
        new = blocks([state[0], state[0] - 1], state[2:])
        return (state[0] - 2, alive(new)) + new

    out = lax.while_loop(cond, body, (i * n_sub - 1, alive(carry)) + carry)
    row = lax.broadcasted_iota(I32, (LANES, 1), 0)
    o = jnp.where(row < HEAD_DIM, out[3], out[5])
    o_ref[...] = o.T.astype(o_ref.dtype)


def _attn_c(qk, vt, *, batch, seq, heads, tq):
    tq = min(tq, seq)
    assert seq % tq == 0 and (tq // LANES) % 2 == 0
    pairs = heads // 2
    later = jnp.asarray(np.triu(np.ones((LANES, LANES), np.float32), 1), BF16)
    return pl.pallas_call(
        _attn_c_kernel,
        grid=(batch, pairs, seq // tq),
        in_specs=[
            pl.BlockSpec((None, tq, LANES), lambda b, h, i: (b, i, h)),
            pl.BlockSpec((None, seq, LANES), lambda b, h, i: (b, 0, pairs + h)),
            pl.BlockSpec((None, LANES, seq), lambda b, h, i: (b, h, 0)),
            pl.BlockSpec((LANES, LANES), lambda b, h, i: (0, 0)),
        ],
        out_specs=pl.BlockSpec((None, tq, LANES), lambda b, h, i: (b, i, h)),
        out_shape=jax.ShapeDtypeStruct((batch, seq, heads * HEAD_DIM), BF16),
        compiler_params=_params(("parallel", "parallel", "arbitrary")),
        name="stick_breaking_attn",
    )(qk, qk, vt, later)


def _router_kernel(x_ref, g_ref, whi_ref, wlo_ref, b_ref, u_ref, info_ref, cnt_ref, run_ref,
                   *, groups, per_group):
    G, E = groups, per_group
    tm = x_ref.shape[0]
    ne = G * E

    @pl.when(pl.program_id(0) == 0)
    def _():
        run_ref[...] = jnp.zeros_like(run_ref)

    xn = _rms(x_ref[...]) * g_ref[...]
    hi, lo = _split_bf16(xn)
    whi = whi_ref[...]
    lt = _dot_t(whi, hi) + _dot_t(whi, lo) + _dot_t(wlo_ref[...], hi) + b_ref[...]
    sub = lax.broadcasted_iota(I32, (E, tm), 0)
    coarse = lt[0:G]
    cmax = jnp.max(coarse, axis=0, keepdims=True)
    g_prob = 1.0 / jnp.sum(jnp.exp(coarse - cmax), axis=0, keepdims=True)
    g_idx = jnp.min(jnp.where(coarse == cmax, sub, G), axis=0, keepdims=True)
    fine = jnp.zeros((E, tm), F32)
    for grp in range(G):
        fine = jnp.where(g_idx == grp, lt[G + grp * E:G + (grp + 1) * E], fine)
    fmax = jnp.max(fine, axis=0, keepdims=True)
    fsum = jnp.sum(jnp.exp(fine - fmax), axis=0, keepdims=True)
    i1 = jnp.min(jnp.where(fine == fmax, sub, E), axis=0, keepdims=True)
    rest = jnp.where(sub == i1, -jnp.inf, fine)
    m2 = jnp.max(rest, axis=0, keepdims=True)
    i2 = jnp.min(jnp.where(rest == m2, sub, E), axis=0, keepdims=True)
    p1 = 1.0 / fsum
    p2 = jnp.exp(m2 - fmax) / fsum
    norm = p1 + p2
    gate1 = g_prob * (p1 / norm)
    gate2 = g_prob * (p2 / norm)
    e1 = g_idx * E + i1
    e2 = g_idx * E + i2
    ex = lax.broadcasted_iota(I32, (ne, tm), 0)
    oh1 = (ex == e1).astype(F32)
    oh2 = (ex == e2).astype(F32)
    both = oh1 + oh2
    earlier = _dot(both.astype(BF16), u_ref[...]) + run_ref[:, 0:1]
    r1 = jnp.sum(oh1 * earlier, axis=0, keepdims=True)
    r2 = jnp.sum(oh2 * earlier, axis=0, keepdims=True)
    total = run_ref[...] + jnp.sum(both, axis=1, keepdims=True)
    run_ref[...] = total
    cnt_ref[...] = total
    row = lax.broadcasted_iota(I32, (SUBLANES, tm), 0)
    fields = (e1.astype(F32), e2.astype(F32), r1, r2, gate1, gate2)
    info = jnp.zeros((SUBLANES, tm), F32)
    for idx, f in enumerate(fields):
        info = jnp.where(row == idx, f, info)
    info_ref[...] = info


def _router(h, g, w_coarse, b_coarse, w_fine, b_fine, *, tm):
    T, D = h.shape
    tm = min(tm, T)
    G, ne = w_coarse.shape[1], w_fine.shape[1]
    pad = LANES - G - ne
    wt = jnp.pad(jnp.concatenate([w_coarse, w_fine], axis=1), ((0, 0), (0, pad))).T
    b = jnp.pad(jnp.concatenate([b_coarse, b_fine]), (0, pad)).reshape(LANES, 1)
    whi, wlo = _split_bf16(wt)
    before = jnp.asarray(np.triu(np.ones((tm, tm), np.float32), 1), BF16)
    fixed = lambda i: (0, 0)
    return pl.pallas_call(
        functools.partial(_router_kernel, groups=G, per_group=ne // G),
        grid=(T // tm,),
        in_specs=[
            pl.BlockSpec((tm, D), lambda i: (i, 0)),
            pl.BlockSpec((1, D), fixed),
            pl.BlockSpec((LANES, D), fixed),
            pl.BlockSpec((LANES, D), fixed),
            pl.BlockSpec((LANES, 1), fixed),
            pl.BlockSpec((tm, tm), fixed),
        ],
        out_specs=[pl.BlockSpec((SUBLANES, tm), lambda i: (0, i)), pl.BlockSpec((ne, LANES), fixed)],
        out_shape=[jax.ShapeDtypeStruct((SUBLANES, T), F32), jax.ShapeDtypeStruct((ne, LANES), F32)],
        scratch_shapes=[pltpu.VMEM((ne, LANES), F32)],
        compiler_params=_params(("arbitrary",)),
        name="moe_router",
    )(h, g.reshape(1, D), whi, wlo, b, before)


def _plan(info, counts, *, tile):
    T = info.shape[1]
    ne = counts.shape[0]
    C = FFN_CHUNK
    experts = info[0:2].astype(I32)
    ranks = info[2:4].astype(I32)
    gates = info[4:6]
    cnt = counts[:, 0].astype(I32)
    padded = (cnt + C - 1) // C * C
    pad_end = jnp.cumsum(padded)
    pad_start = pad_end - padded
    ids = jnp.arange(ne, dtype=I32)
    start_of = jnp.sum(jnp.where(experts[:, :, None] == ids, pad_start, 0), axis=-1)
    dest = ranks + start_of
    n_chunks = (MOE_TOP_K * T + ne * (C - 1) + C - 1) // C
    n_used = (pad_end[-1] // C).astype(I32)
    first_row = jnp.minimum(jnp.arange(n_chunks, dtype=I32), n_used - 1) * C
    chunk_expert = jnp.minimum(jnp.sum(pad_end[None, :] <= first_row[:, None], axis=1), ne - 1).astype(I32)
    dest_tiles = dest.reshape(MOE_TOP_K, T // tile, tile).transpose(1, 0, 2).reshape(T // tile, 1, MOE_TOP_K * tile)
    return dest_tiles, gates.T, chunk_expert, n_used.reshape(1), n_chunks


def _dispatch_kernel(dst_ref, h_ref, g_ref, xs_in, xs_out, xbuf, sem):
    del xs_in
    i = pl.program_id(0)
    n = pl.num_programs(0)
    tm, D = h_ref.shape
    slot = i % 2

    def wait(s):
        for _ in range(MOE_TOP_K):
            pltpu.make_async_copy(xbuf.at[s], xs_out.at[pl.ds(0, tm)], sem.at[s]).wait()

    @pl.when(i >= 2)
    def _():
        wait(slot)

    xn = _rms(h_ref[...]) * g_ref[...]
    xbuf[slot] = _pack_bf16_pair(xn[:, :D // 2], xn[:, D // 2:])

    def issue(r, _):
        for k in range(MOE_TOP_K):
            pltpu.make_async_copy(xbuf.at[slot, pl.ds(r, 1)], xs_out.at[pl.ds(dst_ref[0, 0, k * tm + r], 1)],
                                  sem.at[slot]).start(priority=k % 2)
        return 0

    lax.fori_loop(0, tm, issue, 0, unroll=True)

    @pl.when(i == n - 1)
    def _():
        wait(slot)

        @pl.when(i >= 1)
        def _():
            wait(1 - slot)


def _dispatch(h, g, dest_tiles, n_rows, *, tm):
    T, D = h.shape
    xs0 = jnp.zeros((n_rows, D // 2), U32)
    return pl.pallas_call(
        _dispatch_kernel,
        grid=(T // tm,),
        in_specs=[
            pl.BlockSpec((1, 1, MOE_TOP_K * tm), lambda i: (i, 0, 0), memory_space=pltpu.SMEM),
            pl.BlockSpec((tm, D), lambda i: (i, 0)),
            pl.BlockSpec((1, D), lambda i: (0, 0)),
            pl.BlockSpec(memory_space=pl.ANY),
        ],
        out_specs=pl.BlockSpec(memory_space=pl.ANY),
        out_shape=jax.ShapeDtypeStruct((n_rows, D // 2), U32),
        scratch_shapes=[pltpu.VMEM((2, tm, D // 2), U32), pltpu.SemaphoreType.DMA((2,))],
        input_output_aliases={3: 0},
        compiler_params=_params(("arbitrary",)),
        name="moe_dispatch",
    )(dest_tiles, h, g.reshape(1, D), xs0)


def _ffn_kernel(ce_ref, nu_ref, x_ref, wgu_ref, wdn_ref, y_ref, wgu_bf, wdn_bf, *, d_expert):
    c = pl.program_id(0)
    n_used = nu_ref[0]

    @pl.when(c < n_used)
    def _():
        @pl.when(jnp.logical_or(c == 0, ce_ref[c] != ce_ref[jnp.maximum(c - 1, 0)]))
        def _():
            wgu_bf[...] = wgu_ref[...].astype(BF16)
            wdn_bf[...] = wdn_ref[...].astype(BF16)

        half = x_ref.shape[1]
        xa, xb = _unpack_bf16_pair(x_ref[...])
        n_blk = 2
        w = d_expert // n_blk
        proj = lambda lo: _dot(xa, wgu_bf[:half, lo:lo + w]) + _dot(xb, wgu_bf[half:, lo:lo + w])
        gates = [proj(b * w) for b in range(n_blk)]
        ups = [proj(d_expert + b * w) for b in range(n_blk)]
        acts = [((g * jax.nn.sigmoid(g)) * u).astype(BF16) for g, u in zip(gates, ups)]
        y = _dot(acts[0], wdn_bf[:w])
        for b in range(1, n_blk):
            y = y + _dot(acts[b], wdn_bf[b * w:(b + 1) * w])
        y_ref[...] = y

    @pl.when(c >= n_used)
    def _():
        y_ref[...] = jnp.zeros_like(y_ref)


def _ffn(xs, chunk_expert, n_used, w_gate_up, w_down, layer):
    n_rows, half = xs.shape
    C = FFN_CHUNK
    n_chunks = n_rows // C
    D, F2 = w_gate_up.shape[2:]
    d_expert = w_down.shape[2]
    return pl.pallas_call(
        functools.partial(_ffn_kernel, d_expert=d_expert),
        grid_spec=pltpu.PrefetchScalarGridSpec(
            num_scalar_prefetch=2,
            grid=(n_chunks,),
            in_specs=[
                pl.BlockSpec((C, half), lambda c, ce, nu: (jnp.minimum(c, nu[0] - 1), 0)),
                pl.BlockSpec((None, None, D, F2), lambda c, ce, nu: (layer, ce[c], 0, 0)),
                pl.BlockSpec((None, None, d_expert, D), lambda c, ce, nu: (layer, ce[c], 0, 0)),
            ],
            out_specs=pl.BlockSpec((C, D), lambda c, ce, nu: (c, 0)),
            scratch_shapes=[pltpu.VMEM((D, F2), BF16), pltpu.VMEM((d_expert, D), BF16)],
        ),
        out_shape=jax.ShapeDtypeStruct((n_rows, D), F32),
        compiler_params=_params(("arbitrary",)),
        name="moe_expert_ffn",
    )(chunk_expert, n_used, xs, w_gate_up, w_down)


def _combine_ple_kernel(dst_ref, dstn_ref, h_ref, gt_ref, p_ref, g_ref, wp_ref, wg_ref, gf_ref, y_hbm,
                        o_ref, ybuf, sem, *, final):
    i = pl.program_id(0)
    n = pl.num_programs(0)
    tm, D = h_ref.shape
    rows = MOE_TOP_K * tm
    slot = i % 2

    def gather(dref, s):
        def issue(r, _):
            for k in range(MOE_TOP_K):
                pltpu.make_async_copy(y_hbm.at[pl.ds(dref[0, 0, k * tm + r], 1)],
                                      ybuf.at[s, pl.ds(k * tm + r, 1)], sem.at[s]).start()
            return 0
        lax.fori_loop(0, tm, issue, 0, unroll=True)

    @pl.when(i == 0)
    def _():
        gather(dst_ref, 0)

    def wait(s):
        pltpu.make_async_copy(y_hbm.at[pl.ds(0, rows)], ybuf.at[s], sem.at[s]).wait()

    wait(slot)
    gather(dstn_ref, 1 - slot)
    gt = gt_ref[...]
    h = h_ref[...] + (gt[:, 0:1] * ybuf[slot, :tm] + gt[:, 1:2] * ybuf[slot, tm:])
    xn = (_rms(h) * g_ref[...]).astype(BF16)
    gate = jax.nn.sigmoid(_dot(xn, wg_ref[...]))
    out = h + _dot(p_ref[...].astype(BF16), wp_ref[...]) * gate
    if final:
        out = _rms(out) * gf_ref[...]
    o_ref[...] = out

    @pl.when(i == n - 1)
    def _():
        wait(1 - slot)


def _combine_ple(h, y, dest_tiles, gates, p, layer, g, w_proj, w_gate, g_final, *, final, tm):
    T, D = h.shape
    Pd = p.shape[2]
    n_tiles = T // tm
    row = lambda i: (i, 0)
    fixed = lambda i: (0, 0)
    smem = functools.partial(pl.BlockSpec, (1, 1, MOE_TOP_K * tm), memory_space=pltpu.SMEM)
    return pl.pallas_call(
        functools.partial(_combine_ple_kernel, final=final),
        grid=(n_tiles,),
        in_specs=[
            smem(lambda i: (i, 0, 0)),
            smem(lambda i: (jnp.minimum(i + 1, n_tiles - 1), 0, 0)),
            pl.BlockSpec((tm, D), row),
            pl.BlockSpec((tm, MOE_TOP_K), row),
            pl.BlockSpec((None, tm, Pd), lambda i: (layer, i, 0)),
            pl.BlockSpec((1, D), fixed),
            pl.BlockSpec((Pd, D), fixed),
            pl.BlockSpec((D, D), fixed),
            pl.BlockSpec((1, D), fixed),
            pl.BlockSpec(memory_space=pl.ANY),
        ],
        out_specs=pl.BlockSpec((tm, D), row),
        out_shape=jax.ShapeDtypeStruct((T, D), F32),
        scratch_shapes=[pltpu.VMEM((2, MOE_TOP_K * tm, D), F32), pltpu.SemaphoreType.DMA((2,))],
        compiler_params=_params(("arbitrary",)),
        name="moe_combine_ple",
    )(dest_tiles, dest_tiles, h, gates, p, g.reshape(1, D), w_proj, w_gate, g_final.reshape(1, D), y)


def _moe_ple(h, p, layer, g_ffn, w_coarse, b_coarse, w_fine, b_fine, w_gate_up, w_down, g_ple, w_proj, w_gate,
             g_final, *, final):
    tile = min(512, h.shape[0])
    info, counts = _router(h, g_ffn, w_coarse, b_coarse, w_fine, b_fine, tm=min(512, h.shape[0]))
    dest_tiles, gates, chunk_expert, n_used, n_chunks = _plan(info, counts, tile=tile)
    xs = _dispatch(h, g_ffn, dest_tiles, n_chunks * FFN_CHUNK, tm=tile)
    y = _ffn(xs, chunk_expert, n_used, w_gate_up, w_down, layer)
    return _combine_ple(h, y, dest_tiles, gates, p, layer, g_ple, w_proj.astype(BF16), w_gate.astype(BF16),
                        g_final, final=final, tm=tile)


def _mixer_a(h, g, w_in, w_out, *, batch, seq):
    T, D = h.shape
    width = A_HEADS * HEAD_DIM
    n_groups = len(A_PATTERNS)
    w = w_in.astype(BF16)
    wvt = jnp.swapaxes(w.reshape(D, n_groups, 3, width)[:, :, 2], 0, 1).swapaxes(1, 2)
    xn = _norm(h, g, tm=1024)
    outs, lses = [], []
    for grp, (window, r) in enumerate(A_PATTERNS):
        assert window // r == ATTN_BLOCK and seq % (r * ATTN_BLOCK) == 0
        L = seq // r
        to_streams = lambda a: a.reshape(batch, L, r, -1).transpose(0, 2, 1, 3).reshape(T, -1)
        from_streams = lambda a: a.reshape(batch, r, L, -1).transpose(0, 2, 1, 3).reshape(T, -1)
        xs = xn if r == 1 else to_streams(xn)
        qk, vt = _proj_a(xs, w, wvt, group=grp, length=L, width=width, tm=512,
                         q_scale=LOG2_E / math.sqrt(HEAD_DIM))
        o, lse = _attn_a_t_group(qk, vt, dilation=r, heads=A_HEADS)
        outs.append(from_streams(o))
        lses.append(from_streams(lse))
    return _merge_proj(outs, lses, w_out.astype(BF16), h, heads=A_HEADS, tm=512)


def _mixer_b(h, g, w_in, w_out, lq1, lk1, lq2, lk2, subln, lambda_init, *, batch, seq):
    D = h.shape[1]
    heads = D // (2 * HEAD_DIM)
    qk, vt = _proj_qk_vt(h, g, w_in, batch=batch, seq=seq, tm=512, q_scale=LOG2_E / math.sqrt(HEAD_DIM))
    o = _attn_b(qk, vt, lq1, lk1, lq2, lk2, subln, batch=batch, seq=seq, heads=heads,
                lambda_init=lambda_init, t=256, tq=512)
    return _mm_res(o.reshape(batch * seq, -1), w_out.astype(BF16), h, tm=512)


def _mixer_c(h, g, w_in, w_out, *, batch, seq):
    D = h.shape[1]
    heads = D // HEAD_DIM
    qk, vt = _proj_qk_vt(h, g, w_in, batch=batch, seq=seq, tm=512, q_scale=LOG2_E / math.sqrt(HEAD_DIM))
    o = _attn_c(qk, vt, batch=batch, seq=seq, heads=heads, tq=512)
    return _mm_res(o.reshape(batch * seq, -1), w_out.astype(BF16), h, tm=512)


def kernel(x, p, norm_mix, norm_ffn, norm_ple, norm_final, a_w_in, a_w_out, b_w_in, b_w_out,
           b_lambda_q1, b_lambda_k1, b_lambda_q2, b_lambda_k2, b_subln, c_w_in, c_w_out,
           moe_w_coarse, moe_b_coarse, moe_w_fine, moe_b_fine, moe_w_gate_up, moe_w_down,
           ple_w_proj, ple_w_gate):
    batch, seq, D = x.shape
    depth = p.shape[0]
    T = batch * seq
    h = x.reshape(T, D)
    p_rows = p.reshape(depth, T, -1)
    for i in range(depth):
        kind = i % N_MIXERS
        j = i // N_MIXERS
        if kind == 0:
            h = _mixer_a(h, norm_mix[i], a_w_in[j], a_w_out[j], batch=batch, seq=seq)
        elif kind == 1:
            lambda_init = 0.8 - 0.6 * math.exp(-0.3 * i)
            h = _mixer_b(h, norm_mix[i], b_w_in[j], b_w_out[j], b_lambda_q1[j], b_lambda_k1[j],
                         b_lambda_q2[j], b_lambda_k2[j], b_subln[j], lambda_init, batch=batch, seq=seq)
        else:
            h = _mixer_c(h, norm_mix[i], c_w_in[j], c_w_out[j], batch=batch, seq=seq)
        h = _moe_ple(h, p_rows, i, norm_ffn[i], moe_w_coarse[i], moe_b_coarse[i], moe_w_fine[i],
                     moe_b_fine[i], moe_w_gate_up, moe_w_down, norm_ple[i], ple_w_proj[i],
                     ple_w_gate[i], norm_final, final=(i == depth - 1))
    return h.reshape(batch, seq, D)
```
